```python
import math
import jax, jax.numpy as jnp
from jax import lax
import numpy as np

D_MODEL = 4096
BATCH = 4
SEQ = 2048
DEPTH = 1
DEC_BATCH = 32
DEC_SEQ = 4
PAST_LEN = 8192
PAGE_SIZE = 128

D_POOL = D_MODEL // 2
POOL_WINDOWS = (2, 4, 8, 16)
N_POOL_GROUPS = len(POOL_WINDOWS)
POOL_GROUP = D_POOL // N_POOL_GROUPS
POOL_HIST = max(POOL_WINDOWS) - 1
HEAD_DIM = 128
N_HEADS = (D_MODEL - D_POOL) // HEAD_DIM
N_KV_HEADS = 4
GROUP = N_HEADS // N_KV_HEADS
D_ATTN = N_HEADS * HEAD_DIM
D_KV = N_KV_HEADS * HEAD_DIM
IDX_HEADS = 8
IDX_DIM = 128
TOPK_MAX = 256
Q_BLOCK = 128
NUM_BUCKETS = 32
MAX_DISTANCE = 128
D_FF = 256 * ((8 * D_MODEL // 3 + 255) // 256)
EPS = 1e-6
NEG = -1e30
SPLITS = (D_POOL,
          D_POOL + D_ATTN,
          D_POOL + D_ATTN + D_KV,
          D_POOL + D_ATTN + 2 * D_KV,
          D_POOL + D_ATTN + 2 * D_KV + IDX_HEADS * IDX_DIM,
          D_POOL + D_ATTN + 2 * D_KV + IDX_HEADS * IDX_DIM + IDX_DIM)
D_IN = SPLITS[-1] + IDX_HEADS

kernel_name = "hymba_pool_dsa_macaron_step"


def rmsnorm(x, g):
    xf = x.astype(jnp.float32)
    y = xf * lax.rsqrt(jnp.mean(xf * xf, axis=-1, keepdims=True) + EPS)
    return (y * g.astype(jnp.float32)).astype(x.dtype)


def swiglu_half(x, g, w_gate, w_up, w_down):
    h = rmsnorm(x, g)
    return x + 0.5 * ((jax.nn.silu(h @ w_gate) * (h @ w_up)) @ w_down)


def project(x, g_mix, w_in, g_q, g_k, g_idx_k):
    b, t = x.shape[:2]
    u = rmsnorm(x, g_mix) @ w_in
    u_pool, q, k, v, qi, ki, wi = jnp.split(u, SPLITS, axis=-1)
    q = rmsnorm(q.reshape(b, t, N_HEADS, HEAD_DIM), g_q)
    k = rmsnorm(k.reshape(b, t, N_KV_HEADS, HEAD_DIM), g_k)
    v = v.reshape(b, t, N_KV_HEADS, HEAD_DIM)
    qi = qi.reshape(b, t, IDX_HEADS, IDX_DIM)
    ki = rmsnorm(ki, g_idx_k)
    return u_pool, q, k, v, qi, ki, wi


def pool_mixer(u_new, hist, pos, w_pool, pool_scale):
    b, t = u_new.shape[:2]
    u_all = jnp.concatenate([hist, u_new], axis=1).astype(jnp.float32)
    cs = jnp.concatenate([jnp.zeros((b, 1, D_POOL), jnp.float32), jnp.cumsum(u_all, axis=1)], axis=1)
    end = cs[:, POOL_HIST + 1:]
    parts = []
    for g, w in enumerate(POOL_WINDOWS):
        lo, hi = g * POOL_GROUP, (g + 1) * POOL_GROUP
        start = cs[:, POOL_HIST + 1 - w:POOL_HIST + 1 - w + t, lo:hi]
        count = jnp.minimum(pos + 1, w).astype(jnp.float32)[None, :, None]
        parts.append((end[..., lo:hi] - start) / count - u_all[:, POOL_HIST:, lo:hi])
    d = jnp.stack(parts, axis=2)
    out = jnp.einsum('btgc,gcd->btgd', d, w_pool.astype(jnp.float32)).reshape(b, t, D_POOL)
    return (out * pool_scale.astype(jnp.float32)).astype(u_new.dtype)


def t5_bucket(rel):
    n = jnp.maximum(rel, 0)
    max_exact = NUM_BUCKETS // 2
    nf = jnp.maximum(n, 1).astype(jnp.float32)
    large = max_exact + (jnp.log(nf / max_exact) / math.log(MAX_DISTANCE / max_exact)
                         * (NUM_BUCKETS - max_exact)).astype(jnp.int32)
    large = jnp.minimum(large, NUM_BUCKETS - 1)
    return jnp.where(n < max_exact, n, large)


def indexer_scores(qi, wi, ki, q_pos):
    s = jnp.einsum('bthd,bsd->bths', qi.astype(jnp.float32), ki.astype(jnp.float32)) * IDX_DIM ** -0.5
    score = jnp.einsum('bths,bth->bts', jax.nn.relu(s), wi.astype(jnp.float32) * IDX_HEADS ** -0.5)
    admissible = jnp.arange(ki.shape[1])[None, :] <= q_pos[:, None]
    return jnp.where(admissible[None], score, -jnp.inf)


def gather_rows(rows, idx):
    return jax.vmap(lambda r, i: r[i])(rows, idx)


def sparse_attend(q, sel, k_sel, v_sel, q_pos, rel_bias):
    b, tq = q.shape[:2]
    topk = sel.shape[-1]
    qg = q.reshape(b, tq, N_KV_HEADS, GROUP, HEAD_DIM).astype(jnp.float32)
    logits = jnp.einsum('btngd,btknd->btkng', qg, k_sel.astype(jnp.float32)) * HEAD_DIM ** -0.5
    rel = q_pos[None, :, None] - sel
    bias = rel_bias[t5_bucket(rel)].astype(jnp.float32)
    logits = logits + bias.reshape(b, tq, topk, N_KV_HEADS, GROUP)
    logits = jnp.where((rel >= 0)[..., None, None], logits, NEG)
    p = jax.nn.softmax(logits, axis=2)
    o = jnp.einsum('btkng,btknd->btngd', p, v_sel.astype(jnp.float32))
    return o.reshape(b, tq, D_ATTN).astype(q.dtype)


def prompt_attention(q, k, v, qi, ki, wi, rel_bias):
    b, s = q.shape[:2]
    topk = min(TOPK_MAX, s // 4)
    n_blocks = s // Q_BLOCK

    def block(i):
        start = i * Q_BLOCK
        qb = lax.dynamic_slice_in_dim(q, start, Q_BLOCK, axis=1)
        qib = lax.dynamic_slice_in_dim(qi, start, Q_BLOCK, axis=1)
        wib = lax.dynamic_slice_in_dim(wi, start, Q_BLOCK, axis=1)
        q_pos = start + jnp.arange(Q_BLOCK)
        _, sel = lax.top_k(indexer_scores(qib, wib, ki, q_pos), topk)
        return sparse_attend(qb, sel, gather_rows(k, sel), gather_rows(v, sel), q_pos, rel_bias)

    out = lax.map(block, jnp.arange(n_blocks))
    return out.transpose(1, 0, 2, 3).reshape(b, s, D_ATTN)


def sample_attention(q, k, v, qi, ki, wi, cache_k, cache_v, cache_idx_k, page_table, rel_bias):
    bd, t = q.shape[:2]
    n_pages = PAST_LEN // PAGE_SIZE
    topk = min(TOPK_MAX, (PAST_LEN + t) // 4)
    past_ki = cache_idx_k[page_table].reshape(bd, PAST_LEN, IDX_DIM)
    keys_i = jnp.concatenate([past_ki, ki.astype(past_ki.dtype)], axis=1)
    q_pos = PAST_LEN + jnp.arange(t)
    _, sel = lax.top_k(indexer_scores(qi, wi, keys_i, q_pos), topk)
    in_past = (sel < PAST_LEN)[..., None, None]
    lp = jnp.minimum(sel // PAGE_SIZE, n_pages - 1).reshape(bd, -1)
    page = jnp.take_along_axis(page_table, lp, axis=1).reshape(sel.shape)
    off = sel % PAGE_SIZE
    new_idx = jnp.clip(sel - PAST_LEN, 0, t - 1)
    k_sel = jnp.where(in_past, cache_k[page, off], gather_rows(k, new_idx).astype(cache_k.dtype))
    v_sel = jnp.where(in_past, cache_v[page, off], gather_rows(v, new_idx).astype(cache_v.dtype))
    return sparse_attend(q, sel, k_sel, v_sel, q_pos, rel_bias)


def setup_inputs(seed: int = 0) -> dict:
    key = jax.random.key(seed)
    ks = jax.random.split(key, 32)
    n_pages = PAST_LEN // PAGE_SIZE
    n_used = DEC_BATCH * n_pages
    n_phys = (n_used * 5 + 3) // 4

    def nrm(k, shape, scale=1.0):
        return jax.random.normal(k, shape, jnp.float32) * scale

    def gain(k, shape):
        return 1.0 + 0.02 * jax.random.normal(k, shape, jnp.float32)

    page_table = jax.random.permutation(ks[6], n_phys)[:n_used].reshape(DEC_BATCH, n_pages).astype(jnp.int32)
    return {
        "x_prompt": nrm(ks[0], (BATCH, SEQ, D_MODEL)),
        "x_sample": nrm(ks[1], (DEC_BATCH, DEC_SEQ, D_MODEL)),
        "cache_k": nrm(ks[2], (DEPTH, n_phys, PAGE_SIZE, N_KV_HEADS, HEAD_DIM)),
        "cache_v": nrm(ks[3], (DEPTH, n_phys, PAGE_SIZE, N_KV_HEADS, HEAD_DIM)),
        "cache_idx_k": nrm(ks[4], (DEPTH, n_phys, PAGE_SIZE, IDX_DIM)),
        "state_pool": nrm(ks[5], (DEPTH, DEC_BATCH, POOL_HIST, D_POOL)),
        "page_table": page_table,
        "g_ffn1": gain(ks[7], (DEPTH, D_MODEL)),
        "w_gate1": nrm(ks[8], (DEPTH, D_MODEL, D_FF), D_MODEL ** -0.5),
        "w_up1": nrm(ks[9], (DEPTH, D_MODEL, D_FF), D_MODEL ** -0.5),
        "w_down1": nrm(ks[10], (DEPTH, D_FF, D_MODEL), D_FF ** -0.5),
        "g_mix": gain(ks[11], (DEPTH, D_MODEL)),
        "w_in": nrm(ks[12], (DEPTH, D_MODEL, D_IN), D_MODEL ** -0.5),
        "g_q": gain(ks[13], (DEPTH, HEAD_DIM)),
        "g_k": gain(ks[14], (DEPTH, HEAD_DIM)),
        "g_idx_k": gain(ks[15], (DEPTH, IDX_DIM)),
        "w_pool": nrm(ks[16], (DEPTH, N_POOL_GROUPS, POOL_GROUP, POOL_GROUP), POOL_GROUP ** -0.5),
        "pool_scale": gain(ks[17], (DEPTH, D_POOL)),
        "w_out": nrm(ks[18], (DEPTH, D_POOL + D_ATTN, D_MODEL), (D_POOL + D_ATTN) ** -0.5),
        "rel_bias": nrm(ks[19], (NUM_BUCKETS, N_HEADS), 0.5),
        "g_ffn2": gain(ks[20], (DEPTH, D_MODEL)),
        "w_gate2": nrm(ks[21], (DEPTH, D_MODEL, D_FF), D_MODEL ** -0.5),
        "w_up2": nrm(ks[22], (DEPTH, D_MODEL, D_FF), D_MODEL ** -0.5),
        "w_down2": nrm(ks[23], (DEPTH, D_FF, D_MODEL), D_FF ** -0.5),
    }


def reference(x_prompt, x_sample, cache_k, cache_v, cache_idx_k, state_pool, page_table,
              g_ffn1, w_gate1, w_up1, w_down1, g_mix, w_in, g_q, g_k, g_idx_k,
              w_pool, pool_scale, w_out, rel_bias, g_ffn2, w_gate2, w_up2, w_down2):
    xp, xs = x_prompt, x_sample
    b, s = xp.shape[:2]
    t = xs.shape[1]
    kp_l, vp_l, ikp_l, pp_l, ks_l, vs_l, iks_l, ps_l = [], [], [], [], [], [], [], []
    for l in range(DEPTH):
        xp = swiglu_half(xp, g_ffn1[l], w_gate1[l], w_up1[l], w_down1[l])
        xs = swiglu_half(xs, g_ffn1[l], w_gate1[l], w_up1[l], w_down1[l])

        u_pool, q, k, v, qi, ki, wi = project(xp, g_mix[l], w_in[l], g_q[l], g_k[l], g_idx_k[l])
        hist0 = jnp.zeros((b, POOL_HIST, D_POOL), u_pool.dtype)
        pool_out = pool_mixer(u_pool, hist0, jnp.arange(s), w_pool[l], pool_scale[l])
        attn_out = prompt_attention(q, k, v, qi, ki, wi, rel_bias)
        xp = xp + jnp.concatenate([pool_out, attn_out], axis=-1) @ w_out[l]
        kp_l.append(k); vp_l.append(v); ikp_l.append(ki); pp_l.append(u_pool[:, s - POOL_HIST:])

        u_pool_s, q_s, k_s, v_s, qi_s, ki_s, wi_s = project(xs, g_mix[l], w_in[l], g_q[l], g_k[l], g_idx_k[l])
        hist = state_pool[l].astype(u_pool_s.dtype)
        pool_out_s = pool_mixer(u_pool_s, hist, PAST_LEN + jnp.arange(t), w_pool[l], pool_scale[l])
        attn_out_s = sample_attention(q_s, k_s, v_s, qi_s, ki_s, wi_s, cache_k[l], cache_v[l],
                                      cache_idx_k[l], page_table, rel_bias)
        xs = xs + jnp.concatenate([pool_out_s, attn_out_s], axis=-1) @ w_out[l]
        ks_l.append(k_s); vs_l.append(v_s); iks_l.append(ki_s)
        ps_l.append(jnp.concatenate([hist, u_pool_s], axis=1)[:, t:])

        xp = swiglu_half(xp, g_ffn2[l], w_gate2[l], w_up2[l], w_down2[l])
        xs = swiglu_half(xs, g_ffn2[l], w_gate2[l], w_up2[l], w_down2[l])

    return (xp, xs,
            jnp.stack(kp_l), jnp.stack(vp_l), jnp.stack(ikp_l), jnp.stack(pp_l),
            jnp.stack(ks_l), jnp.stack(vs_l), jnp.stack(iks_l), jnp.stack(ps_l))
```

```python
import functools
import math

import jax
import jax.numpy as jnp
import numpy as np
from jax import lax
from jax.experimental import pallas as pl
from jax.experimental.pallas import tpu as pltpu

F32 = jnp.float32
BF16 = jnp.bfloat16
I32 = jnp.int32

SUBLANES = 8
LANES = 128
VMEM_LIMIT_BYTES = 56 * 1024 * 1024

D_MODEL = 4096
D_POOL = D_MODEL // 2
POOL_WINDOWS = (2, 4, 8, 16)
N_POOL_GROUPS = len(POOL_WINDOWS)
POOL_GROUP = D_POOL // N_POOL_GROUPS
POOL_HIST = max(POOL_WINDOWS) - 1
HEAD_DIM = 128
N_HEADS = (D_MODEL - D_POOL) // HEAD_DIM
N_KV_HEADS = 4
GROUP = N_HEADS // N_KV_HEADS
D_ATTN = N_HEADS * HEAD_DIM
D_KV = N_KV_HEADS * HEAD_DIM
IDX_HEADS = 8
IDX_DIM = 128
TOPK_MAX = 256
NUM_BUCKETS = 32
MAX_DISTANCE = 128
EPS = 1e-6
NEG = -1e30

C_Q = D_POOL
C_K = C_Q + D_ATTN
C_V = C_K + D_KV
C_QI = C_V + D_KV
C_KI = C_QI + IDX_HEADS * IDX_DIM
C_WI = C_KI + IDX_DIM
D_IN = C_WI + IDX_HEADS
D_IN_PAD = ((D_IN + 2 * LANES - 1) // (2 * LANES)) * (2 * LANES)
PROJ_TN = 2 * LANES

INT_MIN = -(2 ** 31)
KEY_NEG_INF = INT_MIN + 0x007FFFFF


def _compiler_params(semantics):
    return pltpu.CompilerParams(dimension_semantics=semantics, vmem_limit_bytes=VMEM_LIMIT_BYTES)


def _dot(a, b):
    return jnp.dot(a, b, preferred_element_type=F32)


def _dot_nt(a, b):
    return lax.dot_general(a, b, (((1,), (1,)), ((), ())), preferred_element_type=F32)


def _rms_scale(x):
    return lax.rsqrt(jnp.mean(x * x, axis=-1, keepdims=True) + EPS)


def _split_bf16(x):
    hi = x.astype(BF16)
    lo = (x - hi.astype(F32)).astype(BF16)
    return hi, lo


def _ffn_kernel(x_ref, g_ref, wg_ref, wu_ref, wd_ref, o_ref, h_ref):
    j = pl.program_id(1)

    @pl.when(j == 0)
    def _():
        x = x_ref[...]
        h_ref[...] = (x * _rms_scale(x) * g_ref[...]).astype(BF16)
        o_ref[...] = jnp.zeros_like(o_ref)

    h = h_ref[...]
    a = _dot(h, wg_ref[...])
    b = _dot(h, wu_ref[...])
    s = (a * jax.nn.sigmoid(a) * b).astype(BF16)
    o_ref[...] += _dot(s, wd_ref[...])

    @pl.when(j == pl.num_programs(1) - 1)
    def _():
        o_ref[...] = x_ref[...] + 0.5 * o_ref[...]


def _ffn(x, g, wg, wu, wd, tm, tf):
    m, d = x.shape
    f = wg.shape[1]
    return pl.pallas_call(
        _ffn_kernel,
        out_shape=jax.ShapeDtypeStruct((m, d), F32),
        grid=(m // tm, f // tf),
        in_specs=[
            pl.BlockSpec((tm, d), lambda i, j: (i, 0), pipeline_mode=pl.Buffered(1)),
            pl.BlockSpec((1, d), lambda i, j: (0, 0)),
            pl.BlockSpec((d, tf), lambda i, j: (0, j)),
            pl.BlockSpec((d, tf), lambda i, j: (0, j)),
            pl.BlockSpec((tf, d), lambda i, j: (j, 0)),
        ],
        out_specs=pl.BlockSpec((tm, d), lambda i, j: (i, 0)),
        scratch_shapes=[pltpu.VMEM((tm, d), BF16)],
        compiler_params=_compiler_params(("parallel", "arbitrary")),
        name="swiglu_half",
    )(x, g.reshape(1, d), wg, wu, wd)


def _proj_kernel(x_ref, g_ref, w_ref, gain_ref, o_ref, h_ref):
    j = pl.program_id(1)

    @pl.when(j == 0)
    def _():
        x = x_ref[...]
        h_ref[...] = (x * _rms_scale(x) * g_ref[...]).astype(BF16)

    u = _dot(h_ref[...], w_ref[...])
    gain = gain_ref[...]

    def normed(lo):
        part = u[:, lo:lo + LANES]
        return part * _rms_scale(part) * gain[:, lo:lo + LANES]

    is_qk = (j >= C_Q // PROJ_TN) & (j < C_V // PROJ_TN)
    is_ki = j == C_KI // PROJ_TN

    @pl.when(is_qk)
    def _():
        o_ref[:, :LANES] = normed(0)
        o_ref[:, LANES:] = normed(LANES)

    @pl.when(is_ki)
    def _():
        o_ref[:, :LANES] = normed(0)
        o_ref[:, LANES:] = u[:, LANES:]

    @pl.when(jnp.logical_not(is_qk | is_ki))
    def _():
        o_ref[...] = u


def _proj(x, g_mix, w_in, gain, tm):
    m, d = x.shape
    n = w_in.shape[1]
    return pl.pallas_call(
        _proj_kernel,
        out_shape=jax.ShapeDtypeStruct((m, n), F32),
        grid=(m // tm, n // PROJ_TN),
        in_specs=[
            pl.BlockSpec((tm, d), lambda i, j: (i, 0)),
            pl.BlockSpec((1, d), lambda i, j: (0, 0)),
            pl.BlockSpec((d, PROJ_TN), lambda i, j: (0, j)),
            pl.BlockSpec((1, PROJ_TN), lambda i, j: (0, j)),
        ],
        out_specs=pl.BlockSpec((tm, PROJ_TN), lambda i, j: (i, j)),
        scratch_shapes=[pltpu.VMEM((tm, d), BF16)],
        compiler_params=_compiler_params(("parallel", "arbitrary")),
        name="in_proj",
    )(x, g_mix.reshape(1, d), w_in, gain)


HALO = 2 * SUBLANES


def _pool_groups(ext_rows, cur, count_of, wp_ref, scale_ref, store):
    for g, w in enumerate(POOL_WINDOWS):
        lo, hi = g * POOL_GROUP, (g + 1) * POOL_GROUP
        acc = ext_rows(0, lo, hi)
        for d in range(1, w):
            acc = acc + ext_rows(d, lo, hi)
        diff = acc / count_of(w) - cur(lo, hi)
        out = _dot(diff.astype(BF16), wp_ref[g]) * scale_ref[:, lo:hi]
        store(lo, hi, out)


def _pool_prompt_kernel(u_ref, halo_ref, wp_ref, scale_ref, o_ref, ext_ref, *, tp):
    i = pl.program_id(1)
    halo = halo_ref[...]
    ext_ref[:HALO, :] = jnp.where(i == 0, jnp.zeros_like(halo), halo)
    ext_ref[HALO:, :] = u_ref[...]
    pos = i * tp + lax.broadcasted_iota(I32, (tp, POOL_GROUP), 0)

    def store(lo, hi, out):
        o_ref[:, lo:hi] = out.astype(o_ref.dtype)

    _pool_groups(
        lambda d, lo, hi: ext_ref[HALO - d:HALO - d + tp, lo:hi],
        lambda lo, hi: u_ref[:, lo:hi],
        lambda w: jnp.minimum(pos + 1, w).astype(F32),
        wp_ref, scale_ref, store)


def _pool_prompt(u_all, wp, scale, batch, seq, tp):
    nt = seq // tp
    kern = functools.partial(_pool_prompt_kernel, tp=tp)
    return pl.pallas_call(
        kern,
        out_shape=jax.ShapeDtypeStruct((batch * seq, D_POOL), BF16),
        grid=(batch, nt),
        in_specs=[
            pl.BlockSpec((tp, D_POOL), lambda b, i: (b * nt + i, 0)),
            pl.BlockSpec((HALO, D_POOL),
                         lambda b, i: (jnp.maximum((b * seq + i * tp) // HALO - 1, 0), 0)),
            pl.BlockSpec((N_POOL_GROUPS, POOL_GROUP, POOL_GROUP), lambda b, i: (0, 0, 0)),
            pl.BlockSpec((1, D_POOL), lambda b, i: (0, 0)),
        ],
        out_specs=pl.BlockSpec((tp, D_POOL), lambda b, i: (b * nt + i, 0)),
        scratch_shapes=[pltpu.VMEM((HALO + tp, D_POOL), F32)],
        compiler_params=_compiler_params(("parallel", "arbitrary")),
        name="pool_prompt",
    )(u_all, u_all, wp, scale)


def _pool_sample_kernel(ext_ref, wp_ref, scale_ref, o_ref, *, t_new):
    for t in range(t_new):
        def store(lo, hi, out, t=t):
            o_ref[t, :, lo:hi] = out.astype(o_ref.dtype)

        _pool_groups(
            lambda d, lo, hi, t=t: ext_ref[POOL_HIST + t - d, :, lo:hi],
            lambda lo, hi, t=t: ext_ref[POOL_HIST + t, :, lo:hi],
            lambda w: float(w),
            wp_ref, scale_ref, store)


def _pool_sample(ext_t, wp, scale, t_new):
    _, bd, _ = ext_t.shape
    kern = functools.partial(_pool_sample_kernel, t_new=t_new)
    return pl.pallas_call(
        kern,
        out_shape=jax.ShapeDtypeStruct((t_new, bd, D_POOL), BF16),
        compiler_params=pltpu.CompilerParams(vmem_limit_bytes=VMEM_LIMIT_BYTES),
        name="pool_sample",
    )(ext_t, wp, scale)


def _bucket_thresholds():
    n = np.arange(0, 4 * MAX_DISTANCE, dtype=np.int64)
    max_exact = NUM_BUCKETS // 2
    nf = np.maximum(n, 1).astype(np.float32)
    large = max_exact + (np.log(nf / np.float32(max_exact)) / np.float32(math.log(MAX_DISTANCE / max_exact))
                         * np.float32(NUM_BUCKETS - max_exact)).astype(np.int32)
    large = np.minimum(large, NUM_BUCKETS - 1)
    bucket = np.where(n < max_exact, n, large)
    assert np.all(np.diff(bucket) >= 0) and bucket[-1] == NUM_BUCKETS - 1
    return [int(np.argmax(bucket >= b)) for b in range(1, NUM_BUCKETS)]


BUCKET_THRESHOLDS = _bucket_thresholds()
FAR_DISTANCE = BUCKET_THRESHOLDS[-1]
BIAS_W = 2 * LANES
assert FAR_DISTANCE <= LANES


def _bias_kernel(rb_ref, o_ref):
    r = lax.broadcasted_iota(I32, (LANES, BIAS_W), 0)
    c = lax.broadcasted_iota(I32, (LANES, BIAS_W), 1)
    d = LANES + r - c
    bucket = jnp.zeros((LANES, BIAS_W), I32)
    for thr in BUCKET_THRESHOLDS:
        bucket = bucket + (d >= thr).astype(I32)
    for h in range(N_HEADS):
        far = rb_ref[NUM_BUCKETS - 1, h]
        val = jnp.zeros((LANES, BIAS_W), F32)
        for b in range(NUM_BUCKETS - 1):
            val = jnp.where(bucket == b, rb_ref[b, h] - far, val)
        o_ref[h] = jnp.where(d >= 0, val, 0.0)


def _bias_tiles(rel_bias):
    return pl.pallas_call(
        _bias_kernel,
        out_shape=jax.ShapeDtypeStruct((N_HEADS, LANES, BIAS_W), F32),
        in_specs=[pl.BlockSpec(memory_space=pltpu.SMEM)],
        name="rel_bias_tiles",
    )(rel_bias)


def _decode_key(key):
    bits = jnp.where(key < 0, key ^ 0x7FFFFFFF, key)
    return lax.bitcast_convert_type(bits, F32)


def _count(mask):
    return jnp.sum(jnp.where(mask, 1.0, 0.0), axis=1, keepdims=True)


def _topk_mask(score, admissible, kpos, need):
    rows, n = score.shape
    score = jnp.where(admissible, score, -jnp.inf)
    need_f = float(need)

    nonneg = _count(score >= 0.0) >= need_f
    lo0 = jnp.where(nonneg, 0, INT_MIN).astype(I32)

    def bit_step(it, lo):
        cand = lo | (jnp.int32(1) << (30 - it))
        ok = _count(score >= _decode_key(cand)) >= need_f
        return jnp.where(ok, cand, lo)

    lo = lax.fori_loop(0, 31, bit_step, lo0)
    thr = jnp.where(lo <= KEY_NEG_INF, -jnp.inf, _decode_key(lo))

    gt = score > thr
    eq = score == thr
    spare = need_f - _count(gt)

    nbits = int(n).bit_length()

    def idx_step(it, lim):
        cand = lim + (jnp.int32(1) << (nbits - 1 - it))
        ok = _count(eq & (kpos < cand)) <= spare
        return jnp.where(ok, cand, lim)

    lim = lax.fori_loop(0, nbits, idx_step, jnp.zeros((rows, 1), I32))
    return (gt | (eq & (kpos < lim))) & admissible


def _attn_prompt_kernel(rb_ref, q_ref, k_ref, v_ref, qi_ref, ki_ref, wi_ref, bias_ref, o_ref,
                        k_bf, v_bf, ki_hi, ki_lo, s_ref, *, tq, seq, topk):
    i = pl.program_id(1)

    @pl.when(i == 0)
    def _():
        k_bf[...] = k_ref[...].astype(BF16)
        v_bf[...] = v_ref[...].astype(BF16)
        hi, lo = _split_bf16(ki_ref[...])
        ki_hi[...] = hi
        ki_lo[...] = lo

    wi = wi_ref[...] * (IDX_DIM ** -0.5 * IDX_HEADS ** -0.5)
    khi = ki_hi[...]
    klo = ki_lo[...]
    score = jnp.zeros((tq, seq), F32)
    for h in range(IDX_HEADS):
        qhi, qlo = _split_bf16(qi_ref[:, h * IDX_DIM:(h + 1) * IDX_DIM])
        s = _dot_nt(qhi, khi) + (_dot_nt(qhi, klo) + _dot_nt(qlo, khi))
        score = score + jnp.maximum(s, 0.0) * wi[:, h:h + 1]

    kpos = lax.broadcasted_iota(I32, (tq, seq), 1)
    qpos = i * tq + lax.broadcasted_iota(I32, (tq, seq), 0)
    keep = _topk_mask(score, kpos <= qpos, kpos, topk)

    for n in range(N_KV_HEADS):
        kn = k_bf[:, n * HEAD_DIM:(n + 1) * HEAD_DIM]
        vn = v_bf[:, n * HEAD_DIM:(n + 1) * HEAD_DIM]
        for g in range(GROUP):
            h = n * GROUP + g
            qh = q_ref[:, h * HEAD_DIM:(h + 1) * HEAD_DIM].astype(BF16)
            s_ref[...] = _dot_nt(qh, kn) * HEAD_DIM ** -0.5 + rb_ref[NUM_BUCKETS - 1, h]

            @pl.when(i == 0)
            def _():
                s_ref[:, :tq] += bias_ref[h, :, LANES:]

            @pl.when(i > 0)
            def _():
                ws = pl.multiple_of((i - 1) * tq, LANES)
                s_ref[:, pl.ds(ws, BIAS_W)] += bias_ref[h]

            s = jnp.where(keep, s_ref[...], NEG)
            p = jnp.exp(s - jnp.max(s, axis=1, keepdims=True))
            l = jnp.sum(p, axis=1, keepdims=True)
            o = _dot(p.astype(BF16), vn) / l
            o_ref[:, h * HEAD_DIM:(h + 1) * HEAD_DIM] = o.astype(o_ref.dtype)


def _attn_prompt(u_all, bias, rel_bias, batch, seq):
    tq = LANES
    nq = seq // tq
    topk = min(TOPK_MAX, seq // 4)
    kern = functools.partial(_attn_prompt_kernel, tq=tq, seq=seq, topk=topk)
    row = lambda b, i: b * nq + i
    return pl.pallas_call(
        kern,
        out_shape=jax.ShapeDtypeStruct((batch * seq, D_ATTN), BF16),
        grid=(batch, nq),
        in_specs=[
            pl.BlockSpec(memory_space=pltpu.SMEM),
            pl.BlockSpec((tq, D_ATTN), lambda b, i: (row(b, i), C_Q // D_ATTN)),
            pl.BlockSpec((seq, D_KV), lambda b, i: (b, C_K // D_KV)),
            pl.BlockSpec((seq, D_KV), lambda b, i: (b, C_V // D_KV)),
            pl.BlockSpec((tq, IDX_HEADS * IDX_DIM), lambda b, i: (row(b, i), C_QI // (IDX_HEADS * IDX_DIM))),
            pl.BlockSpec((seq, IDX_DIM), lambda b, i: (b, C_KI // IDX_DIM)),
            pl.BlockSpec((tq, LANES), lambda b, i: (row(b, i), C_WI // LANES)),
            pl.BlockSpec((N_HEADS, LANES, BIAS_W), lambda b, i: (0, 0, 0)),
        ],
        out_specs=pl.BlockSpec((tq, D_ATTN), lambda b, i: (row(b, i), 0)),
        scratch_shapes=[
            pltpu.VMEM((seq, D_KV), BF16),
            pltpu.VMEM((seq, D_KV), BF16),
            pltpu.VMEM((seq, IDX_DIM), BF16),
            pltpu.VMEM((seq, IDX_DIM), BF16),
            pltpu.VMEM((tq, seq), F32),
        ],
        compiler_params=_compiler_params(("parallel", "arbitrary")),
        name="attn_prompt",
    )(rel_bias, u_all, u_all, u_all, u_all, u_all, u_all, bias)


def _gather_pages(pt_ref, b, n_pages, page, srcs_dsts_sems):
    def copy(src, dst, sem, p, pg):
        return pltpu.make_async_copy(src.at[pg], dst.at[pl.ds(pl.multiple_of(p * page, page), page)], sem)

    def start(p, carry):
        pg = pt_ref[b, p]
        for src, dst, sem in srcs_dsts_sems:
            copy(src, dst, sem, p, pg).start()
        return carry

    lax.fori_loop(0, n_pages, start, 0)

    def wait_all():
        def wait(p, carry):
            for src, dst, sem in srcs_dsts_sems:
                copy(src, dst, sem, p, 0).wait()
            return carry
        lax.fori_loop(0, n_pages, wait, 0)

    return wait_all


def _sample_index_kernel(pt_ref, cache_ref, qi_ref, w_ref, kinew_ref, o_ref,
                         ki_all, sc_ref, sem, *, n_pages, page, past, chunk, topk, group_rows):
    b = pl.program_id(0)
    total = ki_all.shape[0]
    wait_all = _gather_pages(pt_ref, b, n_pages, page, [(cache_ref, ki_all, sem.at[0])])
    ki_all[past:past + SUBLANES, :] = kinew_ref[0]
    ki_all[past + SUBLANES:, :] = jnp.zeros((total - past - SUBLANES, IDX_DIM), F32)
    wait_all()

    qhi, qlo = _split_bf16(qi_ref[0])
    w = w_ref[0] * (IDX_DIM ** -0.5 * IDX_HEADS ** -0.5)
    rows = qi_ref.shape[1]
    for c0 in range(0, total, chunk):
        khi, klo = _split_bf16(ki_all[c0:c0 + chunk, :])
        s = _dot_nt(qhi, khi) + (_dot_nt(qhi, klo) + _dot_nt(qlo, khi))
        s = jnp.maximum(s, 0.0) * w
        sc_ref[:, c0:c0 + chunk] = jnp.sum(s.reshape(rows // IDX_HEADS, IDX_HEADS, chunk), axis=1)

    nrow = rows // IDX_HEADS
    kpos = lax.broadcasted_iota(I32, (nrow, total), 1)
    qpos = past + (lax.broadcasted_iota(I32, (nrow, total), 0) >> (group_rows.bit_length() - 1))
    keep = _topk_mask(sc_ref[...], kpos <= qpos, kpos, topk)
    o_ref[0] = jnp.where(keep, 1.0, 0.0)


def _sample_index(page_table, cache_idx, qi_rep, w_col, ki_new, past, topk):
    bd, n_pages = page_table.shape
    page = cache_idx.shape[1]
    total = past + LANES
    rows = qi_rep.shape[1]
    nrow = rows // IDX_HEADS
    chunk = total // 5
    assert chunk * 5 == total and chunk % LANES == 0
    kern = functools.partial(_sample_index_kernel, n_pages=n_pages, page=page, past=past, chunk=chunk,
                             topk=topk, group_rows=GROUP)
    return pl.pallas_call(
        kern,
        out_shape=jax.ShapeDtypeStruct((bd, nrow, total), F32),
        grid_spec=pltpu.PrefetchScalarGridSpec(
            num_scalar_prefetch=1,
            grid=(bd,),
            in_specs=[
                pl.BlockSpec(memory_space=pl.ANY),
                pl.BlockSpec((1, rows, IDX_DIM), lambda b, pt: (b, 0, 0)),
                pl.BlockSpec((1, rows, 1), lambda b, pt: (b, 0, 0)),
                pl.BlockSpec((1, SUBLANES, IDX_DIM), lambda b, pt: (b, 0, 0)),
            ],
            out_specs=pl.BlockSpec((1, nrow, total), lambda b, pt: (b, 0, 0)),
            scratch_shapes=[
                pltpu.VMEM((total, IDX_DIM), F32),
                pltpu.VMEM((nrow, total), F32),
                pltpu.SemaphoreType.DMA((1,)),
            ],
        ),
        compiler_params=_compiler_params(("arbitrary",)),
        name="sample_index",
    )(page_table, cache_idx, qi_rep, w_col, ki_new)


def _sample_attend_kernel(pt_ref, ck_ref, cv_ref, q_ref, knew_ref, vnew_ref, keep_ref, tail_ref, far_ref,
                          o_ref, k_all, v_all, sem, *, n_pages, page, past):
    b = pl.program_id(0)
    total = k_all.shape[0]
    wait_all = _gather_pages(pt_ref, b, n_pages, page,
                             [(ck_ref, k_all, sem.at[0]), (cv_ref, v_all, sem.at[1])])
    pad = jnp.zeros((total - past - SUBLANES, D_KV), F32)
    k_all[past:past + SUBLANES, :] = knew_ref[0]
    v_all[past:past + SUBLANES, :] = vnew_ref[0]
    k_all[past + SUBLANES:, :] = pad
    v_all[past + SUBLANES:, :] = pad
    wait_all()

    keep = keep_ref[0] > 0.5
    near = total - BIAS_W
    for n in range(N_KV_HEADS):
        kn = k_all[:, n * HEAD_DIM:(n + 1) * HEAD_DIM].astype(BF16)
        vn = v_all[:, n * HEAD_DIM:(n + 1) * HEAD_DIM].astype(BF16)
        s = _dot_nt(q_ref[0, n].astype(BF16), kn) * HEAD_DIM ** -0.5 + far_ref[n]
        s = jnp.concatenate([s[:, :near], s[:, near:] + tail_ref[n]], axis=1)
        s = jnp.where(keep, s, NEG)
        p = jnp.exp(s - jnp.max(s, axis=1, keepdims=True))
        l = jnp.sum(p, axis=1, keepdims=True)
        o_ref[0, n] = (_dot(p.astype(BF16), vn) / l).astype(o_ref.dtype)


def _sample_attend(page_table, cache_k, cache_v, q_rows, k_new, v_new, keep, tail, far, past):
    bd, n_pages = page_table.shape
    page = cache_k.shape[1]
    total = past + LANES
    nrow = q_rows.shape[2]
    kern = functools.partial(_sample_attend_kernel, n_pages=n_pages, page=page, past=past)
    return pl.pallas_call(
        kern,
        out_shape=jax.ShapeDtypeStruct((bd, N_KV_HEADS, nrow, HEAD_DIM), BF16),
        grid_spec=pltpu.PrefetchScalarGridSpec(
            num_scalar_prefetch=1,
            grid=(bd,),
            in_specs=[
                pl.BlockSpec(memory_space=pl.ANY),
                pl.BlockSpec(memory_space=pl.ANY),
                pl.BlockSpec((1, N_KV_HEADS, nrow, HEAD_DIM), lambda b, pt: (b, 0, 0, 0)),
                pl.BlockSpec((1, SUBLANES, D_KV), lambda b, pt: (b, 0, 0)),
                pl.BlockSpec((1, SUBLANES, D_KV), lambda b, pt: (b, 0, 0)),
                pl.BlockSpec((1, nrow, total), lambda b, pt: (b, 0, 0)),
                pl.BlockSpec((N_KV_HEADS, nrow, BIAS_W), lambda b, pt: (0, 0, 0)),
                pl.BlockSpec((N_KV_HEADS, nrow, 1), lambda b, pt: (0, 0, 0)),
            ],
            out_specs=pl.BlockSpec((1, N_KV_HEADS, nrow, HEAD_DIM), lambda b, pt: (b, 0, 0, 0)),
            scratch_shapes=[
                pltpu.VMEM((total, D_KV), F32),
                pltpu.VMEM((total, D_KV), F32),
                pltpu.SemaphoreType.DMA((2,)),
            ],
        ),
        compiler_params=_compiler_params(("arbitrary",)),
        name="sample_attend",
    )(page_table, cache_k, cache_v, q_rows, k_new, v_new, keep, tail, far)


def _out_proj_kernel(x_ref, pool_ref, attn_ref, wp_ref, wa_ref, o_ref):
    o_ref[...] = x_ref[...] + (_dot(pool_ref[...], wp_ref[...]) + _dot(attn_ref[...], wa_ref[...]))


def _out_proj(x, pool, attn, w_out, tm, tn):
    m, d = x.shape
    return pl.pallas_call(
        _out_proj_kernel,
        out_shape=jax.ShapeDtypeStruct((m, d), F32),
        grid=(m // tm, d // tn),
        in_specs=[
            pl.BlockSpec((tm, tn), lambda i, j: (i, j)),
            pl.BlockSpec((tm, D_POOL), lambda i, j: (i, 0)),
            pl.BlockSpec((tm, D_ATTN), lambda i, j: (i, 0)),
            pl.BlockSpec((D_POOL, tn), lambda i, j: (0, j)),
            pl.BlockSpec((D_ATTN, tn), lambda i, j: (D_POOL // D_ATTN, j)),
        ],
        out_specs=pl.BlockSpec((tm, tn), lambda i, j: (i, j)),
        compiler_params=_compiler_params(("parallel", "arbitrary")),
        name="out_proj",
    )(x, pool, attn, w_out, w_out)


def _proj_gain(g_q, g_k, g_idx_k):
    return jnp.concatenate([
        jnp.ones((D_POOL,), F32), jnp.tile(g_q, N_HEADS), jnp.tile(g_k, N_KV_HEADS),
        jnp.ones((C_KI - C_V,), F32), g_idx_k, jnp.ones((D_IN_PAD - C_WI,), F32)]).reshape(1, D_IN_PAD)


def _pad_rows(a, rows):
    return jnp.pad(a, ((0, 0), (0, rows - a.shape[1]), (0, 0)))


def kernel(x_prompt, x_sample, cache_k, cache_v, cache_idx_k, state_pool, page_table, g_ffn1, w_gate1, w_up1,
           w_down1, g_mix, w_in, g_q, g_k, g_idx_k, w_pool, pool_scale, w_out, rel_bias, g_ffn2, w_gate2,
           w_up2, w_down2):
    batch, seq, d = x_prompt.shape
    bd, t_new, _ = x_sample.shape
    depth = g_ffn1.shape[0]
    assert depth == 1 and d == D_MODEL
    past = page_table.shape[1] * cache_k.shape[2]
    mp, ms = batch * seq, bd * t_new
    tm_p, tm_s = 512, ms
    tf = 256

    xp = x_prompt.reshape(mp, d)
    xs = x_sample.reshape(ms, d)
    l = 0

    def ffn(x, tm, g, wg, wu, wd):
        return _ffn(x, g, wg, wu, wd, tm, tf)

    wg1, wu1, wd1 = (w[l].astype(BF16) for w in (w_gate1, w_up1, w_down1))
    xp = ffn(xp, tm_p, g_ffn1[l], wg1, wu1, wd1)
    xs = ffn(xs, tm_s, g_ffn1[l], wg1, wu1, wd1)

    w_in_p = jnp.pad(w_in[l].astype(BF16), ((0, 0), (0, D_IN_PAD - D_IN)))
    gain = _proj_gain(g_q[l], g_k[l], g_idx_k[l])
    up = _proj(xp, g_mix[l], w_in_p, gain, tm_p)
    us = _proj(xs, g_mix[l], w_in_p, gain, tm_s)

    wp = w_pool[l].astype(BF16)
    scale = pool_scale[l].reshape(1, D_POOL)
    bias = _bias_tiles(rel_bias)

    pool_p = _pool_prompt(up, wp, scale, batch, seq, 256)
    attn_p = _attn_prompt(up, bias, rel_bias, batch, seq)

    hist = state_pool[l]
    u_pool_s = us[:, :D_POOL].reshape(bd, t_new, D_POOL)
    ext = jnp.concatenate([hist, u_pool_s], axis=1)
    pool_s = _pool_sample(ext.transpose(1, 0, 2), wp, scale, t_new)
    pool_s = pool_s.transpose(1, 0, 2).reshape(ms, D_POOL)

    us3 = us.reshape(bd, t_new, D_IN_PAD)
    rows_i = t_new * GROUP * IDX_HEADS
    qi_s = us3[:, :, C_QI:C_KI].reshape(bd, t_new, 1, IDX_HEADS, IDX_DIM)
    qi_rep = jnp.broadcast_to(qi_s, (bd, t_new, GROUP, IDX_HEADS, IDX_DIM)).reshape(bd, rows_i, IDX_DIM)
    wi_s = us3[:, :, C_WI:C_WI + IDX_HEADS].reshape(bd, t_new, 1, IDX_HEADS)
    w_col = jnp.broadcast_to(wi_s, (bd, t_new, GROUP, IDX_HEADS)).reshape(bd, rows_i, 1)
    ki_new = _pad_rows(us3[:, :, C_KI:C_WI], SUBLANES)
    k_new = _pad_rows(us3[:, :, C_K:C_V], SUBLANES)
    v_new = _pad_rows(us3[:, :, C_V:C_QI], SUBLANES)
    n_phys, page = cache_k.shape[1], cache_k.shape[2]
    topk_s = min(TOPK_MAX, (past + t_new) // 4)
    keep_s = _sample_index(page_table, cache_idx_k[l], qi_rep, w_col, ki_new, past, topk_s)

    nrow = t_new * GROUP
    q_rows = us3[:, :, C_Q:C_K].reshape(bd, t_new, N_KV_HEADS, GROUP, HEAD_DIM)
    q_rows = q_rows.transpose(0, 2, 1, 3, 4).reshape(bd, N_KV_HEADS, nrow, HEAD_DIM)
    tail = bias[:, :t_new, :].reshape(N_KV_HEADS, GROUP, t_new, BIAS_W)
    tail = tail.transpose(0, 2, 1, 3).reshape(N_KV_HEADS, nrow, BIAS_W)
    far = jnp.broadcast_to(rel_bias[NUM_BUCKETS - 1].reshape(N_KV_HEADS, 1, GROUP),
                           (N_KV_HEADS, t_new, GROUP)).reshape(N_KV_HEADS, nrow, 1)
    attn_s = _sample_attend(page_table, cache_k[l].reshape(n_phys, page, D_KV),
                            cache_v[l].reshape(n_phys, page, D_KV), q_rows, k_new, v_new, keep_s, tail, far, past)
    attn_s = attn_s.reshape(bd, N_KV_HEADS, t_new, GROUP, HEAD_DIM).transpose(0, 2, 1, 3, 4).reshape(ms, D_ATTN)

    w_out_bf = w_out[l].astype(BF16)
    xp = _out_proj(xp, pool_p, attn_p, w_out_bf, tm_p, 512)
    xs = _out_proj(xs, pool_s, attn_s, w_out_bf, tm_s, 512)

    wg2, wu2, wd2 = (w[l].astype(BF16) for w in (w_gate2, w_up2, w_down2))
    xp = ffn(xp, tm_p, g_ffn2[l], wg2, wu2, wd2)
    xs = ffn(xs, tm_s, g_ffn2[l], wg2, wu2, wd2)

    up4 = up.reshape(batch, seq, D_IN_PAD)
    return (
        xp.reshape(batch, seq, d),
        xs.reshape(bd, t_new, d),
        up4[:, :, C_K:C_V].reshape(1, batch, seq, N_KV_HEADS, HEAD_DIM),
        up4[:, :, C_V:C_QI].reshape(1, batch, seq, N_KV_HEADS, HEAD_DIM),
        up4[:, :, C_KI:C_WI].reshape(1, batch, seq, IDX_DIM),
        up4[:, seq - POOL_HIST:, :D_POOL].reshape(1, batch, POOL_HIST, D_POOL),
        us3[:, :, C_K:C_V].reshape(1, bd, t_new, N_KV_HEADS, HEAD_DIM),
        us3[:, :, C_V:C_QI].reshape(1, bd, t_new, N_KV_HEADS, HEAD_DIM),
        us3[:, :, C_KI:C_WI].reshape(1, bd, t_new, IDX_DIM),
        ext[:, t_new:].reshape(1, bd, POOL_HIST, D_POOL),
    )
```

```python
import functools
import math

import jax
import jax.numpy as jnp
import numpy as np
from jax import lax
from jax.experimental import pallas as pl
from jax.experimental.pallas import tpu as pltpu

F32 = jnp.float32
BF16 = jnp.bfloat16
I32 = jnp.int32

SUBLANES = 8
LANES = 128
VMEM_LIMIT_BYTES = 56 * 1024 * 1024

D_MODEL = 4096
D_POOL = D_MODEL // 2
POOL_WINDOWS = (2, 4, 8, 16)
N_POOL_GROUPS = len(POOL_WINDOWS)
POOL_GROUP = D_POOL // N_POOL_GROUPS
POOL_HIST = max(POOL_WINDOWS) - 1
HEAD_DIM = 128
N_HEADS = (D_MODEL - D_POOL) // HEAD_DIM
N_KV_HEADS = 4
GROUP = N_HEADS // N_KV_HEADS
D_ATTN = N_HEADS * HEAD_DIM
D_KV = N_KV_HEADS * HEAD_DIM
IDX_HEADS = 8
IDX_DIM = 128
TOPK_MAX = 256
NUM_BUCKETS = 32
MAX_DISTANCE = 128
EPS = 1e-6
NEG = -1e30

C_Q = D_POOL
C_K = C_Q + D_ATTN
C_V = C_K + D_KV
C_QI = C_V + D_KV
C_KI = C_QI + IDX_HEADS * IDX_DIM
C_WI = C_KI + IDX_DIM
D_IN = C_WI + IDX_HEADS
D_IN_PAD = ((D_IN + 2 * LANES - 1) // (2 * LANES)) * (2 * LANES)
PROJ_TN = 10 * LANES
assert D_IN_PAD % PROJ_TN == 0

INT_MIN = -(2 ** 31)
KEY_NEG_INF = INT_MIN + 0x007FFFFF


def _compiler_params(semantics):
    return pltpu.CompilerParams(dimension_semantics=semantics, vmem_limit_bytes=VMEM_LIMIT_BYTES)


def _dot(a, b):
    return jnp.dot(a, b, preferred_element_type=F32)


def _dot_nt(a, b):
    return lax.dot_general(a, b, (((1,), (1,)), ((), ())), preferred_element_type=F32)


def _rms_scale(x):
    return lax.rsqrt(jnp.mean(x * x, axis=-1, keepdims=True) + EPS)


def _split_bf16(x):
    hi = x.astype(BF16)
    lo = (x - hi.astype(F32)).astype(BF16)
    return hi, lo


def _ffn_kernel(x_ref, g_ref, wg_ref, wu_ref, wd_ref, o_ref, h_ref):
    j = pl.program_id(1)

    @pl.when(j == 0)
    def _():
        x = x_ref[...]
        h_ref[...] = (x * _rms_scale(x) * g_ref[...]).astype(BF16)
        o_ref[...] = jnp.zeros_like(o_ref)

    h = h_ref[...]
    a = _dot(h, wg_ref[...])
    b = _dot(h, wu_ref[...])
    s = (a * jax.nn.sigmoid(a) * b).astype(BF16)
    o_ref[...] += _dot(s, wd_ref[...])

    @pl.when(j == pl.num_programs(1) - 1)
    def _():
        o_ref[...] = x_ref[...] + 0.5 * o_ref[...]


def _ffn(x, g, wg, wu, wd, tm, tf):
    m, d = x.shape
    f = wg.shape[1]
    return pl.pallas_call(
        _ffn_kernel,
        out_shape=jax.ShapeDtypeStruct((m, d), F32),
        grid=(m // tm, f // tf),
        in_specs=[
            pl.BlockSpec((tm, d), lambda i, j: (i, 0), pipeline_mode=pl.Buffered(1)),
            pl.BlockSpec((1, d), lambda i, j: (0, 0)),
            pl.BlockSpec((d, tf), lambda i, j: (0, j)),
            pl.BlockSpec((d, tf), lambda i, j: (0, j)),
            pl.BlockSpec((tf, d), lambda i, j: (j, 0)),
        ],
        out_specs=pl.BlockSpec((tm, d), lambda i, j: (i, 0)),
        scratch_shapes=[pltpu.VMEM((tm, d), BF16)],
        compiler_params=_compiler_params(("parallel", "arbitrary")),
        name="swiglu_half",
    )(x, g.reshape(1, d), wg, wu, wd)


def _proj_kernel(x_ref, g_ref, w_ref, gain_ref, o_ref, h_ref):
    j = pl.program_id(1)

    @pl.when(j == 0)
    def _():
        x = x_ref[...]
        h_ref[...] = (x * _rms_scale(x) * g_ref[...]).astype(BF16)

    u = _dot(h_ref[...], w_ref[...])
    gain = gain_ref[...]

    def is_normed(col):
        return C_Q <= col < C_V or C_KI <= col < C_WI

    for jt in range(D_IN_PAD // PROJ_TN):
        @pl.when(j == jt)
        def _(jt=jt):
            c = 0
            while c < PROJ_TN:
                if is_normed(jt * PROJ_TN + c):
                    part = u[:, c:c + LANES]
                    o_ref[:, c:c + LANES] = part * _rms_scale(part) * gain[:, c:c + LANES]
                    c += LANES
                else:
                    end = c
                    while end < PROJ_TN and not is_normed(jt * PROJ_TN + end):
                        end += LANES
                    o_ref[:, c:end] = u[:, c:end]
                    c = end


def _proj(x, g_mix, w_in, gain, tm):
    m, d = x.shape
    n = w_in.shape[1]
    return pl.pallas_call(
        _proj_kernel,
        out_shape=jax.ShapeDtypeStruct((m, n), F32),
        grid=(m // tm, n // PROJ_TN),
        in_specs=[
            pl.BlockSpec((tm, d), lambda i, j: (i, 0), pipeline_mode=pl.Buffered(1)),
            pl.BlockSpec((1, d), lambda i, j: (0, 0)),
            pl.BlockSpec((d, PROJ_TN), lambda i, j: (0, j)),
            pl.BlockSpec((1, PROJ_TN), lambda i, j: (0, j)),
        ],
        out_specs=pl.BlockSpec((tm, PROJ_TN), lambda i, j: (i, j)),
        scratch_shapes=[pltpu.VMEM((tm, d), BF16)],
        compiler_params=_compiler_params(("parallel", "arbitrary")),
        name="in_proj",
    )(x, g_mix.reshape(1, d), w_in, gain)


HALO = 2 * SUBLANES


def _pool_groups(ext_rows, cur, count_of, wp_ref, scale_ref, store):
    for g, w in enumerate(POOL_WINDOWS):
        lo, hi = g * POOL_GROUP, (g + 1) * POOL_GROUP
        acc = ext_rows(0, lo, hi)
        for d in range(1, w):
            acc = acc + ext_rows(d, lo, hi)
        diff = acc / count_of(w) - cur(lo, hi)
        out = _dot(diff.astype(BF16), wp_ref[g]) * scale_ref[:, lo:hi]
        store(lo, hi, out)


def _pool_prompt_kernel(u_ref, halo_ref, wp_ref, scale_ref, o_ref, ext_ref, *, tp):
    i = pl.program_id(1)
    halo = halo_ref[...]
    ext_ref[:HALO, :] = jnp.where(i == 0, jnp.zeros_like(halo), halo)
    ext_ref[HALO:, :] = u_ref[...]
    pos = i * tp + lax.broadcasted_iota(I32, (tp, POOL_GROUP), 0)

    def store(lo, hi, out):
        o_ref[:, lo:hi] = out.astype(o_ref.dtype)

    _pool_groups(
        lambda d, lo, hi: ext_ref[HALO - d:HALO - d + tp, lo:hi],
        lambda lo, hi: u_ref[:, lo:hi],
        lambda w: jnp.minimum(pos + 1, w).astype(F32),
        wp_ref, scale_ref, store)


def _pool_prompt(u_all, wp, scale, batch, seq, tp):
    nt = seq // tp
    kern = functools.partial(_pool_prompt_kernel, tp=tp)
    return pl.pallas_call(
        kern,
        out_shape=jax.ShapeDtypeStruct((batch * seq, D_POOL), BF16),
        grid=(batch, nt),
        in_specs=[
            pl.BlockSpec((tp, D_POOL), lambda b, i: (b * nt + i, 0)),
            pl.BlockSpec((HALO, D_POOL),
                         lambda b, i: (jnp.maximum((b * seq + i * tp) // HALO - 1, 0), 0)),
            pl.BlockSpec((N_POOL_GROUPS, POOL_GROUP, POOL_GROUP), lambda b, i: (0, 0, 0)),
            pl.BlockSpec((1, D_POOL), lambda b, i: (0, 0)),
        ],
        out_specs=pl.BlockSpec((tp, D_POOL), lambda b, i: (b * nt + i, 0)),
        scratch_shapes=[pltpu.VMEM((HALO + tp, D_POOL), F32)],
        compiler_params=_compiler_params(("parallel", "arbitrary")),
        name="pool_prompt",
    )(u_all, u_all, wp, scale)


def _pool_sample_kernel(ext_ref, wp_ref, scale_ref, o_ref, *, t_new):
    for t in range(t_new):
        def store(lo, hi, out, t=t):
            o_ref[t, :, lo:hi] = out.astype(o_ref.dtype)

        _pool_groups(
            lambda d, lo, hi, t=t: ext_ref[POOL_HIST + t - d, :, lo:hi],
            lambda lo, hi, t=t: ext_ref[POOL_HIST + t, :, lo:hi],
            lambda w: float(w),
            wp_ref, scale_ref, store)


def _pool_sample(ext_t, wp, scale, t_new):
    _, bd, _ = ext_t.shape
    kern = functools.partial(_pool_sample_kernel, t_new=t_new)
    return pl.pallas_call(
        kern,
        out_shape=jax.ShapeDtypeStruct((t_new, bd, D_POOL), BF16),
        compiler_params=pltpu.CompilerParams(vmem_limit_bytes=VMEM_LIMIT_BYTES),
        name="pool_sample",
    )(ext_t, wp, scale)


def _bucket_thresholds():
    n = np.arange(0, 4 * MAX_DISTANCE, dtype=np.int64)
    max_exact = NUM_BUCKETS // 2
    nf = np.maximum(n, 1).astype(np.float32)
    large = max_exact + (np.log(nf / np.float32(max_exact)) / np.float32(math.log(MAX_DISTANCE / max_exact))
                         * np.float32(NUM_BUCKETS - max_exact)).astype(np.int32)
    large = np.minimum(large, NUM_BUCKETS - 1)
    bucket = np.where(n < max_exact, n, large)
    assert np.all(np.diff(bucket) >= 0) and bucket[-1] == NUM_BUCKETS - 1
    return [int(np.argmax(bucket >= b)) for b in range(1, NUM_BUCKETS)]


BUCKET_THRESHOLDS = _bucket_thresholds()
FAR_DISTANCE = BUCKET_THRESHOLDS[-1]
BIAS_W = 2 * LANES
assert FAR_DISTANCE <= LANES


def _bias_kernel(rb_ref, o_ref):
    r = lax.broadcasted_iota(I32, (LANES, BIAS_W), 0)
    c = lax.broadcasted_iota(I32, (LANES, BIAS_W), 1)
    d = LANES + r - c
    bucket = jnp.zeros((LANES, BIAS_W), I32)
    for thr in BUCKET_THRESHOLDS:
        bucket = bucket + (d >= thr).astype(I32)
    for h in range(N_HEADS):
        far = rb_ref[NUM_BUCKETS - 1, h]
        val = jnp.zeros((LANES, BIAS_W), F32)
        for b in range(NUM_BUCKETS - 1):
            val = jnp.where(bucket == b, rb_ref[b, h] - far, val)
        o_ref[h] = jnp.where(d >= 0, val, 0.0)


def _bias_tiles(rel_bias):
    return pl.pallas_call(
        _bias_kernel,
        out_shape=jax.ShapeDtypeStruct((N_HEADS, LANES, BIAS_W), F32),
        in_specs=[pl.BlockSpec(memory_space=pltpu.SMEM)],
        name="rel_bias_tiles",
    )(rel_bias)


def _decode_key(key):
    bits = jnp.where(key < 0, key ^ 0x7FFFFFFF, key)
    return lax.bitcast_convert_type(bits, F32)


def _count(mask):
    return jnp.sum(jnp.where(mask, 1.0, 0.0), axis=1, keepdims=True)


def _topk_mask(score, admissible, kpos, need):
    rows, n = score.shape
    score = jnp.where(admissible, score, -jnp.inf)
    need_f = float(need)

    nonneg = _count(score >= 0.0) >= need_f
    lo0 = jnp.where(nonneg, 0, INT_MIN).astype(I32)

    def bit_step(it, lo):
        cand = lo | (jnp.int32(1) << (30 - it))
        ok = _count(score >= _decode_key(cand)) >= need_f
        return jnp.where(ok, cand, lo)

    lo = lax.fori_loop(0, 31, bit_step, lo0)
    thr = jnp.where(lo <= KEY_NEG_INF, -jnp.inf, _decode_key(lo))

    gt = score > thr
    eq = score == thr
    n_gt = _count(gt)
    spare = need_f - n_gt
    nbits = int(n).bit_length()

    def trim_ties():
        def idx_step(it, lim):
            cand = lim + (jnp.int32(1) << (nbits - 1 - it))
            ok = _count(eq & (kpos < cand)) <= spare
            return jnp.where(ok, cand, lim)
        return lax.fori_loop(0, nbits, idx_step, jnp.zeros((rows, 1), I32))

    overshoot = jnp.max(n_gt + _count(eq)) > need_f
    lim = lax.cond(overshoot, trim_ties, lambda: jnp.full((rows, 1), 2 ** nbits, I32))
    return (gt | (eq & (kpos < lim))) & admissible


def _attn_prompt_body(i, ext, first, q_ref, qi_ref, wi_ref, bias_ref, o_ref, k_bf, v_bf, ki_hi, ki_lo, s_ref,
                      tq, topk):
    wi = wi_ref[...] * (IDX_DIM ** -0.5 * IDX_HEADS ** -0.5)
    khi = ki_hi[:ext, :]
    klo = ki_lo[:ext, :]
    score = jnp.zeros((tq, ext), F32)
    for h in range(IDX_HEADS):
        qhi, qlo = _split_bf16(qi_ref[:, h * IDX_DIM:(h + 1) * IDX_DIM])
        s = _dot_nt(qhi, khi) + (_dot_nt(qhi, klo) + _dot_nt(qlo, khi))
        score = score + jnp.maximum(s, 0.0) * wi[:, h:h + 1]

    kpos = lax.broadcasted_iota(I32, (tq, ext), 1)
    qpos = i * tq + lax.broadcasted_iota(I32, (tq, ext), 0)
    keep = _topk_mask(score, kpos <= qpos, kpos, topk)

    for n in range(N_KV_HEADS):
        kn = k_bf[:ext, n * HEAD_DIM:(n + 1) * HEAD_DIM]
        vn = v_bf[:ext, n * HEAD_DIM:(n + 1) * HEAD_DIM]
        for g in range(GROUP):
            h = n * GROUP + g
            qh = (q_ref[:, h * HEAD_DIM:(h + 1) * HEAD_DIM] * HEAD_DIM ** -0.5).astype(BF16)
            s_ref[:, :ext] = _dot_nt(qh, kn)
            if first:
                @pl.when(i == 0)
                def _():
                    s_ref[:, :tq] += bias_ref[h, :, LANES:]

            @pl.when(i > 0)
            def _():
                ws = pl.multiple_of((i - 1) * tq, LANES)
                s_ref[:, pl.ds(ws, BIAS_W)] += bias_ref[h]

            s = jnp.where(keep, s_ref[:, :ext], NEG)
            p = jnp.exp(s - jnp.max(s, axis=1, keepdims=True))
            l = jnp.sum(p, axis=1, keepdims=True)
            o = _dot(p.astype(BF16), vn) / l
            o_ref[:, h * HEAD_DIM:(h + 1) * HEAD_DIM] = o.astype(o_ref.dtype)


def _attn_prompt_kernel(q_ref, k_ref, v_ref, qi_ref, ki_ref, wi_ref, bias_ref, o_ref,
                        k_bf, v_bf, ki_hi, ki_lo, s_ref, *, tq, topk, extents):
    i = pl.program_id(1)

    @pl.when(i == 0)
    def _():
        k_bf[...] = k_ref[...].astype(BF16)
        v_bf[...] = v_ref[...].astype(BF16)
        hi, lo = _split_bf16(ki_ref[...])
        ki_hi[...] = hi
        ki_lo[...] = lo

    lo_tile = 0
    for ext in extents:
        hi_tile = ext // tq

        @pl.when((i >= lo_tile) & (i < hi_tile))
        def _(ext=ext, first=lo_tile == 0):
            _attn_prompt_body(i, ext, first, q_ref, qi_ref, wi_ref, bias_ref, o_ref, k_bf, v_bf, ki_hi, ki_lo,
                              s_ref, tq, topk)
        lo_tile = hi_tile


def _attn_prompt(u_all, bias, batch, seq):
    tq = LANES
    nq = seq // tq
    topk = min(TOPK_MAX, seq // 4)
    n_ext = 4
    extents = tuple(seq * (e + 1) // n_ext for e in range(n_ext))
    assert all(ext % (2 * tq) == 0 for ext in extents)
    kern = functools.partial(_attn_prompt_kernel, tq=tq, topk=topk, extents=extents)
    row = lambda b, i: b * nq + i
    return pl.pallas_call(
        kern,
        out_shape=jax.ShapeDtypeStruct((batch * seq, D_ATTN), BF16),
        grid=(batch, nq),
        in_specs=[
            pl.BlockSpec((tq, D_ATTN), lambda b, i: (row(b, i), C_Q // D_ATTN)),
            pl.BlockSpec((seq, D_KV), lambda b, i: (b, C_K // D_KV)),
            pl.BlockSpec((seq, D_KV), lambda b, i: (b, C_V // D_KV)),
            pl.BlockSpec((tq, IDX_HEADS * IDX_DIM), lambda b, i: (row(b, i), C_QI // (IDX_HEADS * IDX_DIM))),
            pl.BlockSpec((seq, IDX_DIM), lambda b, i: (b, C_KI // IDX_DIM)),
            pl.BlockSpec((tq, LANES), lambda b, i: (row(b, i), C_WI // LANES)),
            pl.BlockSpec((N_HEADS, LANES, BIAS_W), lambda b, i: (0, 0, 0)),
        ],
        out_specs=pl.BlockSpec((tq, D_ATTN), lambda b, i: (row(b, i), 0)),
        scratch_shapes=[
            pltpu.VMEM((seq, D_KV), BF16),
            pltpu.VMEM((seq, D_KV), BF16),
            pltpu.VMEM((seq, IDX_DIM), BF16),
            pltpu.VMEM((seq, IDX_DIM), BF16),
            pltpu.VMEM((tq, seq), F32),
        ],
        compiler_params=_compiler_params(("parallel", "arbitrary")),
        name="attn_prompt",
    )(u_all, u_all, u_all, u_all, u_all, u_all, bias)


def _gather_pages(pt_ref, b, n_pages, page, srcs_dsts_sems):
    def copy(src, dst, sem, p, pg):
        return pltpu.make_async_copy(src.at[pg], dst.at[pl.ds(pl.multiple_of(p * page, page), page)], sem)

    def start(p, carry):
        pg = pt_ref[b, p]
        for src, dst, sem in srcs_dsts_sems:
            copy(src, dst, sem, p, pg).start()
        return carry

    lax.fori_loop(0, n_pages, start, 0)

    def wait_all():
        def wait(p, carry):
            for src, dst, sem in srcs_dsts_sems:
                copy(src, dst, sem, p, 0).wait()
            return carry
        lax.fori_loop(0, n_pages, wait, 0)

    return wait_all


def _sample_index_kernel(pt_ref, cache_ref, qi_ref, w_ref, kinew_ref, o_ref,
                         ki_all, sc_ref, sem, *, n_pages, page, past, chunk, topk, group_rows):
    b = pl.program_id(0)
    total = ki_all.shape[0]
    wait_all = _gather_pages(pt_ref, b, n_pages, page, [(cache_ref, ki_all, sem.at[0])])
    ki_all[past:past + SUBLANES, :] = kinew_ref[0]
    ki_all[past + SUBLANES:, :] = jnp.zeros((total - past - SUBLANES, IDX_DIM), F32)
    wait_all()

    qhi, qlo = _split_bf16(qi_ref[0])
    w = w_ref[0] * (IDX_DIM ** -0.5 * IDX_HEADS ** -0.5)
    rows = qi_ref.shape[1]
    for c0 in range(0, total, chunk):
        khi, klo = _split_bf16(ki_all[c0:c0 + chunk, :])
        s = _dot_nt(qhi, khi) + (_dot_nt(qhi, klo) + _dot_nt(qlo, khi))
        s = jnp.maximum(s, 0.0) * w
        sc_ref[:, c0:c0 + chunk] = jnp.sum(s.reshape(rows // IDX_HEADS, IDX_HEADS, chunk), axis=1)

    nrow = rows // IDX_HEADS
    kpos = lax.broadcasted_iota(I32, (nrow, total), 1)
    qpos = past + (lax.broadcasted_iota(I32, (nrow, total), 0) >> (group_rows.bit_length() - 1))
    keep = _topk_mask(sc_ref[...], kpos <= qpos, kpos, topk)
    o_ref[0] = jnp.where(keep, 1.0, 0.0)


def _sample_index(page_table, cache_idx, qi_rep, w_col, ki_new, past, topk):
    bd, n_pages = page_table.shape
    page = cache_idx.shape[1]
    total = past + LANES
    rows = qi_rep.shape[1]
    nrow = rows // IDX_HEADS
    chunk = total // 5
    assert chunk * 5 == total and chunk % LANES == 0
    kern = functools.partial(_sample_index_kernel, n_pages=n_pages, page=page, past=past, chunk=chunk,
                             topk=topk, group_rows=GROUP)
    return pl.pallas_call(
        kern,
        out_shape=jax.ShapeDtypeStruct((bd, nrow, total), F32),
        grid_spec=pltpu.PrefetchScalarGridSpec(
            num_scalar_prefetch=1,
            grid=(bd,),
            in_specs=[
                pl.BlockSpec(memory_space=pl.ANY),
                pl.BlockSpec((1, rows, IDX_DIM), lambda b, pt: (b, 0, 0)),
                pl.BlockSpec((1, rows, 1), lambda b, pt: (b, 0, 0)),
                pl.BlockSpec((1, SUBLANES, IDX_DIM), lambda b, pt: (b, 0, 0)),
            ],
            out_specs=pl.BlockSpec((1, nrow, total), lambda b, pt: (b, 0, 0)),
            scratch_shapes=[
                pltpu.VMEM((total, IDX_DIM), F32),
                pltpu.VMEM((nrow, total), F32),
                pltpu.SemaphoreType.DMA((1,)),
            ],
        ),
        compiler_params=_compiler_params(("arbitrary",)),
        name="sample_index",
    )(page_table, cache_idx, qi_rep, w_col, ki_new)


def _sample_attend_kernel(pt_ref, ck_ref, cv_ref, q_ref, knew_ref, vnew_ref, keep_ref, tail_ref,
                          o_ref, k_all, v_all, sem, *, layer, n_pages, page, past):
    b = pl.program_id(0)
    total = k_all.shape[0] // N_KV_HEADS
    new0, new1 = past * N_KV_HEADS, (past + SUBLANES) * N_KV_HEADS
    wait_all = _gather_pages(pt_ref, b, n_pages, page * N_KV_HEADS,
                             [(ck_ref.at[layer], k_all, sem.at[0]), (cv_ref.at[layer], v_all, sem.at[1])])
    pad = jnp.zeros((total * N_KV_HEADS - new1, HEAD_DIM), F32)
    k_all[new0:new1, :] = knew_ref[0]
    v_all[new0:new1, :] = vnew_ref[0]
    k_all[new1:, :] = pad
    v_all[new1:, :] = pad
    wait_all()

    keep = keep_ref[0] > 0.5
    near = total - BIAS_W
    for n in range(N_KV_HEADS):
        kn = k_all[pl.ds(n, total, stride=N_KV_HEADS), :].astype(BF16)
        vn = v_all[pl.ds(n, total, stride=N_KV_HEADS), :].astype(BF16)
        s = _dot_nt((q_ref[0, n] * HEAD_DIM ** -0.5).astype(BF16), kn)
        s = jnp.concatenate([s[:, :near], s[:, near:] + tail_ref[n]], axis=1)
        s = jnp.where(keep, s, NEG)
        p = jnp.exp(s - jnp.max(s, axis=1, keepdims=True))
        l = jnp.sum(p, axis=1, keepdims=True)
        o_ref[0, n] = (_dot(p.astype(BF16), vn) / l).astype(o_ref.dtype)


def _sample_attend(page_table, cache_k, cache_v, layer, page, q_rows, k_new, v_new, keep, tail, past):
    bd, n_pages = page_table.shape
    total = past + LANES
    nrow = q_rows.shape[2]
    kern = functools.partial(_sample_attend_kernel, layer=layer, n_pages=n_pages, page=page, past=past)
    return pl.pallas_call(
        kern,
        out_shape=jax.ShapeDtypeStruct((bd, N_KV_HEADS, nrow, HEAD_DIM), BF16),
        grid_spec=pltpu.PrefetchScalarGridSpec(
            num_scalar_prefetch=1,
            grid=(bd,),
            in_specs=[
                pl.BlockSpec(memory_space=pl.ANY),
                pl.BlockSpec(memory_space=pl.ANY),
                pl.BlockSpec((1, N_KV_HEADS, nrow, HEAD_DIM), lambda b, pt: (b, 0, 0, 0)),
                pl.BlockSpec((1, SUBLANES * N_KV_HEADS, HEAD_DIM), lambda b, pt: (b, 0, 0)),
                pl.BlockSpec((1, SUBLANES * N_KV_HEADS, HEAD_DIM), lambda b, pt: (b, 0, 0)),
                pl.BlockSpec((1, nrow, total), lambda b, pt: (b, 0, 0)),
                pl.BlockSpec((N_KV_HEADS, nrow, BIAS_W), lambda b, pt: (0, 0, 0)),
            ],
            out_specs=pl.BlockSpec((1, N_KV_HEADS, nrow, HEAD_DIM), lambda b, pt: (b, 0, 0, 0)),
            scratch_shapes=[
                pltpu.VMEM((total * N_KV_HEADS, HEAD_DIM), F32),
                pltpu.VMEM((total * N_KV_HEADS, HEAD_DIM), F32),
                pltpu.SemaphoreType.DMA((2,)),
            ],
        ),
        compiler_params=_compiler_params(("arbitrary",)),
        name="sample_attend",
    )(page_table, cache_k, cache_v, q_rows, k_new, v_new, keep, tail)


def _out_proj_kernel(x_ref, pool_ref, attn_ref, wp_ref, wa_ref, o_ref):
    o_ref[...] = x_ref[...] + (_dot(pool_ref[...], wp_ref[...]) + _dot(attn_ref[...], wa_ref[...]))


def _out_proj(x, pool, attn, w_out, tm, tn):
    m, d = x.shape
    return pl.pallas_call(
        _out_proj_kernel,
        out_shape=jax.ShapeDtypeStruct((m, d), F32),
        grid=(m // tm, d // tn),
        in_specs=[
            pl.BlockSpec((tm, tn), lambda i, j: (i, j)),
            pl.BlockSpec((tm, D_POOL), lambda i, j: (i, 0)),
            pl.BlockSpec((tm, D_ATTN), lambda i, j: (i, 0)),
            pl.BlockSpec((D_POOL, tn), lambda i, j: (0, j)),
            pl.BlockSpec((D_ATTN, tn), lambda i, j: (D_POOL // D_ATTN, j)),
        ],
        out_specs=pl.BlockSpec((tm, tn), lambda i, j: (i, j)),
        compiler_params=_compiler_params(("parallel", "arbitrary")),
        name="out_proj",
    )(x, pool, attn, w_out, w_out)


def _proj_gain(g_q, g_k, g_idx_k):
    return jnp.concatenate([
        jnp.ones((D_POOL,), F32), jnp.tile(g_q, N_HEADS), jnp.tile(g_k, N_KV_HEADS),
        jnp.ones((C_KI - C_V,), F32), g_idx_k, jnp.ones((D_IN_PAD - C_WI,), F32)]).reshape(1, D_IN_PAD)


def _pad_rows(a, rows):
    return jnp.pad(a, ((0, 0), (0, rows - a.shape[1]), (0, 0)))


def kernel(x_prompt, x_sample, cache_k, cache_v, cache_idx_k, state_pool, page_table, g_ffn1, w_gate1, w_up1,
           w_down1, g_mix, w_in, g_q, g_k, g_idx_k, w_pool, pool_scale, w_out, rel_bias, g_ffn2, w_gate2,
           w_up2, w_down2):
    batch, seq, d = x_prompt.shape
    bd, t_new, _ = x_sample.shape
    depth = g_ffn1.shape[0]
    assert depth == 1 and d == D_MODEL
    past = page_table.shape[1] * cache_k.shape[2]
    mp, ms = batch * seq, bd * t_new
    tm_p, tm_s = 512, ms
    tf = 256

    xp = x_prompt.reshape(mp, d)
    xs = x_sample.reshape(ms, d)
    l = 0

    def ffn(x, tm, g, wg, wu, wd):
        return _ffn(x, g, wg, wu, wd, tm, tf)

    wg1, wu1, wd1 = (w[l].astype(BF16) for w in (w_gate1, w_up1, w_down1))
    xp = ffn(xp, tm_p, g_ffn1[l], wg1, wu1, wd1)
    xs = ffn(xs, tm_s, g_ffn1[l], wg1, wu1, wd1)

    w_in_p = jnp.pad(w_in[l].astype(BF16), ((0, 0), (0, D_IN_PAD - D_IN)))
    gain = _proj_gain(g_q[l], g_k[l], g_idx_k[l])
    up = _proj(xp, g_mix[l], w_in_p, gain, tm_p)
    us = _proj(xs, g_mix[l], w_in_p, gain, tm_s)

    wp = w_pool[l].astype(BF16)
    scale = pool_scale[l].reshape(1, D_POOL)
    bias = _bias_tiles(rel_bias)

    pool_p = _pool_prompt(up, wp, scale, batch, seq, 256)
    attn_p = _attn_prompt(up, bias, batch, seq)

    hist = state_pool[l]
    u_pool_s = us[:, :D_POOL].reshape(bd, t_new, D_POOL)
    ext = jnp.concatenate([hist, u_pool_s], axis=1)
    pool_s = _pool_sample(ext.transpose(1, 0, 2), wp, scale, t_new)
    pool_s = pool_s.transpose(1, 0, 2).reshape(ms, D_POOL)

    us3 = us.reshape(bd, t_new, D_IN_PAD)
    rows_i = t_new * GROUP * IDX_HEADS
    qi_s = us3[:, :, C_QI:C_KI].reshape(bd, t_new, 1, IDX_HEADS, IDX_DIM)
    qi_rep = jnp.broadcast_to(qi_s, (bd, t_new, GROUP, IDX_HEADS, IDX_DIM)).reshape(bd, rows_i, IDX_DIM)
    wi_s = us3[:, :, C_WI:C_WI + IDX_HEADS].reshape(bd, t_new, 1, IDX_HEADS)
    w_col = jnp.broadcast_to(wi_s, (bd, t_new, GROUP, IDX_HEADS)).reshape(bd, rows_i, 1)
    ki_new = _pad_rows(us3[:, :, C_KI:C_WI], SUBLANES)
    k_new = _pad_rows(us3[:, :, C_K:C_V], SUBLANES)
    v_new = _pad_rows(us3[:, :, C_V:C_QI], SUBLANES)
    n_phys, page = cache_k.shape[1], cache_k.shape[2]
    topk_s = min(TOPK_MAX, (past + t_new) // 4)
    keep_s = _sample_index(page_table, cache_idx_k[l], qi_rep, w_col, ki_new, past, topk_s)

    nrow = t_new * GROUP
    q_rows = us3[:, :, C_Q:C_K].reshape(bd, t_new, N_KV_HEADS, GROUP, HEAD_DIM)
    q_rows = q_rows.transpose(0, 2, 1, 3, 4).reshape(bd, N_KV_HEADS, nrow, HEAD_DIM)
    tail = bias[:, :t_new, :].reshape(N_KV_HEADS, GROUP, t_new, BIAS_W)
    tail = tail.transpose(0, 2, 1, 3).reshape(N_KV_HEADS, nrow, BIAS_W)
    rows_shape = (depth, n_phys, page * N_KV_HEADS, HEAD_DIM)
    new_shape = (bd, SUBLANES * N_KV_HEADS, HEAD_DIM)
    attn_s = _sample_attend(page_table, cache_k.reshape(rows_shape), cache_v.reshape(rows_shape), l, page, q_rows,
                            k_new.reshape(new_shape), v_new.reshape(new_shape), keep_s, tail, past)
    attn_s = attn_s.reshape(bd, N_KV_HEADS, t_new, GROUP, HEAD_DIM).transpose(0, 2, 1, 3, 4).reshape(ms, D_ATTN)

    w_out_bf = w_out[l].astype(BF16)
    xp = _out_proj(xp, pool_p, attn_p, w_out_bf, tm_p, 512)
    xs = _out_proj(xs, pool_s, attn_s, w_out_bf, tm_s, 512)

    wg2, wu2, wd2 = (w[l].astype(BF16) for w in (w_gate2, w_up2, w_down2))
    xp = ffn(xp, tm_p, g_ffn2[l], wg2, wu2, wd2)
    xs = ffn(xs, tm_s, g_ffn2[l], wg2, wu2, wd2)

    up4 = up.reshape(batch, seq, D_IN_PAD)
    return (
        xp.reshape(batch, seq, d),
        xs.reshape(bd, t_new, d),
        up4[:, :, C_K:C_V].reshape(1, batch, seq, N_KV_HEADS, HEAD_DIM),
        up4[:, :, C_V:C_QI].reshape(1, batch, seq, N_KV_HEADS, HEAD_DIM),
        up4[:, :, C_KI:C_WI].reshape(1, batch, seq, IDX_DIM),
        up4[:, seq - POOL_HIST:, :D_POOL].reshape(1, batch, POOL_HIST, D_POOL),
        us3[:, :, C_K:C_V].reshape(1, bd, t_new, N_KV_HEADS, HEAD_DIM),
        us3[:, :, C_V:C_QI].reshape(1, bd, t_new, N_KV_HEADS, HEAD_DIM),
        us3[:, :, C_KI:C_WI].reshape(1, bd, t_new, IDX_DIM),
        ext[:, t_new:].reshape(1, bd, POOL_HIST, D_POOL),
    )
```

```python
import functools
import math

import jax
import jax.numpy as jnp
import numpy as np
from jax import lax
from jax.experimental import pallas as pl
from jax.experimental.pallas import tpu as pltpu

F32 = jnp.float32
BF16 = jnp.bfloat16
I32 = jnp.int32

SUBLANES = 8
LANES = 128
VMEM_LIMIT_BYTES = 56 * 1024 * 1024

D_MODEL = 4096
D_POOL = D_MODEL // 2
POOL_WINDOWS = (2, 4, 8, 16)
N_POOL_GROUPS = len(POOL_WINDOWS)
POOL_GROUP = D_POOL // N_POOL_GROUPS
POOL_HIST = max(POOL_WINDOWS) - 1
HEAD_DIM = 128
N_HEADS = (D_MODEL - D_POOL) // HEAD_DIM
N_KV_HEADS = 4
GROUP = N_HEADS // N_KV_HEADS
D_ATTN = N_HEADS * HEAD_DIM
D_KV = N_KV_HEADS * HEAD_DIM
IDX_HEADS = 8
IDX_DIM = 128
TOPK_MAX = 256
NUM_BUCKETS = 32
MAX_DISTANCE = 128
EPS = 1e-6
NEG = -1e30

C_Q = D_POOL
C_K = C_Q + D_ATTN
C_V = C_K + D_KV
C_QI = C_V + D_KV
C_KI = C_QI + IDX_HEADS * IDX_DIM
C_WI = C_KI + IDX_DIM
D_IN = C_WI + IDX_HEADS
D_IN_PAD = ((D_IN + 2 * LANES - 1) // (2 * LANES)) * (2 * LANES)
PROJ_TN = 10 * LANES
assert D_IN_PAD % PROJ_TN == 0

INT_MIN = -(2 ** 31)
KEY_NEG_INF = INT_MIN + 0x007FFFFF


def _compiler_params(semantics):
    return pltpu.CompilerParams(dimension_semantics=semantics, vmem_limit_bytes=VMEM_LIMIT_BYTES)


def _dot(a, b):
    return jnp.dot(a, b, preferred_element_type=F32)


def _dot_nt(a, b):
    return lax.dot_general(a, b, (((1,), (1,)), ((), ())), preferred_element_type=F32)


def _rms_scale(x):
    return lax.rsqrt(jnp.mean(x * x, axis=-1, keepdims=True) + EPS)


def _split_bf16(x):
    hi = x.astype(BF16)
    lo = (x - hi.astype(F32)).astype(BF16)
    return hi, lo


def _ffn_step(j, last, x_ref, g_ref, wg, wu, wd, o_ref, h_ref):
    @pl.when(j == 0)
    def _():
        x = x_ref[...]
        h_ref[...] = (x * _rms_scale(x) * g_ref[...]).astype(BF16)
        o_ref[...] = jnp.zeros_like(o_ref)

    h = h_ref[...]
    a = _dot(h, wg)
    b = _dot(h, wu)
    s = (a * jax.nn.sigmoid(a) * b).astype(BF16)
    o_ref[...] += _dot(s, wd)

    @pl.when(j == last)
    def _():
        o_ref[...] = x_ref[...] + 0.5 * o_ref[...]


def _ffn_kernel(x_ref, g_ref, wg_ref, wu_ref, wd_ref, o_ref, h_ref):
    _ffn_step(pl.program_id(1), pl.num_programs(1) - 1, x_ref, g_ref, wg_ref[...], wu_ref[...], wd_ref[...],
              o_ref, h_ref)


def _ffn_cast_kernel(x_ref, g_ref, wg_ref, wu_ref, wd_ref, o_ref, wg_o, wu_o, wd_o, h_ref):
    wg = wg_ref[...].astype(BF16)
    wu = wu_ref[...].astype(BF16)
    wd = wd_ref[...].astype(BF16)
    wg_o[...] = wg
    wu_o[...] = wu
    wd_o[...] = wd
    _ffn_step(pl.program_id(0), pl.num_programs(0) - 1, x_ref, g_ref, wg, wu, wd, o_ref, h_ref)


def _ffn_cast(x, g, wg, wu, wd, tf):
    m, d = x.shape
    f = wg.shape[1]
    w_in = lambda: pl.BlockSpec((d, tf), lambda j: (0, j))
    w_out = lambda: pl.BlockSpec((tf, d), lambda j: (j, 0))
    return pl.pallas_call(
        _ffn_cast_kernel,
        out_shape=[jax.ShapeDtypeStruct((m, d), F32), jax.ShapeDtypeStruct((d, f), BF16),
                   jax.ShapeDtypeStruct((d, f), BF16), jax.ShapeDtypeStruct((f, d), BF16)],
        grid=(f // tf,),
        in_specs=[pl.BlockSpec((m, d), lambda j: (0, 0)), pl.BlockSpec((1, d), lambda j: (0, 0)),
                  w_in(), w_in(), w_out()],
        out_specs=[pl.BlockSpec((m, d), lambda j: (0, 0)), w_in(), w_in(), w_out()],
        scratch_shapes=[pltpu.VMEM((m, d), BF16)],
        compiler_params=_compiler_params(("arbitrary",)),
        name="swiglu_half_cast",
    )(x, g.reshape(1, d), wg, wu, wd)


def _ffn(x, g, wg, wu, wd, tm, tf):
    m, d = x.shape
    f = wg.shape[1]
    return pl.pallas_call(
        _ffn_kernel,
        out_shape=jax.ShapeDtypeStruct((m, d), F32),
        grid=(m // tm, f // tf),
        in_specs=[
            pl.BlockSpec((tm, d), lambda i, j: (i, 0), pipeline_mode=pl.Buffered(1)),
            pl.BlockSpec((1, d), lambda i, j: (0, 0)),
            pl.BlockSpec((d, tf), lambda i, j: (0, j)),
            pl.BlockSpec((d, tf), lambda i, j: (0, j)),
            pl.BlockSpec((tf, d), lambda i, j: (j, 0)),
        ],
        out_specs=pl.BlockSpec((tm, d), lambda i, j: (i, 0)),
        scratch_shapes=[pltpu.VMEM((tm, d), BF16)],
        compiler_params=_compiler_params(("parallel", "arbitrary")),
        name="swiglu_half",
    )(x, g.reshape(1, d), wg, wu, wd)


def _proj_kernel(x_ref, g_ref, w_ref, gain_ref, o_ref, h_ref):
    j = pl.program_id(1)

    @pl.when(j == 0)
    def _():
        x = x_ref[...]
        h_ref[...] = (x * _rms_scale(x) * g_ref[...]).astype(BF16)

    u = _dot(h_ref[...], w_ref[...])
    gain = gain_ref[...]

    def is_normed(col):
        return C_Q <= col < C_V or C_KI <= col < C_WI

    for jt in range(D_IN_PAD // PROJ_TN):
        @pl.when(j == jt)
        def _(jt=jt):
            c = 0
            while c < PROJ_TN:
                if is_normed(jt * PROJ_TN + c):
                    part = u[:, c:c + LANES]
                    o_ref[:, c:c + LANES] = part * _rms_scale(part) * gain[:, c:c + LANES]
                    c += LANES
                else:
                    end = c
                    while end < PROJ_TN and not is_normed(jt * PROJ_TN + end):
                        end += LANES
                    o_ref[:, c:end] = u[:, c:end]
                    c = end


def _proj(x, g_mix, w_in, gain, tm):
    m, d = x.shape
    n = w_in.shape[1]
    return pl.pallas_call(
        _proj_kernel,
        out_shape=jax.ShapeDtypeStruct((m, n), F32),
        grid=(m // tm, n // PROJ_TN),
        in_specs=[
            pl.BlockSpec((tm, d), lambda i, j: (i, 0), pipeline_mode=pl.Buffered(1)),
            pl.BlockSpec((1, d), lambda i, j: (0, 0)),
            pl.BlockSpec((d, PROJ_TN), lambda i, j: (0, j)),
            pl.BlockSpec((1, PROJ_TN), lambda i, j: (0, j)),
        ],
        out_specs=pl.BlockSpec((tm, PROJ_TN), lambda i, j: (i, j)),
        scratch_shapes=[pltpu.VMEM((tm, d), BF16)],
        compiler_params=_compiler_params(("parallel", "arbitrary")),
        name="in_proj",
    )(x, g_mix.reshape(1, d), w_in, gain)


HALO = 2 * SUBLANES


def _pool_groups(ext_rows, cur, count_of, wp_ref, scale_ref, store):
    for g, w in enumerate(POOL_WINDOWS):
        lo, hi = g * POOL_GROUP, (g + 1) * POOL_GROUP
        acc = ext_rows(0, lo, hi)
        for d in range(1, w):
            acc = acc + ext_rows(d, lo, hi)
        diff = acc / count_of(w) - cur(lo, hi)
        out = _dot(diff.astype(BF16), wp_ref[g]) * scale_ref[:, lo:hi]
        store(lo, hi, out)


def _pool_prompt_kernel(u_ref, halo_ref, wp_ref, scale_ref, o_ref, ext_ref, *, tp):
    i = pl.program_id(1)
    halo = halo_ref[...]
    ext_ref[:HALO, :] = jnp.where(i == 0, jnp.zeros_like(halo), halo)
    ext_ref[HALO:, :] = u_ref[...]
    pos = i * tp + lax.broadcasted_iota(I32, (tp, POOL_GROUP), 0)

    def store(lo, hi, out):
        o_ref[:, lo:hi] = out.astype(o_ref.dtype)

    _pool_groups(
        lambda d, lo, hi: ext_ref[HALO - d:HALO - d + tp, lo:hi],
        lambda lo, hi: u_ref[:, lo:hi],
        lambda w: jnp.minimum(pos + 1, w).astype(F32),
        wp_ref, scale_ref, store)


def _pool_prompt(u_all, wp, scale, batch, seq, tp):
    nt = seq // tp
    kern = functools.partial(_pool_prompt_kernel, tp=tp)
    return pl.pallas_call(
        kern,
        out_shape=jax.ShapeDtypeStruct((batch * seq, D_POOL), BF16),
        grid=(batch, nt),
        in_specs=[
            pl.BlockSpec((tp, D_POOL), lambda b, i: (b * nt + i, 0)),
            pl.BlockSpec((HALO, D_POOL),
                         lambda b, i: (jnp.maximum((b * seq + i * tp) // HALO - 1, 0), 0)),
            pl.BlockSpec((N_POOL_GROUPS, POOL_GROUP, POOL_GROUP), lambda b, i: (0, 0, 0)),
            pl.BlockSpec((1, D_POOL), lambda b, i: (0, 0)),
        ],
        out_specs=pl.BlockSpec((tp, D_POOL), lambda b, i: (b * nt + i, 0)),
        scratch_shapes=[pltpu.VMEM((HALO + tp, D_POOL), F32)],
        compiler_params=_compiler_params(("parallel", "arbitrary")),
        name="pool_prompt",
    )(u_all, u_all, wp, scale)


def _pool_sample_kernel(ext_ref, wp_ref, scale_ref, o_ref, *, t_new):
    for t in range(t_new):
        def store(lo, hi, out, t=t):
            o_ref[t, :, lo:hi] = out.astype(o_ref.dtype)

        _pool_groups(
            lambda d, lo, hi, t=t: ext_ref[POOL_HIST + t - d, :, lo:hi],
            lambda lo, hi, t=t: ext_ref[POOL_HIST + t, :, lo:hi],
            lambda w: float(w),
            wp_ref, scale_ref, store)


def _pool_sample(ext_t, wp, scale, t_new):
    _, bd, _ = ext_t.shape
    kern = functools.partial(_pool_sample_kernel, t_new=t_new)
    return pl.pallas_call(
        kern,
        out_shape=jax.ShapeDtypeStruct((t_new, bd, D_POOL), BF16),
        compiler_params=pltpu.CompilerParams(vmem_limit_bytes=VMEM_LIMIT_BYTES),
        name="pool_sample",
    )(ext_t, wp, scale)


def _bucket_thresholds():
    n = np.arange(0, 4 * MAX_DISTANCE, dtype=np.int64)
    max_exact = NUM_BUCKETS // 2
    nf = np.maximum(n, 1).astype(np.float32)
    large = max_exact + (np.log(nf / np.float32(max_exact)) / np.float32(math.log(MAX_DISTANCE / max_exact))
                         * np.float32(NUM_BUCKETS - max_exact)).astype(np.int32)
    large = np.minimum(large, NUM_BUCKETS - 1)
    bucket = np.where(n < max_exact, n, large)
    assert np.all(np.diff(bucket) >= 0) and bucket[-1] == NUM_BUCKETS - 1
    return [int(np.argmax(bucket >= b)) for b in range(1, NUM_BUCKETS)]


BUCKET_THRESHOLDS = _bucket_thresholds()
FAR_DISTANCE = BUCKET_THRESHOLDS[-1]
BIAS_W = 2 * LANES
assert FAR_DISTANCE <= LANES


def _bias_kernel(rb_ref, o_ref):
    r = lax.broadcasted_iota(I32, (LANES, BIAS_W), 0)
    c = lax.broadcasted_iota(I32, (LANES, BIAS_W), 1)
    d = LANES + r - c
    bucket = jnp.zeros((LANES, BIAS_W), I32)
    for thr in BUCKET_THRESHOLDS:
        bucket = bucket + (d >= thr).astype(I32)
    for h in range(N_HEADS):
        far = rb_ref[NUM_BUCKETS - 1, h]
        val = jnp.zeros((LANES, BIAS_W), F32)
        for b in range(NUM_BUCKETS - 1):
            val = jnp.where(bucket == b, rb_ref[b, h] - far, val)
        o_ref[h] = jnp.where(d >= 0, val, 0.0)


def _bias_tiles(rel_bias):
    return pl.pallas_call(
        _bias_kernel,
        out_shape=jax.ShapeDtypeStruct((N_HEADS, LANES, BIAS_W), F32),
        in_specs=[pl.BlockSpec(memory_space=pltpu.SMEM)],
        name="rel_bias_tiles",
    )(rel_bias)


def _decode_key(key):
    bits = jnp.where(key < 0, key ^ 0x7FFFFFFF, key)
    return lax.bitcast_convert_type(bits, F32)


def _count(mask):
    return jnp.sum(jnp.where(mask, 1.0, 0.0), axis=1, keepdims=True)


def _topk_mask(score, admissible, kpos, need):
    rows, n = score.shape
    score = jnp.where(admissible, score, -jnp.inf)
    need_f = float(need)

    nonneg = _count(score >= 0.0) >= need_f
    lo0 = jnp.where(nonneg, 0, INT_MIN).astype(I32)

    def bit_step(it, lo):
        cand = lo | (jnp.int32(1) << (30 - it))
        ok = _count(score >= _decode_key(cand)) >= need_f
        return jnp.where(ok, cand, lo)

    lo = lax.fori_loop(0, 31, bit_step, lo0)
    thr = jnp.where(lo <= KEY_NEG_INF, -jnp.inf, _decode_key(lo))

    gt = score > thr
    eq = score == thr
    n_gt = _count(gt)
    spare = need_f - n_gt
    nbits = int(n).bit_length()

    def trim_ties():
        def idx_step(it, lim):
            cand = lim + (jnp.int32(1) << (nbits - 1 - it))
            ok = _count(eq & (kpos < cand)) <= spare
            return jnp.where(ok, cand, lim)
        return lax.fori_loop(0, nbits, idx_step, jnp.zeros((rows, 1), I32))

    overshoot = jnp.max(n_gt + _count(eq)) > need_f
    lim = lax.cond(overshoot, trim_ties, lambda: jnp.full((rows, 1), 2 ** nbits, I32))
    return (gt | (eq & (kpos < lim))) & admissible


def _attn_prompt_body(i, ext, first, q_ref, qi_ref, wi_ref, bias_ref, o_ref, k_bf, v_bf, ki_hi, ki_lo, s_ref,
                      tq, topk):
    wi = wi_ref[...] * (IDX_DIM ** -0.5 * IDX_HEADS ** -0.5)
    khi = ki_hi[:ext, :]
    klo = ki_lo[:ext, :]
    score = jnp.zeros((tq, ext), F32)
    for h in range(IDX_HEADS):
        qhi, qlo = _split_bf16(qi_ref[:, h * IDX_DIM:(h + 1) * IDX_DIM])
        s = _dot_nt(qhi, khi) + (_dot_nt(qhi, klo) + _dot_nt(qlo, khi))
        score = score + jnp.maximum(s, 0.0) * wi[:, h:h + 1]

    kpos = lax.broadcasted_iota(I32, (tq, ext), 1)
    qpos = i * tq + lax.broadcasted_iota(I32, (tq, ext), 0)
    keep = _topk_mask(score, kpos <= qpos, kpos, topk)

    for n in range(N_KV_HEADS):
        kn = k_bf[:ext, n * HEAD_DIM:(n + 1) * HEAD_DIM]
        vn = v_bf[:ext, n * HEAD_DIM:(n + 1) * HEAD_DIM]
        for g in range(GROUP):
            h = n * GROUP + g
            qh = (q_ref[:, h * HEAD_DIM:(h + 1) * HEAD_DIM] * HEAD_DIM ** -0.5).astype(BF16)
            s_ref[:, :ext] = _dot_nt(qh, kn)
            def add_near_bias(h=h):
                ws = pl.multiple_of((i - 1) * tq, LANES)
                s_ref[:, pl.ds(ws, BIAS_W)] += bias_ref[h]

            if first:
                @pl.when(i == 0)
                def _():
                    s_ref[:, :tq] += bias_ref[h, :, LANES:]

                pl.when(i > 0)(add_near_bias)
            else:
                add_near_bias()

            s = jnp.where(keep, s_ref[:, :ext], NEG)
            p = jnp.exp(s - jnp.max(s, axis=1, keepdims=True))
            l = jnp.sum(p, axis=1, keepdims=True)
            o = _dot(p.astype(BF16), vn) / l
            o_ref[:, h * HEAD_DIM:(h + 1) * HEAD_DIM] = o.astype(o_ref.dtype)


def _attn_prompt_kernel(q_ref, k_ref, v_ref, qi_ref, ki_ref, wi_ref, bias_ref, o_ref,
                        k_bf, v_bf, ki_hi, ki_lo, s_ref, *, tq, topk, ext, tile0):
    j = pl.program_id(1)

    @pl.when(j == 0)
    def _():
        k_bf[...] = k_ref[:ext, :].astype(BF16)
        v_bf[...] = v_ref[:ext, :].astype(BF16)
        hi, lo = _split_bf16(ki_ref[:ext, :])
        ki_hi[...] = hi
        ki_lo[...] = lo

    _attn_prompt_body(tile0 + j, ext, tile0 == 0, q_ref, qi_ref, wi_ref, bias_ref, o_ref, k_bf, v_bf, ki_hi, ki_lo,
                      s_ref, tq, topk)


def _attn_prompt_span(u_all, bias, batch, seq, tile0, ntile, tq, topk):
    nq = seq // tq
    ext = (tile0 + ntile) * tq
    kern = functools.partial(_attn_prompt_kernel, tq=tq, topk=topk, ext=ext, tile0=tile0)
    row = lambda b, j: b * nq + tile0 + j
    return pl.pallas_call(
        kern,
        out_shape=jax.ShapeDtypeStruct((batch * ntile * tq, D_ATTN), BF16),
        grid=(batch, ntile),
        in_specs=[
            pl.BlockSpec((tq, D_ATTN), lambda b, j: (row(b, j), C_Q // D_ATTN)),
            pl.BlockSpec((seq, D_KV), lambda b, j: (b, C_K // D_KV)),
            pl.BlockSpec((seq, D_KV), lambda b, j: (b, C_V // D_KV)),
            pl.BlockSpec((tq, IDX_HEADS * IDX_DIM), lambda b, j: (row(b, j), C_QI // (IDX_HEADS * IDX_DIM))),
            pl.BlockSpec((seq, IDX_DIM), lambda b, j: (b, C_KI // IDX_DIM)),
            pl.BlockSpec((tq, LANES), lambda b, j: (row(b, j), C_WI // LANES)),
            pl.BlockSpec((N_HEADS, LANES, BIAS_W), lambda b, j: (0, 0, 0)),
        ],
        out_specs=pl.BlockSpec((tq, D_ATTN), lambda b, j: (b * ntile + j, 0)),
        scratch_shapes=[
            pltpu.VMEM((ext, D_KV), BF16),
            pltpu.VMEM((ext, D_KV), BF16),
            pltpu.VMEM((ext, IDX_DIM), BF16),
            pltpu.VMEM((ext, IDX_DIM), BF16),
            pltpu.VMEM((tq, ext), F32),
        ],
        compiler_params=_compiler_params(("parallel", "arbitrary")),
        name=f"attn_prompt_{ext}",
    )(u_all, u_all, u_all, u_all, u_all, u_all, bias)


ATTN_SPANS = 4


def _attn_prompt(u_all, bias, batch, seq):
    tq = LANES
    nq = seq // tq
    topk = min(TOPK_MAX, seq // 4)
    ntile = nq // ATTN_SPANS
    assert ntile * ATTN_SPANS == nq and ntile >= 2
    spans = [_attn_prompt_span(u_all, bias, batch, seq, e * ntile, ntile, tq, topk).reshape(batch, ntile * tq, D_ATTN)
             for e in range(ATTN_SPANS)]
    return jnp.concatenate(spans, axis=1).reshape(batch * seq, D_ATTN)


def _gather_pages(pt_ref, b, n_pages, page, srcs_dsts_sems):
    def copy(src, dst, sem, p, pg):
        return pltpu.make_async_copy(src.at[pg], dst.at[pl.ds(pl.multiple_of(p * page, page), page)], sem)

    def start(p, carry):
        pg = pt_ref[b, p]
        for src, dst, sem in srcs_dsts_sems:
            copy(src, dst, sem, p, pg).start()
        return carry

    lax.fori_loop(0, n_pages, start, 0)

    def wait_all():
        def wait(p, carry):
            for src, dst, sem in srcs_dsts_sems:
                copy(src, dst, sem, p, 0).wait()
            return carry
        lax.fori_loop(0, n_pages, wait, 0)

    return wait_all


def _sample_index_kernel(pt_ref, cache_ref, qi_ref, w_ref, kinew_ref, o_ref,
                         ki_all, sc_ref, sem, *, n_pages, page, past, chunk, topk, group_rows):
    b = pl.program_id(0)
    total = ki_all.shape[0]
    wait_all = _gather_pages(pt_ref, b, n_pages, page, [(cache_ref, ki_all, sem.at[0])])
    ki_all[past:past + SUBLANES, :] = kinew_ref[0]
    ki_all[past + SUBLANES:, :] = jnp.zeros((total - past - SUBLANES, IDX_DIM), F32)
    wait_all()

    qhi, qlo = _split_bf16(qi_ref[0])
    w = w_ref[0] * (IDX_DIM ** -0.5 * IDX_HEADS ** -0.5)
    rows = qi_ref.shape[1]
    for c0 in range(0, total, chunk):
        khi, klo = _split_bf16(ki_all[c0:c0 + chunk, :])
        s = _dot_nt(qhi, khi) + (_dot_nt(qhi, klo) + _dot_nt(qlo, khi))
        s = jnp.maximum(s, 0.0) * w
        sc_ref[:, c0:c0 + chunk] = jnp.sum(s.reshape(rows // IDX_HEADS, IDX_HEADS, chunk), axis=1)

    nrow = rows // IDX_HEADS
    kpos = lax.broadcasted_iota(I32, (nrow, total), 1)
    qpos = past + (lax.broadcasted_iota(I32, (nrow, total), 0) >> (group_rows.bit_length() - 1))
    keep = _topk_mask(sc_ref[...], kpos <= qpos, kpos, topk)
    o_ref[0] = jnp.where(keep, 1.0, 0.0)


def _sample_index(page_table, cache_idx, qi_rep, w_col, ki_new, past, topk):
    bd, n_pages = page_table.shape
    page = cache_idx.shape[1]
    total = past + LANES
    rows = qi_rep.shape[1]
    nrow = rows // IDX_HEADS
    chunk = total // 5
    assert chunk * 5 == total and chunk % LANES == 0
    kern = functools.partial(_sample_index_kernel, n_pages=n_pages, page=page, past=past, chunk=chunk,
                             topk=topk, group_rows=GROUP)
    return pl.pallas_call(
        kern,
        out_shape=jax.ShapeDtypeStruct((bd, nrow, total), F32),
        grid_spec=pltpu.PrefetchScalarGridSpec(
            num_scalar_prefetch=1,
            grid=(bd,),
            in_specs=[
                pl.BlockSpec(memory_space=pl.ANY),
                pl.BlockSpec((1, rows, IDX_DIM), lambda b, pt: (b, 0, 0)),
                pl.BlockSpec((1, rows, 1), lambda b, pt: (b, 0, 0)),
                pl.BlockSpec((1, SUBLANES, IDX_DIM), lambda b, pt: (b, 0, 0)),
            ],
            out_specs=pl.BlockSpec((1, nrow, total), lambda b, pt: (b, 0, 0)),
            scratch_shapes=[
                pltpu.VMEM((total, IDX_DIM), F32),
                pltpu.VMEM((nrow, total), F32),
                pltpu.SemaphoreType.DMA((1,)),
            ],
        ),
        compiler_params=_compiler_params(("arbitrary",)),
        name="sample_index",
    )(page_table, cache_idx, qi_rep, w_col, ki_new)


def _sample_attend_kernel(pt_ref, ck_ref, cv_ref, q_ref, knew_ref, vnew_ref, keep_ref, tail_ref,
                          o_ref, k_all, v_all, sem, *, layer, n_pages, page, past):
    b = pl.program_id(0)
    total = k_all.shape[0] // N_KV_HEADS
    new0, new1 = past * N_KV_HEADS, (past + SUBLANES) * N_KV_HEADS
    wait_all = _gather_pages(pt_ref, b, n_pages, page * N_KV_HEADS,
                             [(ck_ref.at[layer], k_all, sem.at[0]), (cv_ref.at[layer], v_all, sem.at[1])])
    pad = jnp.zeros((total * N_KV_HEADS - new1, HEAD_DIM), F32)
    k_all[new0:new1, :] = knew_ref[0]
    v_all[new0:new1, :] = vnew_ref[0]
    k_all[new1:, :] = pad
    v_all[new1:, :] = pad
    wait_all()

    keep = keep_ref[0] > 0.5
    near = total - BIAS_W
    for n in range(N_KV_HEADS):
        kn = k_all[pl.ds(n, total, stride=N_KV_HEADS), :].astype(BF16)
        vn = v_all[pl.ds(n, total, stride=N_KV_HEADS), :].astype(BF16)
        s = _dot_nt((q_ref[0, n] * HEAD_DIM ** -0.5).astype(BF16), kn)
        s = jnp.concatenate([s[:, :near], s[:, near:] + tail_ref[n]], axis=1)
        s = jnp.where(keep, s, NEG)
        p = jnp.exp(s - jnp.max(s, axis=1, keepdims=True))
        l = jnp.sum(p, axis=1, keepdims=True)
        o_ref[0, n] = (_dot(p.astype(BF16), vn) / l).astype(o_ref.dtype)


def _sample_attend(page_table, cache_k, cache_v, layer, page, q_rows, k_new, v_new, keep, tail, past):
    bd, n_pages = page_table.shape
    total = past + LANES
    nrow = q_rows.shape[2]
    kern = functools.partial(_sample_attend_kernel, layer=layer, n_pages=n_pages, page=page, past=past)
    return pl.pallas_call(
        kern,
        out_shape=jax.ShapeDtypeStruct((bd, N_KV_HEADS, nrow, HEAD_DIM), BF16),
        grid_spec=pltpu.PrefetchScalarGridSpec(
            num_scalar_prefetch=1,
            grid=(bd,),
            in_specs=[
                pl.BlockSpec(memory_space=pl.ANY),
                pl.BlockSpec(memory_space=pl.ANY),
                pl.BlockSpec((1, N_KV_HEADS, nrow, HEAD_DIM), lambda b, pt: (b, 0, 0, 0)),
                pl.BlockSpec((1, SUBLANES * N_KV_HEADS, HEAD_DIM), lambda b, pt: (b, 0, 0)),
                pl.BlockSpec((1, SUBLANES * N_KV_HEADS, HEAD_DIM), lambda b, pt: (b, 0, 0)),
                pl.BlockSpec((1, nrow, total), lambda b, pt: (b, 0, 0)),
                pl.BlockSpec((N_KV_HEADS, nrow, BIAS_W), lambda b, pt: (0, 0, 0)),
            ],
            out_specs=pl.BlockSpec((1, N_KV_HEADS, nrow, HEAD_DIM), lambda b, pt: (b, 0, 0, 0)),
            scratch_shapes=[
                pltpu.VMEM((total * N_KV_HEADS, HEAD_DIM), F32),
                pltpu.VMEM((total * N_KV_HEADS, HEAD_DIM), F32),
                pltpu.SemaphoreType.DMA((2,)),
            ],
        ),
        compiler_params=_compiler_params(("arbitrary",)),
        name="sample_attend",
    )(page_table, cache_k, cache_v, q_rows, k_new, v_new, keep, tail)


def _out_proj_kernel(x_ref, pool_ref, attn_ref, wp_ref, wa_ref, o_ref):
    o_ref[...] = x_ref[...] + (_dot(pool_ref[...], wp_ref[...]) + _dot(attn_ref[...], wa_ref[...]))


def _out_proj(x, pool, attn, w_out, tm, tn):
    m, d = x.shape
    return pl.pallas_call(
        _out_proj_kernel,
        out_shape=jax.ShapeDtypeStruct((m, d), F32),
        grid=(m // tm, d // tn),
        in_specs=[
            pl.BlockSpec((tm, tn), lambda i, j: (i, j)),
            pl.BlockSpec((tm, D_POOL), lambda i, j: (i, 0)),
            pl.BlockSpec((tm, D_ATTN), lambda i, j: (i, 0)),
            pl.BlockSpec((D_POOL, tn), lambda i, j: (0, j)),
            pl.BlockSpec((D_ATTN, tn), lambda i, j: (D_POOL // D_ATTN, j)),
        ],
        out_specs=pl.BlockSpec((tm, tn), lambda i, j: (i, j)),
        compiler_params=_compiler_params(("parallel", "arbitrary")),
        name="out_proj",
    )(x, pool, attn, w_out, w_out)


def _proj_gain(g_q, g_k, g_idx_k):
    return jnp.concatenate([
        jnp.ones((D_POOL,), F32), jnp.tile(g_q, N_HEADS), jnp.tile(g_k, N_KV_HEADS),
        jnp.ones((C_KI - C_V,), F32), g_idx_k, jnp.ones((D_IN_PAD - C_WI,), F32)]).reshape(1, D_IN_PAD)


def _pad_rows(a, rows):
    return jnp.pad(a, ((0, 0), (0, rows - a.shape[1]), (0, 0)))


def kernel(x_prompt, x_sample, cache_k, cache_v, cache_idx_k, state_pool, page_table, g_ffn1, w_gate1, w_up1,
           w_down1, g_mix, w_in, g_q, g_k, g_idx_k, w_pool, pool_scale, w_out, rel_bias, g_ffn2, w_gate2,
           w_up2, w_down2):
    batch, seq, d = x_prompt.shape
    bd, t_new, _ = x_sample.shape
    depth = g_ffn1.shape[0]
    assert depth == 1 and d == D_MODEL
    past = page_table.shape[1] * cache_k.shape[2]
    mp, ms = batch * seq, bd * t_new
    tm_p, tm_s = 512, ms
    tf = 256

    xp = x_prompt.reshape(mp, d)
    xs = x_sample.reshape(ms, d)
    l = 0

    xs, wg1, wu1, wd1 = _ffn_cast(xs, g_ffn1[l], w_gate1[l], w_up1[l], w_down1[l], tf)
    xp = _ffn(xp, g_ffn1[l], wg1, wu1, wd1, tm_p, tf)

    w_in_p = jnp.pad(w_in[l].astype(BF16), ((0, 0), (0, D_IN_PAD - D_IN)))
    gain = _proj_gain(g_q[l], g_k[l], g_idx_k[l])
    up = _proj(xp, g_mix[l], w_in_p, gain, tm_p)
    us = _proj(xs, g_mix[l], w_in_p, gain, tm_s)

    wp = w_pool[l].astype(BF16)
    scale = pool_scale[l].reshape(1, D_POOL)
    bias = _bias_tiles(rel_bias)

    pool_p = _pool_prompt(up, wp, scale, batch, seq, 256)
    attn_p = _attn_prompt(up, bias, batch, seq)

    hist = state_pool[l]
    u_pool_s = us[:, :D_POOL].reshape(bd, t_new, D_POOL)
    ext = jnp.concatenate([hist, u_pool_s], axis=1)
    pool_s = _pool_sample(ext.transpose(1, 0, 2), wp, scale, t_new)
    pool_s = pool_s.transpose(1, 0, 2).reshape(ms, D_POOL)

    us3 = us.reshape(bd, t_new, D_IN_PAD)
    rows_i = t_new * GROUP * IDX_HEADS
    qi_s = us3[:, :, C_QI:C_KI].reshape(bd, t_new, 1, IDX_HEADS, IDX_DIM)
    qi_rep = jnp.broadcast_to(qi_s, (bd, t_new, GROUP, IDX_HEADS, IDX_DIM)).reshape(bd, rows_i, IDX_DIM)
    wi_s = us3[:, :, C_WI:C_WI + IDX_HEADS].reshape(bd, t_new, 1, IDX_HEADS)
    w_col = jnp.broadcast_to(wi_s, (bd, t_new, GROUP, IDX_HEADS)).reshape(bd, rows_i, 1)
    ki_new = _pad_rows(us3[:, :, C_KI:C_WI], SUBLANES)
    k_new = _pad_rows(us3[:, :, C_K:C_V], SUBLANES)
    v_new = _pad_rows(us3[:, :, C_V:C_QI], SUBLANES)
    n_phys, page = cache_k.shape[1], cache_k.shape[2]
    topk_s = min(TOPK_MAX, (past + t_new) // 4)
    keep_s = _sample_index(page_table, cache_idx_k[l], qi_rep, w_col, ki_new, past, topk_s)

    nrow = t_new * GROUP
    q_rows = us3[:, :, C_Q:C_K].reshape(bd, t_new, N_KV_HEADS, GROUP, HEAD_DIM)
    q_rows = q_rows.transpose(0, 2, 1, 3, 4).reshape(bd, N_KV_HEADS, nrow, HEAD_DIM)
    tail = bias[:, :t_new, :].reshape(N_KV_HEADS, GROUP, t_new, BIAS_W)
    tail = tail.transpose(0, 2, 1, 3).reshape(N_KV_HEADS, nrow, BIAS_W)
    rows_shape = (depth, n_phys, page * N_KV_HEADS, HEAD_DIM)
    new_shape = (bd, SUBLANES * N_KV_HEADS, HEAD_DIM)
    attn_s = _sample_attend(page_table, cache_k.reshape(rows_shape), cache_v.reshape(rows_shape), l, page, q_rows,
                            k_new.reshape(new_shape), v_new.reshape(new_shape), keep_s, tail, past)
    attn_s = attn_s.reshape(bd, N_KV_HEADS, t_new, GROUP, HEAD_DIM).transpose(0, 2, 1, 3, 4).reshape(ms, D_ATTN)

    w_out_bf = w_out[l].astype(BF16)
    xp = _out_proj(xp, pool_p, attn_p, w_out_bf, tm_p, 512)
    xs = _out_proj(xs, pool_s, attn_s, w_out_bf, tm_s, 512)

    xs, wg2, wu2, wd2 = _ffn_cast(xs, g_ffn2[l], w_gate2[l], w_up2[l], w_down2[l], tf)
    xp = _ffn(xp, g_ffn2[l], wg2, wu2, wd2, tm_p, tf)

    up4 = up.reshape(batch, seq, D_IN_PAD)
    return (
        xp.reshape(batch, seq, d),
        xs.reshape(bd, t_new, d),
        up4[:, :, C_K:C_V].reshape(1, batch, seq, N_KV_HEADS, HEAD_DIM),
        up4[:, :, C_V:C_QI].reshape(1, batch, seq, N_KV_HEADS, HEAD_DIM),
        up4[:, :, C_KI:C_WI].reshape(1, batch, seq, IDX_DIM),
        up4[:, seq - POOL_HIST:, :D_POOL].reshape(1, batch, POOL_HIST, D_POOL),
        us3[:, :, C_K:C_V].reshape(1, bd, t_new, N_KV_HEADS, HEAD_DIM),
        us3[:, :, C_V:C_QI].reshape(1, bd, t_new, N_KV_HEADS, HEAD_DIM),
        us3[:, :, C_KI:C_WI].reshape(1, bd, t_new, IDX_DIM),
        ext[:, t_new:].reshape(1, bd, POOL_HIST, D_POOL),
    )
```

```python
import functools
import math

import jax
import jax.numpy as jnp
import numpy as np
from jax import lax
from jax.experimental import pallas as pl
from jax.experimental.pallas import tpu as pltpu

F32 = jnp.float32
BF16 = jnp.bfloat16
I32 = jnp.int32

SUBLANES = 8
LANES = 128
VMEM_LIMIT_BYTES = 56 * 1024 * 1024

D_MODEL = 4096
D_POOL = D_MODEL // 2
POOL_WINDOWS = (2, 4, 8, 16)
N_POOL_GROUPS = len(POOL_WINDOWS)
POOL_GROUP = D_POOL // N_POOL_GROUPS
POOL_HIST = max(POOL_WINDOWS) - 1
HEAD_DIM = 128
N_HEADS = (D_MODEL - D_POOL) // HEAD_DIM
N_KV_HEADS = 4
GROUP = N_HEADS // N_KV_HEADS
D_ATTN = N_HEADS * HEAD_DIM
D_KV = N_KV_HEADS * HEAD_DIM
IDX_HEADS = 8
IDX_DIM = 128
TOPK_MAX = 256
NUM_BUCKETS = 32
MAX_DISTANCE = 128
EPS = 1e-6
NEG = -1e30

C_Q = D_POOL
C_K = C_Q + D_ATTN
C_V = C_K + D_KV
C_QI = C_V + D_KV
C_KI = C_QI + IDX_HEADS * IDX_DIM
C_WI = C_KI + IDX_DIM
D_IN = C_WI + IDX_HEADS
D_IN_PAD = ((D_IN + 2 * LANES - 1) // (2 * LANES)) * (2 * LANES)
PROJ_TN = 10 * LANES
assert D_IN_PAD % PROJ_TN == 0

INT_MIN = -(2 ** 31)
KEY_NEG_INF = INT_MIN + 0x007FFFFF


def _compiler_params(semantics):
    return pltpu.CompilerParams(dimension_semantics=semantics, vmem_limit_bytes=VMEM_LIMIT_BYTES)


def _dot(a, b):
    return jnp.dot(a, b, preferred_element_type=F32)


def _dot_nt(a, b):
    return lax.dot_general(a, b, (((1,), (1,)), ((), ())), preferred_element_type=F32)


def _rms_scale(x):
    return lax.rsqrt(jnp.mean(x * x, axis=-1, keepdims=True) + EPS)


def _split_bf16(x):
    hi = x.astype(BF16)
    lo = (x - hi.astype(F32)).astype(BF16)
    return hi, lo


def _ffn_step(j, last, x_ref, g_ref, load_weights, o_ref, h_ref):
    @pl.when(j == 0)
    def _():
        x = x_ref[...]
        h_ref[...] = (x * _rms_scale(x) * g_ref[...]).astype(BF16)
        o_ref[...] = jnp.zeros_like(o_ref)

    wg, wu, wd = load_weights()
    h = h_ref[...]
    a = _dot(h, wg)
    b = _dot(h, wu)
    s = (a * jax.nn.sigmoid(a) * b).astype(BF16)
    o_ref[...] += _dot(s, wd)

    @pl.when(j == last)
    def _():
        o_ref[...] = x_ref[...] + 0.5 * o_ref[...]


def _ffn_kernel(x_ref, g_ref, wg_ref, wu_ref, wd_ref, o_ref, h_ref):
    _ffn_step(pl.program_id(1), pl.num_programs(1) - 1, x_ref, g_ref,
              lambda: (wg_ref[...], wu_ref[...], wd_ref[...]), o_ref, h_ref)


def _ffn_cast_kernel(x_ref, g_ref, wg_ref, wu_ref, wd_ref, o_ref, wg_o, wu_o, wd_o, h_ref):
    def load_weights():
        wg = wg_ref[...].astype(BF16)
        wu = wu_ref[...].astype(BF16)
        wd = wd_ref[...].astype(BF16)
        wg_o[...] = wg
        wu_o[...] = wu
        wd_o[...] = wd
        return wg, wu, wd

    _ffn_step(pl.program_id(0), pl.num_programs(0) - 1, x_ref, g_ref, load_weights, o_ref, h_ref)


def _ffn_cast(x, g, wg, wu, wd, tf):
    m, d = x.shape
    f = wg.shape[1]
    w_in = lambda: pl.BlockSpec((d, tf), lambda j: (0, j))
    w_out = lambda: pl.BlockSpec((tf, d), lambda j: (j, 0))
    return pl.pallas_call(
        _ffn_cast_kernel,
        out_shape=[jax.ShapeDtypeStruct((m, d), F32), jax.ShapeDtypeStruct((d, f), BF16),
                   jax.ShapeDtypeStruct((d, f), BF16), jax.ShapeDtypeStruct((f, d), BF16)],
        grid=(f // tf,),
        in_specs=[pl.BlockSpec((m, d), lambda j: (0, 0)), pl.BlockSpec((1, d), lambda j: (0, 0)),
                  w_in(), w_in(), w_out()],
        out_specs=[pl.BlockSpec((m, d), lambda j: (0, 0)), w_in(), w_in(), w_out()],
        scratch_shapes=[pltpu.VMEM((m, d), BF16)],
        compiler_params=_compiler_params(("arbitrary",)),
        name="swiglu_half_cast",
    )(x, g.reshape(1, d), wg, wu, wd)


def _ffn(x, g, wg, wu, wd, tm, tf):
    m, d = x.shape
    f = wg.shape[1]
    return pl.pallas_call(
        _ffn_kernel,
        out_shape=jax.ShapeDtypeStruct((m, d), F32),
        grid=(m // tm, f // tf),
        in_specs=[
            pl.BlockSpec((tm, d), lambda i, j: (i, 0), pipeline_mode=pl.Buffered(1)),
            pl.BlockSpec((1, d), lambda i, j: (0, 0)),
            pl.BlockSpec((d, tf), lambda i, j: (0, j)),
            pl.BlockSpec((d, tf), lambda i, j: (0, j)),
            pl.BlockSpec((tf, d), lambda i, j: (j, 0)),
        ],
        out_specs=pl.BlockSpec((tm, d), lambda i, j: (i, 0)),
        scratch_shapes=[pltpu.VMEM((tm, d), BF16)],
        compiler_params=_compiler_params(("parallel", "arbitrary")),
        name="swiglu_half",
    )(x, g.reshape(1, d), wg, wu, wd)


def _proj_kernel(x_ref, g_ref, w_ref, gain_ref, o_ref, h_ref):
    j = pl.program_id(1)

    @pl.when(j == 0)
    def _():
        x = x_ref[...]
        h_ref[...] = (x * _rms_scale(x) * g_ref[...]).astype(BF16)

    u = _dot(h_ref[...], w_ref[...])
    gain = gain_ref[...]

    def is_normed(col):
        return C_Q <= col < C_V or C_KI <= col < C_WI

    for jt in range(D_IN_PAD // PROJ_TN):
        @pl.when(j == jt)
        def _(jt=jt):
            c = 0
            while c < PROJ_TN:
                if is_normed(jt * PROJ_TN + c):
                    part = u[:, c:c + LANES]
                    o_ref[:, c:c + LANES] = part * _rms_scale(part) * gain[:, c:c + LANES]
                    c += LANES
                else:
                    end = c
                    while end < PROJ_TN and not is_normed(jt * PROJ_TN + end):
                        end += LANES
                    o_ref[:, c:end] = u[:, c:end]
                    c = end


def _proj(x, g_mix, w_in, gain, tm):
    m, d = x.shape
    n = w_in.shape[1]
    return pl.pallas_call(
        _proj_kernel,
        out_shape=jax.ShapeDtypeStruct((m, n), F32),
        grid=(m // tm, n // PROJ_TN),
        in_specs=[
            pl.BlockSpec((tm, d), lambda i, j: (i, 0), pipeline_mode=pl.Buffered(1)),
            pl.BlockSpec((1, d), lambda i, j: (0, 0)),
            pl.BlockSpec((d, PROJ_TN), lambda i, j: (0, j)),
            pl.BlockSpec((1, PROJ_TN), lambda i, j: (0, j)),
        ],
        out_specs=pl.BlockSpec((tm, PROJ_TN), lambda i, j: (i, j)),
        scratch_shapes=[pltpu.VMEM((tm, d), BF16)],
        compiler_params=_compiler_params(("parallel", "arbitrary")),
        name="in_proj",
    )(x, g_mix.reshape(1, d), w_in, gain)


HALO = 2 * SUBLANES


def _pool_groups(ext_rows, cur, count_of, wp_ref, scale_ref, store):
    for g, w in enumerate(POOL_WINDOWS):
        lo, hi = g * POOL_GROUP, (g + 1) * POOL_GROUP
        acc = ext_rows(0, lo, hi)
        for d in range(1, w):
            acc = acc + ext_rows(d, lo, hi)
        diff = acc / count_of(w) - cur(lo, hi)
        out = _dot(diff.astype(BF16), wp_ref[g]) * scale_ref[:, lo:hi]
        store(lo, hi, out)


def _pool_prompt_kernel(u_ref, halo_ref, wp_ref, scale_ref, o_ref, ext_ref, *, tp):
    i = pl.program_id(1)
    halo = halo_ref[...]
    ext_ref[:HALO, :] = jnp.where(i == 0, jnp.zeros_like(halo), halo)
    ext_ref[HALO:, :] = u_ref[...]
    pos = i * tp + lax.broadcasted_iota(I32, (tp, POOL_GROUP), 0)

    def store(lo, hi, out):
        o_ref[:, lo:hi] = out.astype(o_ref.dtype)

    _pool_groups(
        lambda d, lo, hi: ext_ref[HALO - d:HALO - d + tp, lo:hi],
        lambda lo, hi: u_ref[:, lo:hi],
        lambda w: jnp.minimum(pos + 1, w).astype(F32),
        wp_ref, scale_ref, store)


def _pool_prompt(u_all, wp, scale, batch, seq, tp):
    nt = seq // tp
    kern = functools.partial(_pool_prompt_kernel, tp=tp)
    return pl.pallas_call(
        kern,
        out_shape=jax.ShapeDtypeStruct((batch * seq, D_POOL), BF16),
        grid=(batch, nt),
        in_specs=[
            pl.BlockSpec((tp, D_POOL), lambda b, i: (b * nt + i, 0)),
            pl.BlockSpec((HALO, D_POOL),
                         lambda b, i: (jnp.maximum((b * seq + i * tp) // HALO - 1, 0), 0)),
            pl.BlockSpec((N_POOL_GROUPS, POOL_GROUP, POOL_GROUP), lambda b, i: (0, 0, 0)),
            pl.BlockSpec((1, D_POOL), lambda b, i: (0, 0)),
        ],
        out_specs=pl.BlockSpec((tp, D_POOL), lambda b, i: (b * nt + i, 0)),
        scratch_shapes=[pltpu.VMEM((HALO + tp, D_POOL), F32)],
        compiler_params=_compiler_params(("parallel", "arbitrary")),
        name="pool_prompt",
    )(u_all, u_all, wp, scale)


def _pool_sample_kernel(ext_ref, wp_ref, scale_ref, o_ref, *, t_new):
    for t in range(t_new):
        def store(lo, hi, out, t=t):
            o_ref[t, :, lo:hi] = out.astype(o_ref.dtype)

        _pool_groups(
            lambda d, lo, hi, t=t: ext_ref[POOL_HIST + t - d, :, lo:hi],
            lambda lo, hi, t=t: ext_ref[POOL_HIST + t, :, lo:hi],
            lambda w: float(w),
            wp_ref, scale_ref, store)


def _pool_sample(ext_t, wp, scale, t_new):
    _, bd, _ = ext_t.shape
    kern = functools.partial(_pool_sample_kernel, t_new=t_new)
    return pl.pallas_call(
        kern,
        out_shape=jax.ShapeDtypeStruct((t_new, bd, D_POOL), BF16),
        compiler_params=pltpu.CompilerParams(vmem_limit_bytes=VMEM_LIMIT_BYTES),
        name="pool_sample",
    )(ext_t, wp, scale)


def _bucket_thresholds():
    n = np.arange(0, 4 * MAX_DISTANCE, dtype=np.int64)
    max_exact = NUM_BUCKETS // 2
    nf = np.maximum(n, 1).astype(np.float32)
    large = max_exact + (np.log(nf / np.float32(max_exact)) / np.float32(math.log(MAX_DISTANCE / max_exact))
                         * np.float32(NUM_BUCKETS - max_exact)).astype(np.int32)
    large = np.minimum(large, NUM_BUCKETS - 1)
    bucket = np.where(n < max_exact, n, large)
    assert np.all(np.diff(bucket) >= 0) and bucket[-1] == NUM_BUCKETS - 1
    return [int(np.argmax(bucket >= b)) for b in range(1, NUM_BUCKETS)]


BUCKET_THRESHOLDS = _bucket_thresholds()
FAR_DISTANCE = BUCKET_THRESHOLDS[-1]
BIAS_W = 2 * LANES
assert FAR_DISTANCE <= LANES


def _bias_kernel(rb_ref, o_ref):
    r = lax.broadcasted_iota(I32, (LANES, BIAS_W), 0)
    c = lax.broadcasted_iota(I32, (LANES, BIAS_W), 1)
    d = LANES + r - c
    bucket = jnp.zeros((LANES, BIAS_W), I32)
    for thr in BUCKET_THRESHOLDS:
        bucket = bucket + (d >= thr).astype(I32)
    for h in range(N_HEADS):
        far = rb_ref[NUM_BUCKETS - 1, h]
        val = jnp.zeros((LANES, BIAS_W), F32)
        for b in range(NUM_BUCKETS - 1):
            val = jnp.where(bucket == b, rb_ref[b, h] - far, val)
        o_ref[h] = jnp.where(d >= 0, val, 0.0)


def _bias_tiles(rel_bias):
    return pl.pallas_call(
        _bias_kernel,
        out_shape=jax.ShapeDtypeStruct((N_HEADS, LANES, BIAS_W), F32),
        in_specs=[pl.BlockSpec(memory_space=pltpu.SMEM)],
        name="rel_bias_tiles",
    )(rel_bias)


def _decode_key(key):
    bits = jnp.where(key < 0, key ^ 0x7FFFFFFF, key)
    return lax.bitcast_convert_type(bits, F32)


def _count(mask):
    return jnp.sum(jnp.where(mask, 1.0, 0.0), axis=1, keepdims=True)


def _topk_mask(score, admissible, kpos, need):
    rows, n = score.shape
    score = jnp.where(admissible, score, -jnp.inf)
    need_f = float(need)

    nonneg = _count(score >= 0.0) >= need_f
    lo0 = jnp.where(nonneg, 0, INT_MIN).astype(I32)

    def bit_step(it, lo):
        cand = lo | (jnp.int32(1) << (30 - it))
        ok = _count(score >= _decode_key(cand)) >= need_f
        return jnp.where(ok, cand, lo)

    lo = lax.fori_loop(0, 31, bit_step, lo0)
    thr = jnp.where(lo <= KEY_NEG_INF, -jnp.inf, _decode_key(lo))

    gt = score > thr
    eq = score == thr
    n_gt = _count(gt)
    spare = need_f - n_gt
    nbits = int(n).bit_length()

    def trim_ties():
        def idx_step(it, lim):
            cand = lim + (jnp.int32(1) << (nbits - 1 - it))
            ok = _count(eq & (kpos < cand)) <= spare
            return jnp.where(ok, cand, lim)
        return lax.fori_loop(0, nbits, idx_step, jnp.zeros((rows, 1), I32))

    overshoot = jnp.max(n_gt + _count(eq)) > need_f
    lim = lax.cond(overshoot, trim_ties, lambda: jnp.full((rows, 1), 2 ** nbits, I32))
    return (gt | (eq & (kpos < lim))) & admissible


def _attn_prompt_body(i, ext, first, q_ref, qi_ref, wi_ref, bias_ref, o_ref, k_bf, v_bf, ki_hi, ki_lo, s_ref,
                      tq, topk):
    wi = wi_ref[...] * (IDX_DIM ** -0.5 * IDX_HEADS ** -0.5)
    khi = ki_hi[:ext, :]
    klo = ki_lo[:ext, :]
    score = jnp.zeros((tq, ext), F32)
    for h in range(IDX_HEADS):
        qhi, qlo = _split_bf16(qi_ref[:, h * IDX_DIM:(h + 1) * IDX_DIM])
        s = _dot_nt(qhi, khi) + (_dot_nt(qhi, klo) + _dot_nt(qlo, khi))
        score = score + jnp.maximum(s, 0.0) * wi[:, h:h + 1]

    kpos = lax.broadcasted_iota(I32, (tq, ext), 1)
    qpos = i * tq + lax.broadcasted_iota(I32, (tq, ext), 0)
    keep = _topk_mask(score, kpos <= qpos, kpos, topk)

    for n in range(N_KV_HEADS):
        kn = k_bf[:ext, n * HEAD_DIM:(n + 1) * HEAD_DIM]
        vn = v_bf[:ext, n * HEAD_DIM:(n + 1) * HEAD_DIM]
        for g in range(GROUP):
            h = n * GROUP + g
            qh = (q_ref[:, h * HEAD_DIM:(h + 1) * HEAD_DIM] * HEAD_DIM ** -0.5).astype(BF16)
            s_ref[:, :ext] = _dot_nt(qh, kn)
            def add_near_bias(h=h):
                ws = pl.multiple_of((i - 1) * tq, LANES)
                s_ref[:, pl.ds(ws, BIAS_W)] += bias_ref[h]

            if first:
                @pl.when(i == 0)
                def _():
                    s_ref[:, :tq] += bias_ref[h, :, LANES:]

                pl.when(i > 0)(add_near_bias)
            else:
                add_near_bias()

            s = jnp.where(keep, s_ref[:, :ext], NEG)
            p = jnp.exp(s - jnp.max(s, axis=1, keepdims=True))
            l = jnp.sum(p, axis=1, keepdims=True)
            o = _dot(p.astype(BF16), vn) / l
            o_ref[:, h * HEAD_DIM:(h + 1) * HEAD_DIM] = o.astype(o_ref.dtype)


def _attn_prompt_kernel(q_ref, k_ref, v_ref, qi_ref, ki_ref, wi_ref, bias_ref, o_ref,
                        k_bf, v_bf, ki_hi, ki_lo, s_ref, *, tq, topk, ext, tile0):
    j = pl.program_id(1)

    @pl.when(j == 0)
    def _():
        k_bf[...] = k_ref[:ext, :].astype(BF16)
        v_bf[...] = v_ref[:ext, :].astype(BF16)
        hi, lo = _split_bf16(ki_ref[:ext, :])
        ki_hi[...] = hi
        ki_lo[...] = lo

    _attn_prompt_body(tile0 + j, ext, tile0 == 0, q_ref, qi_ref, wi_ref, bias_ref, o_ref, k_bf, v_bf, ki_hi, ki_lo,
                      s_ref, tq, topk)


def _attn_prompt_span(u_all, bias, batch, seq, tile0, ntile, tq, topk):
    nq = seq // tq
    ext = (tile0 + ntile) * tq
    kern = functools.partial(_attn_prompt_kernel, tq=tq, topk=topk, ext=ext, tile0=tile0)
    row = lambda b, j: b * nq + tile0 + j
    return pl.pallas_call(
        kern,
        out_shape=jax.ShapeDtypeStruct((batch * ntile * tq, D_ATTN), BF16),
        grid=(batch, ntile),
        in_specs=[
            pl.BlockSpec((tq, D_ATTN), lambda b, j: (row(b, j), C_Q // D_ATTN)),
            pl.BlockSpec((seq, D_KV), lambda b, j: (b, C_K // D_KV)),
            pl.BlockSpec((seq, D_KV), lambda b, j: (b, C_V // D_KV)),
            pl.BlockSpec((tq, IDX_HEADS * IDX_DIM), lambda b, j: (row(b, j), C_QI // (IDX_HEADS * IDX_DIM))),
            pl.BlockSpec((seq, IDX_DIM), lambda b, j: (b, C_KI // IDX_DIM)),
            pl.BlockSpec((tq, LANES), lambda b, j: (row(b, j), C_WI // LANES)),
            pl.BlockSpec((N_HEADS, LANES, BIAS_W), lambda b, j: (0, 0, 0)),
        ],
        out_specs=pl.BlockSpec((tq, D_ATTN), lambda b, j: (b * ntile + j, 0)),
        scratch_shapes=[
            pltpu.VMEM((ext, D_KV), BF16),
            pltpu.VMEM((ext, D_KV), BF16),
            pltpu.VMEM((ext, IDX_DIM), BF16),
            pltpu.VMEM((ext, IDX_DIM), BF16),
            pltpu.VMEM((tq, ext), F32),
        ],
        compiler_params=_compiler_params(("parallel", "arbitrary")),
        name=f"attn_prompt_{ext}",
    )(u_all, u_all, u_all, u_all, u_all, u_all, bias)


ATTN_SPANS = 4


def _attn_prompt(u_all, bias, batch, seq):
    tq = LANES
    nq = seq // tq
    topk = min(TOPK_MAX, seq // 4)
    ntile = nq // ATTN_SPANS
    assert ntile * ATTN_SPANS == nq and ntile >= 2
    spans = [_attn_prompt_span(u_all, bias, batch, seq, e * ntile, ntile, tq, topk).reshape(batch, ntile * tq, D_ATTN)
             for e in range(ATTN_SPANS)]
    return jnp.concatenate(spans, axis=1).reshape(batch * seq, D_ATTN)


def _page_copies(pt_ref, n_pages, rows, srcs_dsts_sems):
    def copy(src, dst, sem, p, pg):
        return pltpu.make_async_copy(src.at[pg], dst.at[pl.ds(pl.multiple_of(p * rows, rows), rows)], sem)

    def start(b, p0):
        def body(p, carry):
            pg = pt_ref[b, p0 + p]
            for src, dst, sem in srcs_dsts_sems:
                copy(src, dst, sem, p, pg).start()
            return carry
        lax.fori_loop(0, n_pages, body, 0)

    def wait():
        def body(p, carry):
            for src, dst, sem in srcs_dsts_sems:
                copy(src, dst, sem, p, 0).wait()
            return carry
        lax.fori_loop(0, n_pages, body, 0)

    return start, wait


def _prefetched(step, n_steps, start_into, wait_for):
    @pl.when(step == 0)
    def _():
        start_into(0, step)

    nxt = step + 1
    for slot in range(2):
        @pl.when((nxt < n_steps) & ((nxt & 1) == slot))
        def _(slot=slot):
            start_into(slot, nxt)

    for slot in range(2):
        @pl.when((step & 1) == slot)
        def _(slot=slot):
            wait_for(slot)


def _sample_scores_kernel(pt_ref, cache_ref, qi_ref, w_ref, kinew_ref, o_ref, ki_buf, sem,
                          *, n_pages, page, past, chunk):
    b = pl.program_id(0)
    total = ki_buf.shape[1]

    def copies(slot):
        return _page_copies(pt_ref, n_pages, page, [(cache_ref, ki_buf.at[slot], sem.at[slot])])

    _prefetched(b, pl.num_programs(0), lambda slot, step: copies(slot)[0](step, 0), lambda slot: copies(slot)[1]())
    ki_all = ki_buf.at[b & 1]
    ki_all[past:past + SUBLANES, :] = kinew_ref[0]
    ki_all[past + SUBLANES:, :] = jnp.zeros((total - past - SUBLANES, IDX_DIM), F32)

    qhi, qlo = _split_bf16(qi_ref[0])
    w = w_ref[0] * (IDX_DIM ** -0.5 * IDX_HEADS ** -0.5)
    rows = qi_ref.shape[1]
    for c0 in range(0, total, chunk):
        khi, klo = _split_bf16(ki_all[c0:c0 + chunk, :])
        s = _dot_nt(qhi, khi) + (_dot_nt(qhi, klo) + _dot_nt(qlo, khi))
        s = jnp.maximum(s, 0.0) * w
        o_ref[0, :, c0:c0 + chunk] = jnp.sum(s.reshape(rows // IDX_HEADS, IDX_HEADS, chunk), axis=1)


def _sample_scores(page_table, cache_idx, qi_rep, w_col, ki_new, past):
    bd, n_pages = page_table.shape
    page = cache_idx.shape[1]
    total = past + LANES
    rows = qi_rep.shape[1]
    nrow = rows // IDX_HEADS
    n_chunks = 5
    chunk = total // n_chunks
    assert chunk * n_chunks == total and chunk % LANES == 0
    kern = functools.partial(_sample_scores_kernel, n_pages=n_pages, page=page, past=past, chunk=chunk)
    return pl.pallas_call(
        kern,
        out_shape=jax.ShapeDtypeStruct((bd, nrow, total), F32),
        grid_spec=pltpu.PrefetchScalarGridSpec(
            num_scalar_prefetch=1,
            grid=(bd,),
            in_specs=[
                pl.BlockSpec(memory_space=pl.ANY),
                pl.BlockSpec((1, rows, IDX_DIM), lambda b, pt: (b, 0, 0)),
                pl.BlockSpec((1, rows, 1), lambda b, pt: (b, 0, 0)),
                pl.BlockSpec((1, SUBLANES, IDX_DIM), lambda b, pt: (b, 0, 0)),
            ],
            out_specs=pl.BlockSpec((1, nrow, total), lambda b, pt: (b, 0, 0)),
            scratch_shapes=[
                pltpu.VMEM((2, total, IDX_DIM), F32),
                pltpu.SemaphoreType.DMA((2,)),
            ],
        ),
        compiler_params=_compiler_params(("arbitrary",)),
        name="sample_scores",
    )(page_table, cache_idx, qi_rep, w_col, ki_new)


def _sample_topk_kernel(sc_ref, o_ref, *, past, topk, rows_per_batch, group_rows):
    rows, total = sc_ref.shape
    kpos = lax.broadcasted_iota(I32, (rows, total), 1)
    row = lax.broadcasted_iota(I32, (rows, total), 0)
    qpos = past + ((row & (rows_per_batch - 1)) >> (group_rows.bit_length() - 1))
    keep = _topk_mask(sc_ref[...], kpos <= qpos, kpos, topk)
    o_ref[...] = jnp.where(keep, 1.0, 0.0)


def _sample_topk(scores, past, topk, rows_per_batch):
    n, total = scores.shape
    tr = LANES
    assert n % tr == 0 and tr % rows_per_batch == 0 and rows_per_batch & (rows_per_batch - 1) == 0
    kern = functools.partial(_sample_topk_kernel, past=past, topk=topk, rows_per_batch=rows_per_batch,
                             group_rows=GROUP)
    return pl.pallas_call(
        kern,
        out_shape=jax.ShapeDtypeStruct((n, total), F32),
        grid=(n // tr,),
        in_specs=[pl.BlockSpec((tr, total), lambda i: (i, 0))],
        out_specs=pl.BlockSpec((tr, total), lambda i: (i, 0)),
        compiler_params=_compiler_params(("parallel",)),
        name="sample_topk",
    )(scores)


def _sample_attend_kernel(pt_ref, ck_ref, cv_ref, q_ref, knew_ref, vnew_ref, keep_ref, tail_ref,
                          o_ref, k_buf, v_buf, m_ref, l_ref, acc_ref, sem, *, layer, half_pages, page):
    s = pl.program_id(0)
    b = s >> 1
    half_keys = half_pages * page
    new0, new1 = half_keys * N_KV_HEADS, (half_keys + SUBLANES) * N_KV_HEADS

    def copies(slot):
        return _page_copies(pt_ref, half_pages, page * N_KV_HEADS,
                            [(ck_ref.at[layer], k_buf.at[slot], sem.at[0, slot]),
                             (cv_ref.at[layer], v_buf.at[slot], sem.at[1, slot])])

    _prefetched(s, pl.num_programs(0),
                lambda slot, step: copies(slot)[0](step >> 1, (step & 1) * half_pages),
                lambda slot: copies(slot)[1]())

    def partial(half, n):
        nkeys = half_keys + (LANES if half == 1 else 0)
        kn = k_buf[half, pl.ds(n, nkeys, stride=N_KV_HEADS), :].astype(BF16)
        vn = v_buf[half, pl.ds(n, nkeys, stride=N_KV_HEADS), :].astype(BF16)
        sc = _dot_nt((q_ref[0, n] * HEAD_DIM ** -0.5).astype(BF16), kn)
        if half == 1:
            near = nkeys - BIAS_W
            sc = jnp.concatenate([sc[:, :near], sc[:, near:] + tail_ref[n]], axis=1)
        keep = keep_ref[0, :, half * half_keys:half * half_keys + nkeys] > 0.5
        sc = jnp.where(keep, sc, NEG)
        m = jnp.max(sc, axis=1, keepdims=True)
        p = jnp.exp(sc - m)
        return m, jnp.sum(p, axis=1, keepdims=True), _dot(p.astype(BF16), vn)

    @pl.when((s & 1) == 0)
    def _():
        for n in range(N_KV_HEADS):
            m, l, acc = partial(0, n)
            m_ref[n] = m
            l_ref[n] = l
            acc_ref[n] = acc

    @pl.when((s & 1) == 1)
    def _():
        pad = jnp.zeros(((half_keys + LANES) * N_KV_HEADS - new1, HEAD_DIM), F32)
        k_buf[1, new0:new1, :] = knew_ref[0]
        v_buf[1, new0:new1, :] = vnew_ref[0]
        k_buf[1, new1:, :] = pad
        v_buf[1, new1:, :] = pad
        for n in range(N_KV_HEADS):
            m1, l1, acc1 = partial(1, n)
            m0 = m_ref[n]
            m = jnp.maximum(m0, m1)
            a0 = jnp.exp(m0 - m)
            a1 = jnp.exp(m1 - m)
            o = (acc_ref[n] * a0 + acc1 * a1) / (l_ref[n] * a0 + l1 * a1)
            o_ref[0, n] = o.astype(o_ref.dtype)


def _sample_attend(page_table, cache_k, cache_v, layer, page, q_rows, k_new, v_new, keep, tail, past):
    bd, n_pages = page_table.shape
    total = past + LANES
    nrow = q_rows.shape[2]
    half_pages = n_pages // 2
    assert half_pages * 2 == n_pages and BIAS_W <= LANES + half_pages * page
    buf_rows = (half_pages * page + LANES) * N_KV_HEADS
    kern = functools.partial(_sample_attend_kernel, layer=layer, half_pages=half_pages, page=page)
    per_batch = lambda rank: (lambda s, pt: (s // 2,) + (0,) * (rank - 1))
    return pl.pallas_call(
        kern,
        out_shape=jax.ShapeDtypeStruct((bd, N_KV_HEADS, nrow, HEAD_DIM), BF16),
        grid_spec=pltpu.PrefetchScalarGridSpec(
            num_scalar_prefetch=1,
            grid=(2 * bd,),
            in_specs=[
                pl.BlockSpec(memory_space=pl.ANY),
                pl.BlockSpec(memory_space=pl.ANY),
                pl.BlockSpec((1, N_KV_HEADS, nrow, HEAD_DIM), per_batch(4)),
                pl.BlockSpec((1, SUBLANES * N_KV_HEADS, HEAD_DIM), per_batch(3)),
                pl.BlockSpec((1, SUBLANES * N_KV_HEADS, HEAD_DIM), per_batch(3)),
                pl.BlockSpec((1, nrow, total), per_batch(3)),
                pl.BlockSpec((N_KV_HEADS, nrow, BIAS_W), lambda s, pt: (0, 0, 0)),
            ],
            out_specs=pl.BlockSpec((1, N_KV_HEADS, nrow, HEAD_DIM), per_batch(4)),
            scratch_shapes=[
                pltpu.VMEM((2, buf_rows, HEAD_DIM), F32),
                pltpu.VMEM((2, buf_rows, HEAD_DIM), F32),
                pltpu.VMEM((N_KV_HEADS, nrow, 1), F32),
                pltpu.VMEM((N_KV_HEADS, nrow, 1), F32),
                pltpu.VMEM((N_KV_HEADS, nrow, HEAD_DIM), F32),
                pltpu.SemaphoreType.DMA((2, 2)),
            ],
        ),
        compiler_params=_compiler_params(("arbitrary",)),
        name="sample_attend",
    )(page_table, cache_k, cache_v, q_rows, k_new, v_new, keep, tail)


def _out_proj_kernel(x_ref, pool_ref, attn_ref, wp_ref, wa_ref, o_ref):
    o_ref[...] = x_ref[...] + (_dot(pool_ref[...], wp_ref[...]) + _dot(attn_ref[...], wa_ref[...]))


def _out_proj(x, pool, attn, w_out, tm, tn):
    m, d = x.shape
    return pl.pallas_call(
        _out_proj_kernel,
        out_shape=jax.ShapeDtypeStruct((m, d), F32),
        grid=(m // tm, d // tn),
        in_specs=[
            pl.BlockSpec((tm, tn), lambda i, j: (i, j)),
            pl.BlockSpec((tm, D_POOL), lambda i, j: (i, 0)),
            pl.BlockSpec((tm, D_ATTN), lambda i, j: (i, 0)),
            pl.BlockSpec((D_POOL, tn), lambda i, j: (0, j)),
            pl.BlockSpec((D_ATTN, tn), lambda i, j: (D_POOL // D_ATTN, j)),
        ],
        out_specs=pl.BlockSpec((tm, tn), lambda i, j: (i, j)),
        compiler_params=_compiler_params(("parallel", "arbitrary")),
        name="out_proj",
    )(x, pool, attn, w_out, w_out)


def _proj_gain(g_q, g_k, g_idx_k):
    return jnp.concatenate([
        jnp.ones((D_POOL,), F32), jnp.tile(g_q, N_HEADS), jnp.tile(g_k, N_KV_HEADS),
        jnp.ones((C_KI - C_V,), F32), g_idx_k, jnp.ones((D_IN_PAD - C_WI,), F32)]).reshape(1, D_IN_PAD)


def _pad_rows(a, rows):
    return jnp.pad(a, ((0, 0), (0, rows - a.shape[1]), (0, 0)))


def kernel(x_prompt, x_sample, cache_k, cache_v, cache_idx_k, state_pool, page_table, g_ffn1, w_gate1, w_up1,
           w_down1, g_mix, w_in, g_q, g_k, g_idx_k, w_pool, pool_scale, w_out, rel_bias, g_ffn2, w_gate2,
           w_up2, w_down2):
    batch, seq, d = x_prompt.shape
    bd, t_new, _ = x_sample.shape
    depth = g_ffn1.shape[0]
    assert depth == 1 and d == D_MODEL
    past = page_table.shape[1] * cache_k.shape[2]
    mp, ms = batch * seq, bd * t_new
    tm_p, tm_s = 512, ms
    tf = 256

    xp = x_prompt.reshape(mp, d)
    xs = x_sample.reshape(ms, d)
    l = 0

    xs, wg1, wu1, wd1 = _ffn_cast(xs, g_ffn1[l], w_gate1[l], w_up1[l], w_down1[l], tf)
    xp = _ffn(xp, g_ffn1[l], wg1, wu1, wd1, tm_p, tf)

    w_in_p = jnp.pad(w_in[l].astype(BF16), ((0, 0), (0, D_IN_PAD - D_IN)))
    gain = _proj_gain(g_q[l], g_k[l], g_idx_k[l])
    up = _proj(xp, g_mix[l], w_in_p, gain, tm_p)
    us = _proj(xs, g_mix[l], w_in_p, gain, tm_s)

    wp = w_pool[l].astype(BF16)
    scale = pool_scale[l].reshape(1, D_POOL)
    bias = _bias_tiles(rel_bias)

    pool_p = _pool_prompt(up, wp, scale, batch, seq, 256)
    attn_p = _attn_prompt(up, bias, batch, seq)

    hist = state_pool[l]
    u_pool_s = us[:, :D_POOL].reshape(bd, t_new, D_POOL)
    ext = jnp.concatenate([hist, u_pool_s], axis=1)
    pool_s = _pool_sample(ext.transpose(1, 0, 2), wp, scale, t_new)
    pool_s = pool_s.transpose(1, 0, 2).reshape(ms, D_POOL)

    us3 = us.reshape(bd, t_new, D_IN_PAD)
    rows_i = t_new * GROUP * IDX_HEADS
    qi_s = us3[:, :, C_QI:C_KI].reshape(bd, t_new, 1, IDX_HEADS, IDX_DIM)
    qi_rep = jnp.broadcast_to(qi_s, (bd, t_new, GROUP, IDX_HEADS, IDX_DIM)).reshape(bd, rows_i, IDX_DIM)
    wi_s = us3[:, :, C_WI:C_WI + IDX_HEADS].reshape(bd, t_new, 1, IDX_HEADS)
    w_col = jnp.broadcast_to(wi_s, (bd, t_new, GROUP, IDX_HEADS)).reshape(bd, rows_i, 1)
    ki_new = _pad_rows(us3[:, :, C_KI:C_WI], SUBLANES)
    k_new = _pad_rows(us3[:, :, C_K:C_V], SUBLANES)
    v_new = _pad_rows(us3[:, :, C_V:C_QI], SUBLANES)
    n_phys, page = cache_k.shape[1], cache_k.shape[2]
    topk_s = min(TOPK_MAX, (past + t_new) // 4)
    nrow = t_new * GROUP
    scores_s = _sample_scores(page_table, cache_idx_k[l], qi_rep, w_col, ki_new, past)
    keep_s = _sample_topk(scores_s.reshape(bd * nrow, past + LANES), past, topk_s, nrow)
    keep_s = keep_s.reshape(bd, nrow, past + LANES)

    q_rows = us3[:, :, C_Q:C_K].reshape(bd, t_new, N_KV_HEADS, GROUP, HEAD_DIM)
    q_rows = q_rows.transpose(0, 2, 1, 3, 4).reshape(bd, N_KV_HEADS, nrow, HEAD_DIM)
    tail = bias[:, :t_new, :].reshape(N_KV_HEADS, GROUP, t_new, BIAS_W)
    tail = tail.transpose(0, 2, 1, 3).reshape(N_KV_HEADS, nrow, BIAS_W)
    rows_shape = (depth, n_phys, page * N_KV_HEADS, HEAD_DIM)
    new_shape = (bd, SUBLANES * N_KV_HEADS, HEAD_DIM)
    attn_s = _sample_attend(page_table, cache_k.reshape(rows_shape), cache_v.reshape(rows_shape), l, page, q_rows,
                            k_new.reshape(new_shape), v_new.reshape(new_shape), keep_s, tail, past)
    attn_s = attn_s.reshape(bd, N_KV_HEADS, t_new, GROUP, HEAD_DIM).transpose(0, 2, 1, 3, 4).reshape(ms, D_ATTN)

    w_out_bf = w_out[l].astype(BF16)
    xp = _out_proj(xp, pool_p, attn_p, w_out_bf, tm_p, 512)
    xs = _out_proj(xs, pool_s, attn_s, w_out_bf, tm_s, 512)

    xs, wg2, wu2, wd2 = _ffn_cast(xs, g_ffn2[l], w_gate2[l], w_up2[l], w_down2[l], tf)
    xp = _ffn(xp, g_ffn2[l], wg2, wu2, wd2, tm_p, tf)

    up4 = up.reshape(batch, seq, D_IN_PAD)
    return (
        xp.reshape(batch, seq, d),
        xs.reshape(bd, t_new, d),
        up4[:, :, C_K:C_V].reshape(1, batch, seq, N_KV_HEADS, HEAD_DIM),
        up4[:, :, C_V:C_QI].reshape(1, batch, seq, N_KV_HEADS, HEAD_DIM),
        up4[:, :, C_KI:C_WI].reshape(1, batch, seq, IDX_DIM),
        up4[:, seq - POOL_HIST:, :D_POOL].reshape(1, batch, POOL_HIST, D_POOL),
        us3[:, :, C_K:C_V].reshape(1, bd, t_new, N_KV_HEADS, HEAD_DIM),
        us3[:, :, C_V:C_QI].reshape(1, bd, t_new, N_KV_HEADS, HEAD_DIM),
        us3[:, :, C_KI:C_WI].reshape(1, bd, t_new, IDX_DIM),
        ext[:, t_new:].reshape(1, bd, POOL_HIST, D_POOL),
    )
```

```python
import functools
import math

import jax
import jax.numpy as jnp
import numpy as np
from jax import lax
from jax.experimental import pallas as pl
from jax.experimental.pallas import tpu as pltpu

F32 = jnp.float32
BF16 = jnp.bfloat16
I32 = jnp.int32

SUBLANES = 8
LANES = 128
VMEM_LIMIT_BYTES = 56 * 1024 * 1024

D_MODEL = 4096
D_POOL = D_MODEL // 2
POOL_WINDOWS = (2, 4, 8, 16)
N_POOL_GROUPS = len(POOL_WINDOWS)
POOL_GROUP = D_POOL // N_POOL_GROUPS
POOL_HIST = max(POOL_WINDOWS) - 1
HEAD_DIM = 128
N_HEADS = (D_MODEL - D_POOL) // HEAD_DIM
N_KV_HEADS = 4
GROUP = N_HEADS // N_KV_HEADS
D_ATTN = N_HEADS * HEAD_DIM
D_KV = N_KV_HEADS * HEAD_DIM
IDX_HEADS = 8
IDX_DIM = 128
TOPK_MAX = 256
NUM_BUCKETS = 32
MAX_DISTANCE = 128
EPS = 1e-6
NEG = -1e30
LOG2E = math.log2(math.e)

C_Q = D_POOL
C_K = C_Q + D_ATTN
C_V = C_K + D_KV
C_QI = C_V + D_KV
C_KI = C_QI + IDX_HEADS * IDX_DIM
C_WI = C_KI + IDX_DIM
D_IN = C_WI + IDX_HEADS
D_IN_PAD = ((D_IN + 2 * LANES - 1) // (2 * LANES)) * (2 * LANES)
PROJ_TN = 10 * LANES
assert D_IN_PAD % PROJ_TN == 0

INT_MIN = -(2 ** 31)
KEY_NEG_INF = INT_MIN + 0x007FFFFF


def _compiler_params(semantics):
    return pltpu.CompilerParams(dimension_semantics=semantics, vmem_limit_bytes=VMEM_LIMIT_BYTES)


def _dot(a, b):
    return jnp.dot(a, b, preferred_element_type=F32)


def _dot_nt(a, b):
    return lax.dot_general(a, b, (((1,), (1,)), ((), ())), preferred_element_type=F32)


def _rms_scale(x):
    return lax.rsqrt(jnp.mean(x * x, axis=-1, keepdims=True) + EPS)


def _split_bf16(x):
    hi = x.astype(BF16)
    lo = (x - hi.astype(F32)).astype(BF16)
    return hi, lo


def _ffn_step(j, last, x_ref, g_ref, load_weights, o_ref, h_ref):
    @pl.when(j == 0)
    def _():
        x = x_ref[...]
        h_ref[...] = (x * _rms_scale(x) * g_ref[...]).astype(BF16)
        o_ref[...] = jnp.zeros_like(o_ref)

    wg, wu, wd = load_weights()
    h = h_ref[...]
    a = _dot(h, wg)
    b = _dot(h, wu)
    s = (a * jax.nn.sigmoid(a) * b).astype(BF16)
    o_ref[...] += _dot(s, wd)

    @pl.when(j == last)
    def _():
        o_ref[...] = x_ref[...] + 0.5 * o_ref[...]


def _ffn_kernel(x_ref, g_ref, wg_ref, wu_ref, wd_ref, o_ref, h_ref):
    _ffn_step(pl.program_id(1), pl.num_programs(1) - 1, x_ref, g_ref,
              lambda: (wg_ref[...], wu_ref[...], wd_ref[...]), o_ref, h_ref)


def _ffn_cast_kernel(x_ref, g_ref, wg_ref, wu_ref, wd_ref, o_ref, wg_o, wu_o, wd_o, h_ref):
    def load_weights():
        wg = wg_ref[...].astype(BF16)
        wu = wu_ref[...].astype(BF16)
        wd = wd_ref[...].astype(BF16)
        wg_o[...] = wg
        wu_o[...] = wu
        wd_o[...] = wd
        return wg, wu, wd

    _ffn_step(pl.program_id(0), pl.num_programs(0) - 1, x_ref, g_ref, load_weights, o_ref, h_ref)


def _ffn_cast(x, g, wg, wu, wd, tf):
    m, d = x.shape
    f = wg.shape[1]
    w_in = lambda: pl.BlockSpec((d, tf), lambda j: (0, j))
    w_out = lambda: pl.BlockSpec((tf, d), lambda j: (j, 0))
    return pl.pallas_call(
        _ffn_cast_kernel,
        out_shape=[jax.ShapeDtypeStruct((m, d), F32), jax.ShapeDtypeStruct((d, f), BF16),
                   jax.ShapeDtypeStruct((d, f), BF16), jax.ShapeDtypeStruct((f, d), BF16)],
        grid=(f // tf,),
        in_specs=[pl.BlockSpec((m, d), lambda j: (0, 0)), pl.BlockSpec((1, d), lambda j: (0, 0)),
                  w_in(), w_in(), w_out()],
        out_specs=[pl.BlockSpec((m, d), lambda j: (0, 0)), w_in(), w_in(), w_out()],
        scratch_shapes=[pltpu.VMEM((m, d), BF16)],
        compiler_params=_compiler_params(("arbitrary",)),
        name="swiglu_half_cast",
    )(x, g.reshape(1, d), wg, wu, wd)


def _ffn(x, g, wg, wu, wd, tm, tf):
    m, d = x.shape
    f = wg.shape[1]
    return pl.pallas_call(
        _ffn_kernel,
        out_shape=jax.ShapeDtypeStruct((m, d), F32),
        grid=(m // tm, f // tf),
        in_specs=[
            pl.BlockSpec((tm, d), lambda i, j: (i, 0), pipeline_mode=pl.Buffered(1)),
            pl.BlockSpec((1, d), lambda i, j: (0, 0)),
            pl.BlockSpec((d, tf), lambda i, j: (0, j)),
            pl.BlockSpec((d, tf), lambda i, j: (0, j)),
            pl.BlockSpec((tf, d), lambda i, j: (j, 0)),
        ],
        out_specs=pl.BlockSpec((tm, d), lambda i, j: (i, 0)),
        scratch_shapes=[pltpu.VMEM((tm, d), BF16)],
        compiler_params=_compiler_params(("parallel", "arbitrary")),
        name="swiglu_half",
    )(x, g.reshape(1, d), wg, wu, wd)


def _proj_kernel(x_ref, g_ref, w_ref, gain_ref, o_ref, h_ref):
    j = pl.program_id(1)

    @pl.when(j == 0)
    def _():
        x = x_ref[...]
        h_ref[...] = (x * _rms_scale(x) * g_ref[...]).astype(BF16)

    u = _dot(h_ref[...], w_ref[...])
    gain = gain_ref[...]

    def is_normed(col):
        return C_Q <= col < C_V or C_KI <= col < C_WI

    for jt in range(D_IN_PAD // PROJ_TN):
        @pl.when(j == jt)
        def _(jt=jt):
            c = 0
            while c < PROJ_TN:
                if is_normed(jt * PROJ_TN + c):
                    part = u[:, c:c + LANES]
                    o_ref[:, c:c + LANES] = part * _rms_scale(part) * gain[:, c:c + LANES]
                    c += LANES
                else:
                    end = c
                    while end < PROJ_TN and not is_normed(jt * PROJ_TN + end):
                        end += LANES
                    o_ref[:, c:end] = u[:, c:end]
                    c = end


def _proj(x, g_mix, w_in, gain, tm):
    m, d = x.shape
    n = w_in.shape[1]
    return pl.pallas_call(
        _proj_kernel,
        out_shape=jax.ShapeDtypeStruct((m, n), F32),
        grid=(m // tm, n // PROJ_TN),
        in_specs=[
            pl.BlockSpec((tm, d), lambda i, j: (i, 0), pipeline_mode=pl.Buffered(1)),
            pl.BlockSpec((1, d), lambda i, j: (0, 0)),
            pl.BlockSpec((d, PROJ_TN), lambda i, j: (0, j)),
            pl.BlockSpec((1, PROJ_TN), lambda i, j: (0, j)),
        ],
        out_specs=pl.BlockSpec((tm, PROJ_TN), lambda i, j: (i, j)),
        scratch_shapes=[pltpu.VMEM((tm, d), BF16)],
        compiler_params=_compiler_params(("parallel", "arbitrary")),
        name="in_proj",
    )(x, g_mix.reshape(1, d), w_in, gain)


HALO = 2 * SUBLANES


def _pool_groups(ext_rows, cur, count_of, wp_ref, scale_ref, store):
    for g, w in enumerate(POOL_WINDOWS):
        lo, hi = g * POOL_GROUP, (g + 1) * POOL_GROUP
        acc = ext_rows(0, lo, hi)
        for d in range(1, w):
            acc = acc + ext_rows(d, lo, hi)
        diff = acc / count_of(w) - cur(lo, hi)
        out = _dot(diff.astype(BF16), wp_ref[g]) * scale_ref[:, lo:hi]
        store(lo, hi, out)


def _pool_prompt_kernel(u_ref, halo_ref, wp_ref, scale_ref, o_ref, ext_ref, *, tp):
    i = pl.program_id(1)
    halo = halo_ref[...]
    ext_ref[:HALO, :] = jnp.where(i == 0, jnp.zeros_like(halo), halo)
    ext_ref[HALO:, :] = u_ref[...]
    pos = i * tp + lax.broadcasted_iota(I32, (tp, POOL_GROUP), 0)

    def store(lo, hi, out):
        o_ref[:, lo:hi] = out.astype(o_ref.dtype)

    _pool_groups(
        lambda d, lo, hi: ext_ref[HALO - d:HALO - d + tp, lo:hi],
        lambda lo, hi: u_ref[:, lo:hi],
        lambda w: jnp.minimum(pos + 1, w).astype(F32),
        wp_ref, scale_ref, store)


def _pool_prompt(u_all, wp, scale, batch, seq, tp):
    nt = seq // tp
    kern = functools.partial(_pool_prompt_kernel, tp=tp)
    return pl.pallas_call(
        kern,
        out_shape=jax.ShapeDtypeStruct((batch * seq, D_POOL), BF16),
        grid=(batch, nt),
        in_specs=[
            pl.BlockSpec((tp, D_POOL), lambda b, i: (b * nt + i, 0)),
            pl.BlockSpec((HALO, D_POOL),
                         lambda b, i: (jnp.maximum((b * seq + i * tp) // HALO - 1, 0), 0)),
            pl.BlockSpec((N_POOL_GROUPS, POOL_GROUP, POOL_GROUP), lambda b, i: (0, 0, 0)),
            pl.BlockSpec((1, D_POOL), lambda b, i: (0, 0)),
        ],
        out_specs=pl.BlockSpec((tp, D_POOL), lambda b, i: (b * nt + i, 0)),
        scratch_shapes=[pltpu.VMEM((HALO + tp, D_POOL), F32)],
        compiler_params=_compiler_params(("parallel", "arbitrary")),
        name="pool_prompt",
    )(u_all, u_all, wp, scale)


def _pool_sample_kernel(ext_ref, wp_ref, scale_ref, o_ref, *, t_new):
    for t in range(t_new):
        def store(lo, hi, out, t=t):
            o_ref[t, :, lo:hi] = out.astype(o_ref.dtype)

        _pool_groups(
            lambda d, lo, hi, t=t: ext_ref[POOL_HIST + t - d, :, lo:hi],
            lambda lo, hi, t=t: ext_ref[POOL_HIST + t, :, lo:hi],
            lambda w: float(w),
            wp_ref, scale_ref, store)


def _pool_sample(ext_t, wp, scale, t_new):
    _, bd, _ = ext_t.shape
    kern = functools.partial(_pool_sample_kernel, t_new=t_new)
    return pl.pallas_call(
        kern,
        out_shape=jax.ShapeDtypeStruct((t_new, bd, D_POOL), BF16),
        compiler_params=pltpu.CompilerParams(vmem_limit_bytes=VMEM_LIMIT_BYTES),
        name="pool_sample",
    )(ext_t, wp, scale)


def _bucket_thresholds():
    n = np.arange(0, 4 * MAX_DISTANCE, dtype=np.int64)
    max_exact = NUM_BUCKETS // 2
    nf = np.maximum(n, 1).astype(np.float32)
    large = max_exact + (np.log(nf / np.float32(max_exact)) / np.float32(math.log(MAX_DISTANCE / max_exact))
                         * np.float32(NUM_BUCKETS - max_exact)).astype(np.int32)
    large = np.minimum(large, NUM_BUCKETS - 1)
    bucket = np.where(n < max_exact, n, large)
    assert np.all(np.diff(bucket) >= 0) and bucket[-1] == NUM_BUCKETS - 1
    return [int(np.argmax(bucket >= b)) for b in range(1, NUM_BUCKETS)]


BUCKET_THRESHOLDS = _bucket_thresholds()
FAR_DISTANCE = BUCKET_THRESHOLDS[-1]
BIAS_W = 2 * LANES
assert FAR_DISTANCE <= LANES


def _bias_kernel(rb_ref, o_ref):
    r = lax.broadcasted_iota(I32, (LANES, BIAS_W), 0)
    c = lax.broadcasted_iota(I32, (LANES, BIAS_W), 1)
    d = LANES + r - c
    bucket = jnp.zeros((LANES, BIAS_W), I32)
    for thr in BUCKET_THRESHOLDS:
        bucket = bucket + (d >= thr).astype(I32)
    for h in range(N_HEADS):
        far = rb_ref[NUM_BUCKETS - 1, h]
        val = jnp.zeros((LANES, BIAS_W), F32)
        for b in range(NUM_BUCKETS - 1):
            val = jnp.where(bucket == b, rb_ref[b, h] - far, val)
        o_ref[h] = jnp.where(d >= 0, val * LOG2E, 0.0)


def _bias_tiles(rel_bias):
    return pl.pallas_call(
        _bias_kernel,
        out_shape=jax.ShapeDtypeStruct((N_HEADS, LANES, BIAS_W), F32),
        in_specs=[pl.BlockSpec(memory_space=pltpu.SMEM)],
        name="rel_bias_tiles",
    )(rel_bias)


def _decode_key(key):
    bits = jnp.where(key < 0, key ^ 0x7FFFFFFF, key)
    return lax.bitcast_convert_type(bits, F32)


def _count(mask):
    return jnp.sum(jnp.where(mask, 1.0, 0.0), axis=1, keepdims=True)


def _topk_mask(score, admissible, kpos, need):
    rows, n = score.shape
    score = jnp.where(admissible, score, -jnp.inf)
    need_f = float(need)

    nonneg = _count(score >= 0.0) >= need_f
    lo0 = jnp.where(nonneg, 0, INT_MIN).astype(I32)

    def bit_step(it, lo):
        cand = lo | (jnp.int32(1) << (30 - it))
        ok = _count(score >= _decode_key(cand)) >= need_f
        return jnp.where(ok, cand, lo)

    lo = lax.fori_loop(0, 31, bit_step, lo0)
    thr = jnp.where(lo <= KEY_NEG_INF, -jnp.inf, _decode_key(lo))

    gt = score > thr
    eq = score == thr
    n_gt = _count(gt)
    spare = need_f - n_gt
    nbits = int(n).bit_length()

    def trim_ties():
        def idx_step(it, lim):
            cand = lim + (jnp.int32(1) << (nbits - 1 - it))
            ok = _count(eq & (kpos < cand)) <= spare
            return jnp.where(ok, cand, lim)
        return lax.fori_loop(0, nbits, idx_step, jnp.zeros((rows, 1), I32))

    overshoot = jnp.max(n_gt + _count(eq)) > need_f
    lim = lax.cond(overshoot, trim_ties, lambda: jnp.full((rows, 1), 2 ** nbits, I32))
    return (gt | (eq & (kpos < lim))) & admissible


def _split3_keys(x):
    hi, lo = _split_bf16(x)
    return jnp.concatenate([hi, lo, hi], axis=1)


def _split3_queries(x):
    hi, lo = _split_bf16(x)
    return jnp.concatenate([hi, hi, lo], axis=1)


def _attn_prompt_body(i, ext, first, q_ref, qi_ref, wi_ref, bias_ref, o_ref, k_bf, v_bf, ki3, s_ref, tq, topk):
    def stack_heads(ref, h0, nh, width):
        return jnp.concatenate([ref[:, (h0 + h) * width:(h0 + h + 1) * width] for h in range(nh)], axis=0)

    wi = wi_ref[...] * (IDX_DIM ** -0.5 * IDX_HEADS ** -0.5)
    s_idx = _dot_nt(_split3_queries(stack_heads(qi_ref, 0, IDX_HEADS, IDX_DIM)), ki3[...])
    score = jnp.zeros((tq, ext), F32)
    for h in range(IDX_HEADS):
        score = score + jnp.maximum(s_idx[h * tq:(h + 1) * tq, :], 0.0) * wi[:, h:h + 1]

    kpos = lax.broadcasted_iota(I32, (tq, ext), 1)
    qpos = i * tq + lax.broadcasted_iota(I32, (tq, ext), 0)
    drop = jnp.where(_topk_mask(score, kpos <= qpos, kpos, topk), 0.0, NEG)

    for n in range(N_KV_HEADS):
        kn = k_bf[:, n * HEAD_DIM:(n + 1) * HEAD_DIM]
        vn = v_bf[:, n * HEAD_DIM:(n + 1) * HEAD_DIM]
        s_n = s_ref.at[n % 2]
        qn = (stack_heads(q_ref, n * GROUP, GROUP, HEAD_DIM) * (HEAD_DIM ** -0.5 * LOG2E)).astype(BF16)
        s_n[...] = _dot_nt(qn, kn)
        for g in range(GROUP):
            h = n * GROUP + g
            rows = slice(g * tq, (g + 1) * tq)

            def add_near_bias(h=h, rows=rows):
                ws = pl.multiple_of((i - 1) * tq, LANES)
                s_n[rows, pl.ds(ws, BIAS_W)] += bias_ref[h]

            if first:
                @pl.when(i == 0)
                def _(h=h, rows=rows):
                    s_n[rows, :tq] += bias_ref[h, :, LANES:]

                pl.when(i > 0)(add_near_bias)
            else:
                add_near_bias()

        s = (s_n[...].reshape(GROUP, tq, ext) + drop[None]).reshape(GROUP * tq, ext)
        p = jnp.exp2(s - jnp.max(s, axis=1, keepdims=True))
        l = jnp.sum(p, axis=1, keepdims=True)
        o = _dot(p.astype(BF16), vn) / l
        for g in range(GROUP):
            h = n * GROUP + g
            o_ref[:, h * HEAD_DIM:(h + 1) * HEAD_DIM] = o[g * tq:(g + 1) * tq, :].astype(o_ref.dtype)


def _attn_prompt_kernel(q_ref, k_ref, v_ref, qi_ref, ki_ref, wi_ref, bias_ref, o_ref,
                        k_bf, v_bf, ki3, s_ref, *, tq, topk, ext, tile0):
    j = pl.program_id(1)

    @pl.when(j == 0)
    def _():
        k_bf[...] = k_ref[:ext, :].astype(BF16)
        v_bf[...] = v_ref[:ext, :].astype(BF16)
        ki3[...] = _split3_keys(ki_ref[:ext, :])

    _attn_prompt_body(tile0 + j, ext, tile0 == 0, q_ref, qi_ref, wi_ref, bias_ref, o_ref, k_bf, v_bf, ki3,
                      s_ref, tq, topk)


def _attn_prompt_span(u_all, bias, batch, seq, tile0, ntile, tq, topk):
    nq = seq // tq
    ext = (tile0 + ntile) * tq
    kern = functools.partial(_attn_prompt_kernel, tq=tq, topk=topk, ext=ext, tile0=tile0)
    row = lambda b, j: b * nq + tile0 + j
    return pl.pallas_call(
        kern,
        out_shape=jax.ShapeDtypeStruct((batch * ntile * tq, D_ATTN), BF16),
        grid=(batch, ntile),
        in_specs=[
            pl.BlockSpec((tq, D_ATTN), lambda b, j: (row(b, j), C_Q // D_ATTN)),
            pl.BlockSpec((seq, D_KV), lambda b, j: (b, C_K // D_KV), pipeline_mode=pl.Buffered(1)),
            pl.BlockSpec((seq, D_KV), lambda b, j: (b, C_V // D_KV), pipeline_mode=pl.Buffered(1)),
            pl.BlockSpec((tq, IDX_HEADS * IDX_DIM), lambda b, j: (row(b, j), C_QI // (IDX_HEADS * IDX_DIM))),
            pl.BlockSpec((seq, IDX_DIM), lambda b, j: (b, C_KI // IDX_DIM), pipeline_mode=pl.Buffered(1)),
            pl.BlockSpec((tq, LANES), lambda b, j: (row(b, j), C_WI // LANES)),
            pl.BlockSpec((N_HEADS, LANES, BIAS_W), lambda b, j: (0, 0, 0)),
        ],
        out_specs=pl.BlockSpec((tq, D_ATTN), lambda b, j: (b * ntile + j, 0)),
        scratch_shapes=[
            pltpu.VMEM((ext, D_KV), BF16),
            pltpu.VMEM((ext, D_KV), BF16),
            pltpu.VMEM((ext, 3 * IDX_DIM), BF16),
            pltpu.VMEM((2, GROUP * tq, ext), F32),
        ],
        compiler_params=_compiler_params(("parallel", "arbitrary")),
        name=f"attn_prompt_{ext}",
    )(u_all, u_all, u_all, u_all, u_all, u_all, bias)


ATTN_SPANS = 4


def _attn_prompt(u_all, bias, batch, seq):
    tq = LANES
    nq = seq // tq
    topk = min(TOPK_MAX, seq // 4)
    ntile = nq // ATTN_SPANS
    assert ntile * ATTN_SPANS == nq and ntile >= 2
    spans = [_attn_prompt_span(u_all, bias, batch, seq, e * ntile, ntile, tq, topk).reshape(batch, ntile * tq, D_ATTN)
             for e in range(ATTN_SPANS)]
    return jnp.concatenate(spans, axis=1).reshape(batch * seq, D_ATTN)


def _page_copies(pt_ref, n_pages, rows, srcs_dsts_sems):
    def copy(src, dst, sem, p, pg):
        return pltpu.make_async_copy(src.at[pg], dst.at[pl.ds(pl.multiple_of(p * rows, rows), rows)], sem)

    def start(b, p0):
        def body(p, carry):
            pg = pt_ref[b, p0 + p]
            for src, dst, sem in srcs_dsts_sems:
                copy(src, dst, sem, p, pg).start()
            return carry
        lax.fori_loop(0, n_pages, body, 0)

    def wait():
        def body(p, carry):
            for src, dst, sem in srcs_dsts_sems:
                copy(src, dst, sem, p, 0).wait()
            return carry
        lax.fori_loop(0, n_pages, body, 0)

    return start, wait


def _prefetched(step, n_steps, start_into, wait_for):
    @pl.when(step == 0)
    def _():
        start_into(0, step)

    nxt = step + 1
    for slot in range(2):
        @pl.when((nxt < n_steps) & ((nxt & 1) == slot))
        def _(slot=slot):
            start_into(slot, nxt)

    for slot in range(2):
        @pl.when((step & 1) == slot)
        def _(slot=slot):
            wait_for(slot)


def _sample_scores_kernel(pt_ref, cache_ref, qi_ref, w_ref, kinew_ref, o_ref, ki_buf, sem,
                          *, n_pages, page, past, chunk):
    b = pl.program_id(0)
    total = ki_buf.shape[1]

    def copies(slot):
        return _page_copies(pt_ref, n_pages, page, [(cache_ref, ki_buf.at[slot], sem.at[slot])])

    _prefetched(b, pl.num_programs(0), lambda slot, step: copies(slot)[0](step, 0), lambda slot: copies(slot)[1]())
    ki_all = ki_buf.at[b & 1]
    ki_all[past:past + SUBLANES, :] = kinew_ref[0]
    ki_all[past + SUBLANES:, :] = jnp.zeros((total - past - SUBLANES, IDX_DIM), F32)

    q3 = _split3_queries(qi_ref[0])
    w = w_ref[0] * (IDX_DIM ** -0.5 * IDX_HEADS ** -0.5)
    rows = qi_ref.shape[1]
    for c0 in range(0, total, chunk):
        s = _dot_nt(q3, _split3_keys(ki_all[c0:c0 + chunk, :]))
        s = jnp.maximum(s, 0.0) * w
        o_ref[0, :, c0:c0 + chunk] = jnp.sum(s.reshape(rows // IDX_HEADS, IDX_HEADS, chunk), axis=1)


def _sample_scores(page_table, cache_idx, qi_rep, w_col, ki_new, past):
    bd, n_pages = page_table.shape
    page = cache_idx.shape[1]
    total = past + LANES
    rows = qi_rep.shape[1]
    nrow = rows // IDX_HEADS
    n_chunks = 5
    chunk = total // n_chunks
    assert chunk * n_chunks == total and chunk % LANES == 0
    kern = functools.partial(_sample_scores_kernel, n_pages=n_pages, page=page, past=past, chunk=chunk)
    return pl.pallas_call(
        kern,
        out_shape=jax.ShapeDtypeStruct((bd, nrow, total), F32),
        grid_spec=pltpu.PrefetchScalarGridSpec(
            num_scalar_prefetch=1,
            grid=(bd,),
            in_specs=[
                pl.BlockSpec(memory_space=pl.ANY),
                pl.BlockSpec((1, rows, IDX_DIM), lambda b, pt: (b, 0, 0)),
                pl.BlockSpec((1, rows, 1), lambda b, pt: (b, 0, 0)),
                pl.BlockSpec((1, SUBLANES, IDX_DIM), lambda b, pt: (b, 0, 0)),
            ],
            out_specs=pl.BlockSpec((1, nrow, total), lambda b, pt: (b, 0, 0)),
            scratch_shapes=[
                pltpu.VMEM((2, total, IDX_DIM), F32),
                pltpu.SemaphoreType.DMA((2,)),
            ],
        ),
        compiler_params=_compiler_params(("arbitrary",)),
        name="sample_scores",
    )(page_table, cache_idx, qi_rep, w_col, ki_new)


def _sample_topk_kernel(sc_ref, o_ref, *, past, topk, rows_per_batch, group_rows):
    rows, total = sc_ref.shape
    kpos = lax.broadcasted_iota(I32, (rows, total), 1)
    row = lax.broadcasted_iota(I32, (rows, total), 0)
    qpos = past + ((row & (rows_per_batch - 1)) >> (group_rows.bit_length() - 1))
    keep = _topk_mask(sc_ref[...], kpos <= qpos, kpos, topk)
    o_ref[...] = jnp.where(keep, 1.0, 0.0)


def _sample_topk(scores, past, topk, rows_per_batch):
    n, total = scores.shape
    tr = LANES
    assert n % tr == 0 and tr % rows_per_batch == 0 and rows_per_batch & (rows_per_batch - 1) == 0
    kern = functools.partial(_sample_topk_kernel, past=past, topk=topk, rows_per_batch=rows_per_batch,
                             group_rows=GROUP)
    return pl.pallas_call(
        kern,
        out_shape=jax.ShapeDtypeStruct((n, total), F32),
        grid=(n // tr,),
        in_specs=[pl.BlockSpec((tr, total), lambda i: (i, 0))],
        out_specs=pl.BlockSpec((tr, total), lambda i: (i, 0)),
        compiler_params=_compiler_params(("parallel",)),
        name="sample_topk",
    )(scores)


def _sample_attend_kernel(pt_ref, ck_ref, cv_ref, q_ref, knew_ref, vnew_ref, keep_ref, tail_ref,
                          o_ref, k_buf, v_buf, m_ref, l_ref, acc_ref, sem, *, layer, half_pages, page):
    s = pl.program_id(0)
    b = s >> 1
    half_keys = half_pages * page
    new0, new1 = half_keys * N_KV_HEADS, (half_keys + SUBLANES) * N_KV_HEADS

    def copies(slot):
        return _page_copies(pt_ref, half_pages, page * N_KV_HEADS,
                            [(ck_ref.at[layer], k_buf.at[slot], sem.at[0, slot]),
                             (cv_ref.at[layer], v_buf.at[slot], sem.at[1, slot])])

    _prefetched(s, pl.num_programs(0),
                lambda slot, step: copies(slot)[0](step >> 1, (step & 1) * half_pages),
                lambda slot: copies(slot)[1]())

    def partial(half, n):
        nkeys = half_keys + (LANES if half == 1 else 0)
        kn = k_buf[half, pl.ds(n, nkeys, stride=N_KV_HEADS), :].astype(BF16)
        vn = v_buf[half, pl.ds(n, nkeys, stride=N_KV_HEADS), :].astype(BF16)
        sc = _dot_nt((q_ref[0, n] * (HEAD_DIM ** -0.5 * LOG2E)).astype(BF16), kn)
        if half == 1:
            near = nkeys - BIAS_W
            sc = jnp.concatenate([sc[:, :near], sc[:, near:] + tail_ref[n]], axis=1)
        keep = keep_ref[0, :, half * half_keys:half * half_keys + nkeys] > 0.5
        sc = jnp.where(keep, sc, NEG)
        m = jnp.max(sc, axis=1, keepdims=True)
        p = jnp.exp2(sc - m)
        return m, jnp.sum(p, axis=1, keepdims=True), _dot(p.astype(BF16), vn)

    @pl.when((s & 1) == 0)
    def _():
        for n in range(N_KV_HEADS):
            m, l, acc = partial(0, n)
            m_ref[n] = m
            l_ref[n] = l
            acc_ref[n] = acc

    @pl.when((s & 1) == 1)
    def _():
        pad = jnp.zeros(((half_keys + LANES) * N_KV_HEADS - new1, HEAD_DIM), F32)
        k_buf[1, new0:new1, :] = knew_ref[0]
        v_buf[1, new0:new1, :] = vnew_ref[0]
        k_buf[1, new1:, :] = pad
        v_buf[1, new1:, :] = pad
        for n in range(N_KV_HEADS):
            m1, l1, acc1 = partial(1, n)
            m0 = m_ref[n]
            m = jnp.maximum(m0, m1)
            a0 = jnp.exp2(m0 - m)
            a1 = jnp.exp2(m1 - m)
            o = (acc_ref[n] * a0 + acc1 * a1) / (l_ref[n] * a0 + l1 * a1)
            o_ref[0, n] = o.astype(o_ref.dtype)


def _sample_attend(page_table, cache_k, cache_v, layer, page, q_rows, k_new, v_new, keep, tail, past):
    bd, n_pages = page_table.shape
    total = past + LANES
    nrow = q_rows.shape[2]
    half_pages = n_pages // 2
    assert half_pages * 2 == n_pages and BIAS_W <= LANES + half_pages * page
    buf_rows = (half_pages * page + LANES) * N_KV_HEADS
    kern = functools.partial(_sample_attend_kernel, layer=layer, half_pages=half_pages, page=page)
    per_batch = lambda rank: (lambda s, pt: (s // 2,) + (0,) * (rank - 1))
    return pl.pallas_call(
        kern,
        out_shape=jax.ShapeDtypeStruct((bd, N_KV_HEADS, nrow, HEAD_DIM), BF16),
        grid_spec=pltpu.PrefetchScalarGridSpec(
            num_scalar_prefetch=1,
            grid=(2 * bd,),
            in_specs=[
                pl.BlockSpec(memory_space=pl.ANY),
                pl.BlockSpec(memory_space=pl.ANY),
                pl.BlockSpec((1, N_KV_HEADS, nrow, HEAD_DIM), per_batch(4)),
                pl.BlockSpec((1, SUBLANES * N_KV_HEADS, HEAD_DIM), per_batch(3)),
                pl.BlockSpec((1, SUBLANES * N_KV_HEADS, HEAD_DIM), per_batch(3)),
                pl.BlockSpec((1, nrow, total), per_batch(3)),
                pl.BlockSpec((N_KV_HEADS, nrow, BIAS_W), lambda s, pt: (0, 0, 0)),
            ],
            out_specs=pl.BlockSpec((1, N_KV_HEADS, nrow, HEAD_DIM), per_batch(4)),
            scratch_shapes=[
                pltpu.VMEM((2, buf_rows, HEAD_DIM), F32),
                pltpu.VMEM((2, buf_rows, HEAD_DIM), F32),
                pltpu.VMEM((N_KV_HEADS, nrow, 1), F32),
                pltpu.VMEM((N_KV_HEADS, nrow, 1), F32),
                pltpu.VMEM((N_KV_HEADS, nrow, HEAD_DIM), F32),
                pltpu.SemaphoreType.DMA((2, 2)),
            ],
        ),
        compiler_params=_compiler_params(("arbitrary",)),
        name="sample_attend",
    )(page_table, cache_k, cache_v, q_rows, k_new, v_new, keep, tail)


def _out_proj_kernel(x_ref, pool_ref, attn_ref, wp_ref, wa_ref, o_ref):
    o_ref[...] = x_ref[...] + (_dot(pool_ref[...], wp_ref[...]) + _dot(attn_ref[...], wa_ref[...]))


def _out_proj(x, pool, attn, w_out, tm, tn):
    m, d = x.shape
    return pl.pallas_call(
        _out_proj_kernel,
        out_shape=jax.ShapeDtypeStruct((m, d), F32),
        grid=(m // tm, d // tn),
        in_specs=[
            pl.BlockSpec((tm, tn), lambda i, j: (i, j)),
            pl.BlockSpec((tm, D_POOL), lambda i, j: (i, 0)),
            pl.BlockSpec((tm, D_ATTN), lambda i, j: (i, 0)),
            pl.BlockSpec((D_POOL, tn), lambda i, j: (0, j)),
            pl.BlockSpec((D_ATTN, tn), lambda i, j: (D_POOL // D_ATTN, j)),
        ],
        out_specs=pl.BlockSpec((tm, tn), lambda i, j: (i, j)),
        compiler_params=_compiler_params(("parallel", "arbitrary")),
        name="out_proj",
    )(x, pool, attn, w_out, w_out)


def _proj_gain(g_q, g_k, g_idx_k):
    return jnp.concatenate([
        jnp.ones((D_POOL,), F32), jnp.tile(g_q, N_HEADS), jnp.tile(g_k, N_KV_HEADS),
        jnp.ones((C_KI - C_V,), F32), g_idx_k, jnp.ones((D_IN_PAD - C_WI,), F32)]).reshape(1, D_IN_PAD)


def _pad_rows(a, rows):
    return jnp.pad(a, ((0, 0), (0, rows - a.shape[1]), (0, 0)))


def kernel(x_prompt, x_sample, cache_k, cache_v, cache_idx_k, state_pool, page_table, g_ffn1, w_gate1, w_up1,
           w_down1, g_mix, w_in, g_q, g_k, g_idx_k, w_pool, pool_scale, w_out, rel_bias, g_ffn2, w_gate2,
           w_up2, w_down2):
    batch, seq, d = x_prompt.shape
    bd, t_new, _ = x_sample.shape
    depth = g_ffn1.shape[0]
    assert depth == 1 and d == D_MODEL
    past = page_table.shape[1] * cache_k.shape[2]
    mp, ms = batch * seq, bd * t_new
    tm_p, tm_s = 512, ms
    tf = 256

    xp = x_prompt.reshape(mp, d)
    xs = x_sample.reshape(ms, d)
    l = 0

    xs, wg1, wu1, wd1 = _ffn_cast(xs, g_ffn1[l], w_gate1[l], w_up1[l], w_down1[l], tf)
    xp = _ffn(xp, g_ffn1[l], wg1, wu1, wd1, tm_p, tf)

    w_in_p = jnp.pad(w_in[l].astype(BF16), ((0, 0), (0, D_IN_PAD - D_IN)))
    gain = _proj_gain(g_q[l], g_k[l], g_idx_k[l])
    up = _proj(xp, g_mix[l], w_in_p, gain, tm_p)
    us = _proj(xs, g_mix[l], w_in_p, gain, tm_s)

    wp = w_pool[l].astype(BF16)
    scale = pool_scale[l].reshape(1, D_POOL)
    bias = _bias_tiles(rel_bias)

    pool_p = _pool_prompt(up, wp, scale, batch, seq, 256)
    attn_p = _attn_prompt(up, bias, batch, seq)

    hist = state_pool[l]
    u_pool_s = us[:, :D_POOL].reshape(bd, t_new, D_POOL)
    ext = jnp.concatenate([hist, u_pool_s], axis=1)
    pool_s = _pool_sample(ext.transpose(1, 0, 2), wp, scale, t_new)
    pool_s = pool_s.transpose(1, 0, 2).reshape(ms, D_POOL)

    us3 = us.reshape(bd, t_new, D_IN_PAD)
    rows_i = t_new * GROUP * IDX_HEADS
    qi_s = us3[:, :, C_QI:C_KI].reshape(bd, t_new, 1, IDX_HEADS, IDX_DIM)
    qi_rep = jnp.broadcast_to(qi_s, (bd, t_new, GROUP, IDX_HEADS, IDX_DIM)).reshape(bd, rows_i, IDX_DIM)
    wi_s = us3[:, :, C_WI:C_WI + IDX_HEADS].reshape(bd, t_new, 1, IDX_HEADS)
    w_col = jnp.broadcast_to(wi_s, (bd, t_new, GROUP, IDX_HEADS)).reshape(bd, rows_i, 1)
    ki_new = _pad_rows(us3[:, :, C_KI:C_WI], SUBLANES)
    k_new = _pad_rows(us3[:, :, C_K:C_V], SUBLANES)
    v_new = _pad_rows(us3[:, :, C_V:C_QI], SUBLANES)
    n_phys, page = cache_k.shape[1], cache_k.shape[2]
    topk_s = min(TOPK_MAX, (past + t_new) // 4)
    nrow = t_new * GROUP
    scores_s = _sample_scores(page_table, cache_idx_k[l], qi_rep, w_col, ki_new, past)
    keep_s = _sample_topk(scores_s.reshape(bd * nrow, past + LANES), past, topk_s, nrow)
    keep_s = keep_s.reshape(bd, nrow, past + LANES)

    q_rows = us3[:, :, C_Q:C_K].reshape(bd, t_new, N_KV_HEADS, GROUP, HEAD_DIM)
    q_rows = q_rows.transpose(0, 2, 1, 3, 4).reshape(bd, N_KV_HEADS, nrow, HEAD_DIM)
    tail = bias[:, :t_new, :].reshape(N_KV_HEADS, GROUP, t_new, BIAS_W)
    tail = tail.transpose(0, 2, 1, 3).reshape(N_KV_HEADS, nrow, BIAS_W)
    rows_shape = (depth, n_phys, page * N_KV_HEADS, HEAD_DIM)
    new_shape = (bd, SUBLANES * N_KV_HEADS, HEAD_DIM)
    attn_s = _sample_attend(page_table, cache_k.reshape(rows_shape), cache_v.reshape(rows_shape), l, page, q_rows,
                            k_new.reshape(new_shape), v_new.reshape(new_shape), keep_s, tail, past)
    attn_s = attn_s.reshape(bd, N_KV_HEADS, t_new, GROUP, HEAD_DIM).transpose(0, 2, 1, 3, 4).reshape(ms, D_ATTN)

    w_out_bf = w_out[l].astype(BF16)
    xp = _out_proj(xp, pool_p, attn_p, w_out_bf, tm_p, 1024)
    xs = _out_proj(xs, pool_s, attn_s, w_out_bf, tm_s, 1024)

    xs, wg2, wu2, wd2 = _ffn_cast(xs, g_ffn2[l], w_gate2[l], w_up2[l], w_down2[l], tf)
    xp = _ffn(xp, g_ffn2[l], wg2, wu2, wd2, tm_p, tf)

    up4 = up.reshape(batch, seq, D_IN_PAD)
    return (
        xp.reshape(batch, seq, d),
        xs.reshape(bd, t_new, d),
        up4[:, :, C_K:C_V].reshape(1, batch, seq, N_KV_HEADS, HEAD_DIM),
        up4[:, :, C_V:C_QI].reshape(1, batch, seq, N_KV_HEADS, HEAD_DIM),
        up4[:, :, C_KI:C_WI].reshape(1, batch, seq, IDX_DIM),
        up4[:, seq - POOL_HIST:, :D_POOL].reshape(1, batch, POOL_HIST, D_POOL),
        us3[:, :, C_K:C_V].reshape(1, bd, t_new, N_KV_HEADS, HEAD_DIM),
        us3[:, :, C_V:C_QI].reshape(1, bd, t_new, N_KV_HEADS, HEAD_DIM),
        us3[:, :, C_KI:C_WI].reshape(1, bd, t_new, IDX_DIM),
        ext[:, t_new:].reshape(1, bd, POOL_HIST, D_POOL),
    )
```

```python
import functools
import math

import jax
import jax.numpy as jnp
import numpy as np
from jax import lax
from jax.experimental import pallas as pl
from jax.experimental.pallas import tpu as pltpu

F32 = jnp.float32
BF16 = jnp.bfloat16
I32 = jnp.int32

SUBLANES = 8
LANES = 128
VMEM_LIMIT_BYTES = 56 * 1024 * 1024

D_MODEL = 4096
D_POOL = D_MODEL // 2
POOL_WINDOWS = (2, 4, 8, 16)
N_POOL_GROUPS = len(POOL_WINDOWS)
POOL_GROUP = D_POOL // N_POOL_GROUPS
POOL_HIST = max(POOL_WINDOWS) - 1
HEAD_DIM = 128
N_HEADS = (D_MODEL - D_POOL) // HEAD_DIM
N_KV_HEADS = 4
GROUP = N_HEADS // N_KV_HEADS
D_ATTN = N_HEADS * HEAD_DIM
D_KV = N_KV_HEADS * HEAD_DIM
IDX_HEADS = 8
IDX_DIM = 128
TOPK_MAX = 256
NUM_BUCKETS = 32
MAX_DISTANCE = 128
EPS = 1e-6
NEG = -1e30
LOG2E = math.log2(math.e)

C_Q = D_POOL
C_K = C_Q + D_ATTN
C_V = C_K + D_KV
C_QI = C_V + D_KV
C_KI = C_QI + IDX_HEADS * IDX_DIM
C_WI = C_KI + IDX_DIM
D_IN = C_WI + IDX_HEADS
D_IN_PAD = ((D_IN + 2 * LANES - 1) // (2 * LANES)) * (2 * LANES)
PROJ_TN = 10 * LANES
assert D_IN_PAD % PROJ_TN == 0

INT_MIN = -(2 ** 31)
KEY_NEG_INF = INT_MIN + 0x007FFFFF


def _compiler_params(semantics):
    return pltpu.CompilerParams(dimension_semantics=semantics, vmem_limit_bytes=VMEM_LIMIT_BYTES)


def _dot(a, b):
    return jnp.dot(a, b, preferred_element_type=F32)


def _dot_nt(a, b):
    return lax.dot_general(a, b, (((1,), (1,)), ((), ())), preferred_element_type=F32)


def _rms_scale(x):
    return lax.rsqrt(jnp.mean(x * x, axis=-1, keepdims=True) + EPS)


def _split_bf16(x):
    hi = x.astype(BF16)
    lo = (x - hi.astype(F32)).astype(BF16)
    return hi, lo


def _ffn_step(j, last, x_ref, g_ref, load_weights, o_ref, h_ref):
    @pl.when(j == 0)
    def _():
        x = x_ref[...]
        h_ref[...] = (x * _rms_scale(x) * g_ref[...]).astype(BF16)
        o_ref[...] = jnp.zeros_like(o_ref)

    wg, wu, wd = load_weights()
    h = h_ref[...]
    a = _dot(h, wg)
    b = _dot(h, wu)
    s = (a * jax.nn.sigmoid(a) * b).astype(BF16)
    o_ref[...] += _dot(s, wd)

    @pl.when(j == last)
    def _():
        o_ref[...] = x_ref[...] + 0.5 * o_ref[...]


def _ffn_kernel(x_ref, g_ref, wg_ref, wu_ref, wd_ref, o_ref, h_ref):
    _ffn_step(pl.program_id(1), pl.num_programs(1) - 1, x_ref, g_ref,
              lambda: (wg_ref[0], wu_ref[0], wd_ref[...]), o_ref, h_ref)


def _ffn_cast_kernel(x_ref, g_ref, wg_ref, wu_ref, wd_ref, o_ref, wg_o, wu_o, wd_o, h_ref):
    def load_weights():
        wg = wg_ref[...].astype(BF16)
        wu = wu_ref[...].astype(BF16)
        wd = wd_ref[...].astype(BF16)
        wg_o[0] = wg
        wu_o[0] = wu
        wd_o[...] = wd
        return wg, wu, wd

    _ffn_step(pl.program_id(0), pl.num_programs(0) - 1, x_ref, g_ref, load_weights, o_ref, h_ref)


def _ffn_cast(x, g, wg, wu, wd, tf):
    m, d = x.shape
    f = wg.shape[1]
    w_in = lambda: pl.BlockSpec((d, tf), lambda j: (0, j))
    w_tile = lambda: pl.BlockSpec((1, d, tf), lambda j: (j, 0, 0))
    w_out = lambda: pl.BlockSpec((tf, d), lambda j: (j, 0))
    return pl.pallas_call(
        _ffn_cast_kernel,
        out_shape=[jax.ShapeDtypeStruct((m, d), F32), jax.ShapeDtypeStruct((f // tf, d, tf), BF16),
                   jax.ShapeDtypeStruct((f // tf, d, tf), BF16), jax.ShapeDtypeStruct((f, d), BF16)],
        grid=(f // tf,),
        in_specs=[pl.BlockSpec((m, d), lambda j: (0, 0)), pl.BlockSpec((1, d), lambda j: (0, 0)),
                  w_in(), w_in(), w_out()],
        out_specs=[pl.BlockSpec((m, d), lambda j: (0, 0)), w_tile(), w_tile(), w_out()],
        scratch_shapes=[pltpu.VMEM((m, d), BF16)],
        compiler_params=_compiler_params(("arbitrary",)),
        name="swiglu_half_cast",
    )(x, g.reshape(1, d), wg, wu, wd)


def _ffn(x, g, wg, wu, wd, tm):
    m, d = x.shape
    nf, _, tf = wg.shape
    return pl.pallas_call(
        _ffn_kernel,
        out_shape=jax.ShapeDtypeStruct((m, d), F32),
        grid=(m // tm, nf),
        in_specs=[
            pl.BlockSpec((tm, d), lambda i, j: (i, 0), pipeline_mode=pl.Buffered(1)),
            pl.BlockSpec((1, d), lambda i, j: (0, 0)),
            pl.BlockSpec((1, d, tf), lambda i, j: (j, 0, 0)),
            pl.BlockSpec((1, d, tf), lambda i, j: (j, 0, 0)),
            pl.BlockSpec((tf, d), lambda i, j: (j, 0)),
        ],
        out_specs=pl.BlockSpec((tm, d), lambda i, j: (i, 0)),
        scratch_shapes=[pltpu.VMEM((tm, d), BF16)],
        compiler_params=_compiler_params(("parallel", "arbitrary")),
        name="swiglu_half",
    )(x, g.reshape(1, d), wg, wu, wd)


def _proj_kernel(x_ref, g_ref, w_ref, gain_ref, o_ref, h_ref):
    j = pl.program_id(1)

    @pl.when(j == 0)
    def _():
        x = x_ref[...]
        h_ref[...] = (x * _rms_scale(x) * g_ref[...]).astype(BF16)

    u = _dot(h_ref[...], w_ref[...])
    gain = gain_ref[...]

    def is_normed(col):
        return C_Q <= col < C_V or C_KI <= col < C_WI

    for jt in range(D_IN_PAD // PROJ_TN):
        @pl.when(j == jt)
        def _(jt=jt):
            c = 0
            while c < PROJ_TN:
                if is_normed(jt * PROJ_TN + c):
                    part = u[:, c:c + LANES]
                    o_ref[:, c:c + LANES] = part * _rms_scale(part) * gain[:, c:c + LANES]
                    c += LANES
                else:
                    end = c
                    while end < PROJ_TN and not is_normed(jt * PROJ_TN + end):
                        end += LANES
                    o_ref[:, c:end] = u[:, c:end]
                    c = end


def _proj(x, g_mix, w_in, gain, tm):
    m, d = x.shape
    n = w_in.shape[1]
    return pl.pallas_call(
        _proj_kernel,
        out_shape=jax.ShapeDtypeStruct((m, n), F32),
        grid=(m // tm, n // PROJ_TN),
        in_specs=[
            pl.BlockSpec((tm, d), lambda i, j: (i, 0), pipeline_mode=pl.Buffered(1)),
            pl.BlockSpec((1, d), lambda i, j: (0, 0)),
            pl.BlockSpec((d, PROJ_TN), lambda i, j: (0, j)),
            pl.BlockSpec((1, PROJ_TN), lambda i, j: (0, j)),
        ],
        out_specs=pl.BlockSpec((tm, PROJ_TN), lambda i, j: (i, j)),
        scratch_shapes=[pltpu.VMEM((tm, d), BF16)],
        compiler_params=_compiler_params(("parallel", "arbitrary")),
        name="in_proj",
    )(x, g_mix.reshape(1, d), w_in, gain)


HALO = 2 * SUBLANES


def _pool_groups(ext_rows, cur, count_of, wp_ref, scale_ref, store):
    for g, w in enumerate(POOL_WINDOWS):
        lo, hi = g * POOL_GROUP, (g + 1) * POOL_GROUP
        acc = ext_rows(0, lo, hi)
        for d in range(1, w):
            acc = acc + ext_rows(d, lo, hi)
        diff = acc / count_of(w) - cur(lo, hi)
        out = _dot(diff.astype(BF16), wp_ref[g]) * scale_ref[:, lo:hi]
        store(lo, hi, out)


def _pool_prompt_kernel(u_ref, halo_ref, wp_ref, scale_ref, o_ref, ext_ref, *, tp):
    i = pl.program_id(1)
    halo = halo_ref[...]
    ext_ref[:HALO, :] = jnp.where(i == 0, jnp.zeros_like(halo), halo)
    ext_ref[HALO:, :] = u_ref[...]
    pos = i * tp + lax.broadcasted_iota(I32, (tp, POOL_GROUP), 0)

    def store(lo, hi, out):
        o_ref[:, lo:hi] = out.astype(o_ref.dtype)

    _pool_groups(
        lambda d, lo, hi: ext_ref[HALO - d:HALO - d + tp, lo:hi],
        lambda lo, hi: u_ref[:, lo:hi],
        lambda w: jnp.minimum(pos + 1, w).astype(F32),
        wp_ref, scale_ref, store)


def _pool_prompt(u_all, wp, scale, batch, seq, tp):
    nt = seq // tp
    kern = functools.partial(_pool_prompt_kernel, tp=tp)
    return pl.pallas_call(
        kern,
        out_shape=jax.ShapeDtypeStruct((batch * seq, D_POOL), BF16),
        grid=(batch, nt),
        in_specs=[
            pl.BlockSpec((tp, D_POOL), lambda b, i: (b * nt + i, 0)),
            pl.BlockSpec((HALO, D_POOL),
                         lambda b, i: (jnp.maximum((b * seq + i * tp) // HALO - 1, 0), 0)),
            pl.BlockSpec((N_POOL_GROUPS, POOL_GROUP, POOL_GROUP), lambda b, i: (0, 0, 0)),
            pl.BlockSpec((1, D_POOL), lambda b, i: (0, 0)),
        ],
        out_specs=pl.BlockSpec((tp, D_POOL), lambda b, i: (b * nt + i, 0)),
        scratch_shapes=[pltpu.VMEM((HALO + tp, D_POOL), F32)],
        compiler_params=_compiler_params(("parallel", "arbitrary")),
        name="pool_prompt",
    )(u_all, u_all, wp, scale)


def _pool_sample_kernel(ext_ref, wp_ref, scale_ref, o_ref, *, t_new):
    for t in range(t_new):
        def store(lo, hi, out, t=t):
            o_ref[t, :, lo:hi] = out.astype(o_ref.dtype)

        _pool_groups(
            lambda d, lo, hi, t=t: ext_ref[POOL_HIST + t - d, :, lo:hi],
            lambda lo, hi, t=t: ext_ref[POOL_HIST + t, :, lo:hi],
            lambda w: float(w),
            wp_ref, scale_ref, store)


def _pool_sample(ext_t, wp, scale, t_new):
    _, bd, _ = ext_t.shape
    kern = functools.partial(_pool_sample_kernel, t_new=t_new)
    return pl.pallas_call(
        kern,
        out_shape=jax.ShapeDtypeStruct((t_new, bd, D_POOL), BF16),
        compiler_params=pltpu.CompilerParams(vmem_limit_bytes=VMEM_LIMIT_BYTES),
        name="pool_sample",
    )(ext_t, wp, scale)


def _bucket_thresholds():
    n = np.arange(0, 4 * MAX_DISTANCE, dtype=np.int64)
    max_exact = NUM_BUCKETS // 2
    nf = np.maximum(n, 1).astype(np.float32)
    large = max_exact + (np.log(nf / np.float32(max_exact)) / np.float32(math.log(MAX_DISTANCE / max_exact))
                         * np.float32(NUM_BUCKETS - max_exact)).astype(np.int32)
    large = np.minimum(large, NUM_BUCKETS - 1)
    bucket = np.where(n < max_exact, n, large)
    assert np.all(np.diff(bucket) >= 0) and bucket[-1] == NUM_BUCKETS - 1
    return [int(np.argmax(bucket >= b)) for b in range(1, NUM_BUCKETS)]


BUCKET_THRESHOLDS = _bucket_thresholds()
FAR_DISTANCE = BUCKET_THRESHOLDS[-1]
BIAS_W = 2 * LANES
assert FAR_DISTANCE <= LANES


def _bias_kernel(rb_ref, o_ref):
    r = lax.broadcasted_iota(I32, (LANES, BIAS_W), 0)
    c = lax.broadcasted_iota(I32, (LANES, BIAS_W), 1)
    d = LANES + r - c
    bucket = jnp.zeros((LANES, BIAS_W), I32)
    for thr in BUCKET_THRESHOLDS:
        bucket = bucket + (d >= thr).astype(I32)
    for h in range(N_HEADS):
        far = rb_ref[NUM_BUCKETS - 1, h]
        val = jnp.zeros((LANES, BIAS_W), F32)
        for b in range(NUM_BUCKETS - 1):
            val = jnp.where(bucket == b, rb_ref[b, h] - far, val)
        o_ref[h] = jnp.where(d >= 0, val * LOG2E, 0.0)


def _bias_tiles(rel_bias):
    return pl.pallas_call(
        _bias_kernel,
        out_shape=jax.ShapeDtypeStruct((N_HEADS, LANES, BIAS_W), F32),
        in_specs=[pl.BlockSpec(memory_space=pltpu.SMEM)],
        name="rel_bias_tiles",
    )(rel_bias)


def _decode_key(key):
    bits = jnp.where(key < 0, key ^ 0x7FFFFFFF, key)
    return lax.bitcast_convert_type(bits, F32)


def _count(mask):
    return jnp.sum(jnp.where(mask, 1.0, 0.0), axis=1, keepdims=True)


def _topk_mask(score, admissible, kpos, need):
    rows, n = score.shape
    score = jnp.where(admissible, score, -jnp.inf)
    need_f = float(need)

    nonneg = _count(score >= 0.0) >= need_f
    lo0 = jnp.where(nonneg, 0, INT_MIN).astype(I32)

    def bit_step(it, lo):
        cand = lo | (jnp.int32(1) << (30 - it))
        ok = _count(score >= _decode_key(cand)) >= need_f
        return jnp.where(ok, cand, lo)

    lo = lax.fori_loop(0, 31, bit_step, lo0)
    thr = jnp.where(lo <= KEY_NEG_INF, -jnp.inf, _decode_key(lo))

    gt = score > thr
    eq = score == thr
    n_gt = _count(gt)
    spare = need_f - n_gt
    nbits = int(n).bit_length()

    def trim_ties():
        def idx_step(it, lim):
            cand = lim + (jnp.int32(1) << (nbits - 1 - it))
            ok = _count(eq & (kpos < cand)) <= spare
            return jnp.where(ok, cand, lim)
        return lax.fori_loop(0, nbits, idx_step, jnp.zeros((rows, 1), I32))

    overshoot = jnp.max(n_gt + _count(eq)) > need_f
    lim = lax.cond(overshoot, trim_ties, lambda: jnp.full((rows, 1), 2 ** nbits, I32))
    return (gt | (eq & (kpos < lim))) & admissible


def _split3_keys(x):
    hi, lo = _split_bf16(x)
    return jnp.concatenate([hi, lo, hi], axis=1)


def _split3_queries(x):
    hi, lo = _split_bf16(x)
    return jnp.concatenate([hi, hi, lo], axis=1)


def _attn_prompt_body(i, ext, first, q_ref, qi_ref, wi_ref, bias_ref, o_ref, k_bf, v_bf, ki3, s_ref, tq, topk):
    def stack_heads(ref, h0, nh, width):
        return jnp.concatenate([ref[:, (h0 + h) * width:(h0 + h + 1) * width] for h in range(nh)], axis=0)

    wi = wi_ref[...] * (IDX_DIM ** -0.5 * IDX_HEADS ** -0.5)
    s_idx = _dot_nt(_split3_queries(stack_heads(qi_ref, 0, IDX_HEADS, IDX_DIM)), ki3[...])
    score = jnp.zeros((tq, ext), F32)
    for h in range(IDX_HEADS):
        score = score + jnp.maximum(s_idx[h * tq:(h + 1) * tq, :], 0.0) * wi[:, h:h + 1]

    kpos = lax.broadcasted_iota(I32, (tq, ext), 1)
    qpos = i * tq + lax.broadcasted_iota(I32, (tq, ext), 0)
    drop = jnp.where(_topk_mask(score, kpos <= qpos, kpos, topk), 0.0, NEG)

    for n in range(N_KV_HEADS):
        kn = k_bf[:, n * HEAD_DIM:(n + 1) * HEAD_DIM]
        vn = v_bf[:, n * HEAD_DIM:(n + 1) * HEAD_DIM]
        s_n = s_ref.at[n % 2]
        qn = (stack_heads(q_ref, n * GROUP, GROUP, HEAD_DIM) * (HEAD_DIM ** -0.5 * LOG2E)).astype(BF16)
        s_n[...] = _dot_nt(qn, kn)
        for g in range(GROUP):
            h = n * GROUP + g
            rows = slice(g * tq, (g + 1) * tq)

            def add_near_bias(h=h, rows=rows):
                ws = pl.multiple_of((i - 1) * tq, LANES)
                s_n[rows, pl.ds(ws, BIAS_W)] += bias_ref[h]

            if first:
                @pl.when(i == 0)
                def _(h=h, rows=rows):
                    s_n[rows, :tq] += bias_ref[h, :, LANES:]

                pl.when(i > 0)(add_near_bias)
            else:
                add_near_bias()

        s = (s_n[...].reshape(GROUP, tq, ext) + drop[None]).reshape(GROUP * tq, ext)
        p = jnp.exp2(s - jnp.max(s, axis=1, keepdims=True))
        l = jnp.sum(p, axis=1, keepdims=True)
        o = _dot(p.astype(BF16), vn) / l
        for g in range(GROUP):
            h = n * GROUP + g
            o_ref[:, h * HEAD_DIM:(h + 1) * HEAD_DIM] = o[g * tq:(g + 1) * tq, :].astype(o_ref.dtype)


def _attn_prompt_kernel(q_ref, k_ref, v_ref, qi_ref, ki_ref, wi_ref, bias_ref, o_ref,
                        k_bf, v_bf, ki3, s_ref, *, tq, topk, ext, tile0):
    j = pl.program_id(1)

    @pl.when(j == 0)
    def _():
        k_bf[...] = k_ref[:ext, :].astype(BF16)
        v_bf[...] = v_ref[:ext, :].astype(BF16)
        ki3[...] = _split3_keys(ki_ref[:ext, :])

    _attn_prompt_body(tile0 + j, ext, tile0 == 0, q_ref, qi_ref, wi_ref, bias_ref, o_ref, k_bf, v_bf, ki3,
                      s_ref, tq, topk)


def _attn_prompt_span(u_all, bias, batch, seq, tile0, ntile, tq, topk):
    nq = seq // tq
    ext = (tile0 + ntile) * tq
    kern = functools.partial(_attn_prompt_kernel, tq=tq, topk=topk, ext=ext, tile0=tile0)
    row = lambda b, j: b * nq + tile0 + j
    return pl.pallas_call(
        kern,
        out_shape=jax.ShapeDtypeStruct((batch * ntile * tq, D_ATTN), BF16),
        grid=(batch, ntile),
        in_specs=[
            pl.BlockSpec((tq, D_ATTN), lambda b, j: (row(b, j), C_Q // D_ATTN)),
            pl.BlockSpec((seq, D_KV), lambda b, j: (b, C_K // D_KV), pipeline_mode=pl.Buffered(1)),
            pl.BlockSpec((seq, D_KV), lambda b, j: (b, C_V // D_KV), pipeline_mode=pl.Buffered(1)),
            pl.BlockSpec((tq, IDX_HEADS * IDX_DIM), lambda b, j: (row(b, j), C_QI // (IDX_HEADS * IDX_DIM))),
            pl.BlockSpec((seq, IDX_DIM), lambda b, j: (b, C_KI // IDX_DIM), pipeline_mode=pl.Buffered(1)),
            pl.BlockSpec((tq, LANES), lambda b, j: (row(b, j), C_WI // LANES)),
            pl.BlockSpec((N_HEADS, LANES, BIAS_W), lambda b, j: (0, 0, 0)),
        ],
        out_specs=pl.BlockSpec((tq, D_ATTN), lambda b, j: (b * ntile + j, 0)),
        scratch_shapes=[
            pltpu.VMEM((ext, D_KV), BF16),
            pltpu.VMEM((ext, D_KV), BF16),
            pltpu.VMEM((ext, 3 * IDX_DIM), BF16),
            pltpu.VMEM((2, GROUP * tq, ext), F32),
        ],
        compiler_params=_compiler_params(("parallel", "arbitrary")),
        name=f"attn_prompt_{ext}",
    )(u_all, u_all, u_all, u_all, u_all, u_all, bias)


ATTN_SPANS = 4


def _attn_prompt(u_all, bias, batch, seq):
    tq = LANES
    nq = seq // tq
    topk = min(TOPK_MAX, seq // 4)
    ntile = nq // ATTN_SPANS
    assert ntile * ATTN_SPANS == nq and ntile >= 2
    spans = [_attn_prompt_span(u_all, bias, batch, seq, e * ntile, ntile, tq, topk).reshape(batch, ntile * tq, D_ATTN)
             for e in range(ATTN_SPANS)]
    return jnp.concatenate(spans, axis=1).reshape(batch * seq, D_ATTN)


def _page_copies(pt_ref, n_pages, rows, srcs_dsts_sems):
    def copy(src, dst, sem, p, pg):
        return pltpu.make_async_copy(src.at[pg], dst.at[pl.ds(pl.multiple_of(p * rows, rows), rows)], sem)

    def start(b, p0):
        def body(p, carry):
            pg = pt_ref[b, p0 + p]
            for src, dst, sem in srcs_dsts_sems:
                copy(src, dst, sem, p, pg).start()
            return carry
        lax.fori_loop(0, n_pages, body, 0)

    def wait():
        def body(p, carry):
            for src, dst, sem in srcs_dsts_sems:
                copy(src, dst, sem, p, 0).wait()
            return carry
        lax.fori_loop(0, n_pages, body, 0)

    return start, wait


def _prefetched(step, n_steps, start_into, wait_for):
    @pl.when(step == 0)
    def _():
        start_into(0, step)

    nxt = step + 1
    for slot in range(2):
        @pl.when((nxt < n_steps) & ((nxt & 1) == slot))
        def _(slot=slot):
            start_into(slot, nxt)

    for slot in range(2):
        @pl.when((step & 1) == slot)
        def _(slot=slot):
            wait_for(slot)


def _sample_scores_kernel(pt_ref, cache_ref, qi_ref, w_ref, kinew_ref, o_ref, ki_buf, sem,
                          *, n_pages, page, past, chunk):
    b = pl.program_id(0)
    total = ki_buf.shape[1]

    def copies(slot):
        return _page_copies(pt_ref, n_pages, page, [(cache_ref, ki_buf.at[slot], sem.at[slot])])

    _prefetched(b, pl.num_programs(0), lambda slot, step: copies(slot)[0](step, 0), lambda slot: copies(slot)[1]())
    ki_all = ki_buf.at[b & 1]
    ki_all[past:past + SUBLANES, :] = kinew_ref[0]
    ki_all[past + SUBLANES:, :] = jnp.zeros((total - past - SUBLANES, IDX_DIM), F32)

    q3 = _split3_queries(qi_ref[0])
    w = w_ref[0] * (IDX_DIM ** -0.5 * IDX_HEADS ** -0.5)
    rows = qi_ref.shape[1]
    for c0 in range(0, total, chunk):
        s = _dot_nt(q3, _split3_keys(ki_all[c0:c0 + chunk, :]))
        s = jnp.maximum(s, 0.0) * w
        o_ref[0, :, c0:c0 + chunk] = jnp.sum(s.reshape(rows // IDX_HEADS, IDX_HEADS, chunk), axis=1)


def _sample_scores(page_table, cache_idx, qi_rep, w_col, ki_new, past):
    bd, n_pages = page_table.shape
    page = cache_idx.shape[1]
    total = past + LANES
    rows = qi_rep.shape[1]
    nrow = rows // IDX_HEADS
    n_chunks = 5
    chunk = total // n_chunks
    assert chunk * n_chunks == total and chunk % LANES == 0
    kern = functools.partial(_sample_scores_kernel, n_pages=n_pages, page=page, past=past, chunk=chunk)
    return pl.pallas_call(
        kern,
        out_shape=jax.ShapeDtypeStruct((bd, nrow, total), F32),
        grid_spec=pltpu.PrefetchScalarGridSpec(
            num_scalar_prefetch=1,
            grid=(bd,),
            in_specs=[
                pl.BlockSpec(memory_space=pl.ANY),
                pl.BlockSpec((1, rows, IDX_DIM), lambda b, pt: (b, 0, 0)),
                pl.BlockSpec((1, rows, 1), lambda b, pt: (b, 0, 0)),
                pl.BlockSpec((1, SUBLANES, IDX_DIM), lambda b, pt: (b, 0, 0)),
            ],
            out_specs=pl.BlockSpec((1, nrow, total), lambda b, pt: (b, 0, 0)),
            scratch_shapes=[
                pltpu.VMEM((2, total, IDX_DIM), F32),
                pltpu.SemaphoreType.DMA((2,)),
            ],
        ),
        compiler_params=_compiler_params(("arbitrary",)),
        name="sample_scores",
    )(page_table, cache_idx, qi_rep, w_col, ki_new)


def _sample_topk_kernel(sc_ref, o_ref, *, past, topk, rows_per_batch, group_rows):
    rows, total = sc_ref.shape
    kpos = lax.broadcasted_iota(I32, (rows, total), 1)
    row = lax.broadcasted_iota(I32, (rows, total), 0)
    qpos = past + ((row & (rows_per_batch - 1)) >> (group_rows.bit_length() - 1))
    keep = _topk_mask(sc_ref[...], kpos <= qpos, kpos, topk)
    o_ref[...] = jnp.where(keep, 1.0, 0.0)


def _sample_topk(scores, past, topk, rows_per_batch):
    n, total = scores.shape
    tr = LANES
    assert n % tr == 0 and tr % rows_per_batch == 0 and rows_per_batch & (rows_per_batch - 1) == 0
    kern = functools.partial(_sample_topk_kernel, past=past, topk=topk, rows_per_batch=rows_per_batch,
                             group_rows=GROUP)
    return pl.pallas_call(
        kern,
        out_shape=jax.ShapeDtypeStruct((n, total), F32),
        grid=(n // tr,),
        in_specs=[pl.BlockSpec((tr, total), lambda i: (i, 0))],
        out_specs=pl.BlockSpec((tr, total), lambda i: (i, 0)),
        compiler_params=_compiler_params(("parallel",)),
        name="sample_topk",
    )(scores)


def _sample_attend_kernel(pt_ref, ck_ref, cv_ref, q_ref, knew_ref, vnew_ref, keep_ref, tail_ref,
                          o_ref, k_buf, v_buf, m_ref, l_ref, acc_ref, sem, *, layer, half_pages, page):
    s = pl.program_id(0)
    b = s >> 1
    half_keys = half_pages * page
    new0, new1 = half_keys * N_KV_HEADS, (half_keys + SUBLANES) * N_KV_HEADS

    def copies(slot):
        return _page_copies(pt_ref, half_pages, page * N_KV_HEADS,
                            [(ck_ref.at[layer], k_buf.at[slot], sem.at[0, slot]),
                             (cv_ref.at[layer], v_buf.at[slot], sem.at[1, slot])])

    _prefetched(s, pl.num_programs(0),
                lambda slot, step: copies(slot)[0](step >> 1, (step & 1) * half_pages),
                lambda slot: copies(slot)[1]())

    def partial(half, n):
        nkeys = half_keys + (LANES if half == 1 else 0)
        kn = k_buf[half, pl.ds(n, nkeys, stride=N_KV_HEADS), :].astype(BF16)
        vn = v_buf[half, pl.ds(n, nkeys, stride=N_KV_HEADS), :].astype(BF16)
        sc = _dot_nt((q_ref[0, n] * (HEAD_DIM ** -0.5 * LOG2E)).astype(BF16), kn)
        if half == 1:
            near = nkeys - BIAS_W
            sc = jnp.concatenate([sc[:, :near], sc[:, near:] + tail_ref[n]], axis=1)
        keep = keep_ref[0, :, half * half_keys:half * half_keys + nkeys] > 0.5
        sc = jnp.where(keep, sc, NEG)
        m = jnp.max(sc, axis=1, keepdims=True)
        p = jnp.exp2(sc - m)
        return m, jnp.sum(p, axis=1, keepdims=True), _dot(p.astype(BF16), vn)

    @pl.when((s & 1) == 0)
    def _():
        for n in range(N_KV_HEADS):
            m, l, acc = partial(0, n)
            m_ref[n] = m
            l_ref[n] = l
            acc_ref[n] = acc

    @pl.when((s & 1) == 1)
    def _():
        pad = jnp.zeros(((half_keys + LANES) * N_KV_HEADS - new1, HEAD_DIM), F32)
        k_buf[1, new0:new1, :] = knew_ref[0]
        v_buf[1, new0:new1, :] = vnew_ref[0]
        k_buf[1, new1:, :] = pad
        v_buf[1, new1:, :] = pad
        for n in range(N_KV_HEADS):
            m1, l1, acc1 = partial(1, n)
            m0 = m_ref[n]
            m = jnp.maximum(m0, m1)
            a0 = jnp.exp2(m0 - m)
            a1 = jnp.exp2(m1 - m)
            o = (acc_ref[n] * a0 + acc1 * a1) / (l_ref[n] * a0 + l1 * a1)
            o_ref[0, n] = o.astype(o_ref.dtype)


def _sample_attend(page_table, cache_k, cache_v, layer, page, q_rows, k_new, v_new, keep, tail, past):
    bd, n_pages = page_table.shape
    total = past + LANES
    nrow = q_rows.shape[2]
    half_pages = n_pages // 2
    assert half_pages * 2 == n_pages and BIAS_W <= LANES + half_pages * page
    buf_rows = (half_pages * page + LANES) * N_KV_HEADS
    kern = functools.partial(_sample_attend_kernel, layer=layer, half_pages=half_pages, page=page)
    per_batch = lambda rank: (lambda s, pt: (s // 2,) + (0,) * (rank - 1))
    return pl.pallas_call(
        kern,
        out_shape=jax.ShapeDtypeStruct((bd, N_KV_HEADS, nrow, HEAD_DIM), BF16),
        grid_spec=pltpu.PrefetchScalarGridSpec(
            num_scalar_prefetch=1,
            grid=(2 * bd,),
            in_specs=[
                pl.BlockSpec(memory_space=pl.ANY),
                pl.BlockSpec(memory_space=pl.ANY),
                pl.BlockSpec((1, N_KV_HEADS, nrow, HEAD_DIM), per_batch(4)),
                pl.BlockSpec((1, SUBLANES * N_KV_HEADS, HEAD_DIM), per_batch(3)),
                pl.BlockSpec((1, SUBLANES * N_KV_HEADS, HEAD_DIM), per_batch(3)),
                pl.BlockSpec((1, nrow, total), per_batch(3)),
                pl.BlockSpec((N_KV_HEADS, nrow, BIAS_W), lambda s, pt: (0, 0, 0)),
            ],
            out_specs=pl.BlockSpec((1, N_KV_HEADS, nrow, HEAD_DIM), per_batch(4)),
            scratch_shapes=[
                pltpu.VMEM((2, buf_rows, HEAD_DIM), F32),
                pltpu.VMEM((2, buf_rows, HEAD_DIM), F32),
                pltpu.VMEM((N_KV_HEADS, nrow, 1), F32),
                pltpu.VMEM((N_KV_HEADS, nrow, 1), F32),
                pltpu.VMEM((N_KV_HEADS, nrow, HEAD_DIM), F32),
                pltpu.SemaphoreType.DMA((2, 2)),
            ],
        ),
        compiler_params=_compiler_params(("arbitrary",)),
        name="sample_attend",
    )(page_table, cache_k, cache_v, q_rows, k_new, v_new, keep, tail)


def _out_proj_kernel(x_ref, pool_ref, attn_ref, wp_ref, wa_ref, o_ref):
    o_ref[...] = x_ref[...] + (_dot(pool_ref[...], wp_ref[...]) + _dot(attn_ref[...], wa_ref[...]))


def _out_proj(x, pool, attn, w_out, tm, tn):
    m, d = x.shape
    return pl.pallas_call(
        _out_proj_kernel,
        out_shape=jax.ShapeDtypeStruct((m, d), F32),
        grid=(m // tm, d // tn),
        in_specs=[
            pl.BlockSpec((tm, tn), lambda i, j: (i, j)),
            pl.BlockSpec((tm, D_POOL), lambda i, j: (i, 0)),
            pl.BlockSpec((tm, D_ATTN), lambda i, j: (i, 0)),
            pl.BlockSpec((D_POOL, tn), lambda i, j: (0, j)),
            pl.BlockSpec((D_ATTN, tn), lambda i, j: (D_POOL // D_ATTN, j)),
        ],
        out_specs=pl.BlockSpec((tm, tn), lambda i, j: (i, j)),
        compiler_params=_compiler_params(("parallel", "arbitrary")),
        name="out_proj",
    )(x, pool, attn, w_out, w_out)


def _proj_gain(g_q, g_k, g_idx_k):
    return jnp.concatenate([
        jnp.ones((D_POOL,), F32), jnp.tile(g_q, N_HEADS), jnp.tile(g_k, N_KV_HEADS),
        jnp.ones((C_KI - C_V,), F32), g_idx_k, jnp.ones((D_IN_PAD - C_WI,), F32)]).reshape(1, D_IN_PAD)


def _pad_rows(a, rows):
    return jnp.pad(a, ((0, 0), (0, rows - a.shape[1]), (0, 0)))


def kernel(x_prompt, x_sample, cache_k, cache_v, cache_idx_k, state_pool, page_table, g_ffn1, w_gate1, w_up1,
           w_down1, g_mix, w_in, g_q, g_k, g_idx_k, w_pool, pool_scale, w_out, rel_bias, g_ffn2, w_gate2,
           w_up2, w_down2):
    batch, seq, d = x_prompt.shape
    bd, t_new, _ = x_sample.shape
    depth = g_ffn1.shape[0]
    assert depth == 1 and d == D_MODEL
    past = page_table.shape[1] * cache_k.shape[2]
    mp, ms = batch * seq, bd * t_new
    tm_p, tm_s = 512, ms
    tf = 256

    xp = x_prompt.reshape(mp, d)
    xs = x_sample.reshape(ms, d)
    l = 0

    xs, wg1, wu1, wd1 = _ffn_cast(xs, g_ffn1[l], w_gate1[l], w_up1[l], w_down1[l], tf)
    xp = _ffn(xp, g_ffn1[l], wg1, wu1, wd1, tm_p)

    w_in_p = jnp.pad(w_in[l], ((0, 0), (0, D_IN_PAD - D_IN))).astype(BF16)
    gain = _proj_gain(g_q[l], g_k[l], g_idx_k[l])
    up = _proj(xp, g_mix[l], w_in_p, gain, tm_p)
    us = _proj(xs, g_mix[l], w_in_p, gain, tm_s)

    wp = w_pool[l].astype(BF16)
    scale = pool_scale[l].reshape(1, D_POOL)
    bias = _bias_tiles(rel_bias)

    pool_p = _pool_prompt(up, wp, scale, batch, seq, 256)
    attn_p = _attn_prompt(up, bias, batch, seq)

    hist = state_pool[l]
    u_pool_s = us[:, :D_POOL].reshape(bd, t_new, D_POOL)
    ext = jnp.concatenate([hist, u_pool_s], axis=1)
    pool_s = _pool_sample(ext.transpose(1, 0, 2), wp, scale, t_new)
    pool_s = pool_s.transpose(1, 0, 2).reshape(ms, D_POOL)

    us3 = us.reshape(bd, t_new, D_IN_PAD)
    rows_i = t_new * GROUP * IDX_HEADS
    qi_s = us3[:, :, C_QI:C_KI].reshape(bd, t_new, 1, IDX_HEADS, IDX_DIM)
    qi_rep = jnp.broadcast_to(qi_s, (bd, t_new, GROUP, IDX_HEADS, IDX_DIM)).reshape(bd, rows_i, IDX_DIM)
    wi_s = us3[:, :, C_WI:C_WI + IDX_HEADS].reshape(bd, t_new, 1, IDX_HEADS)
    w_col = jnp.broadcast_to(wi_s, (bd, t_new, GROUP, IDX_HEADS)).reshape(bd, rows_i, 1)
    ki_new = _pad_rows(us3[:, :, C_KI:C_WI], SUBLANES)
    k_new = _pad_rows(us3[:, :, C_K:C_V], SUBLANES)
    v_new = _pad_rows(us3[:, :, C_V:C_QI], SUBLANES)
    n_phys, page = cache_k.shape[1], cache_k.shape[2]
    topk_s = min(TOPK_MAX, (past + t_new) // 4)
    nrow = t_new * GROUP
    scores_s = _sample_scores(page_table, cache_idx_k[l], qi_rep, w_col, ki_new, past)
    keep_s = _sample_topk(scores_s.reshape(bd * nrow, past + LANES), past, topk_s, nrow)
    keep_s = keep_s.reshape(bd, nrow, past + LANES)

    q_rows = us3[:, :, C_Q:C_K].reshape(bd, t_new, N_KV_HEADS, GROUP, HEAD_DIM)
    q_rows = q_rows.transpose(0, 2, 1, 3, 4).reshape(bd, N_KV_HEADS, nrow, HEAD_DIM)
    tail = bias[:, :t_new, :].reshape(N_KV_HEADS, GROUP, t_new, BIAS_W)
    tail = tail.transpose(0, 2, 1, 3).reshape(N_KV_HEADS, nrow, BIAS_W)
    rows_shape = (depth, n_phys, page * N_KV_HEADS, HEAD_DIM)
    new_shape = (bd, SUBLANES * N_KV_HEADS, HEAD_DIM)
    attn_s = _sample_attend(page_table, cache_k.reshape(rows_shape), cache_v.reshape(rows_shape), l, page, q_rows,
                            k_new.reshape(new_shape), v_new.reshape(new_shape), keep_s, tail, past)
    attn_s = attn_s.reshape(bd, N_KV_HEADS, t_new, GROUP, HEAD_DIM).transpose(0, 2, 1, 3, 4).reshape(ms, D_ATTN)

    w_out_bf = w_out[l].astype(BF16)
    xp = _out_proj(xp, pool_p, attn_p, w_out_bf, tm_p, 1024)
    xs = _out_proj(xs, pool_s, attn_s, w_out_bf, tm_s, 1024)

    xs, wg2, wu2, wd2 = _ffn_cast(xs, g_ffn2[l], w_gate2[l], w_up2[l], w_down2[l], tf)
    xp = _ffn(xp, g_ffn2[l], wg2, wu2, wd2, tm_p)

    up4 = up.reshape(batch, seq, D_IN_PAD)
    return (
        xp.reshape(batch, seq, d),
        xs.reshape(bd, t_new, d),
        up4[:, :, C_K:C_V].reshape(1, batch, seq, N_KV_HEADS, HEAD_DIM),
        up4[:, :, C_V:C_QI].reshape(1, batch, seq, N_KV_HEADS, HEAD_DIM),
        up4[:, :, C_KI:C_WI].reshape(1, batch, seq, IDX_DIM),
        up4[:, seq - POOL_HIST:, :D_POOL].reshape(1, batch, POOL_HIST, D_POOL),
        us3[:, :, C_K:C_V].reshape(1, bd, t_new, N_KV_HEADS, HEAD_DIM),
        us3[:, :, C_V:C_QI].reshape(1, bd, t_new, N_KV_HEADS, HEAD_DIM),
        us3[:, :, C_KI:C_WI].reshape(1, bd, t_new, IDX_DIM),
        ext[:, t_new:].reshape(1, bd, POOL_HIST, D_POOL),
    )
```

```python
import functools
import math

import jax
import jax.numpy as jnp
import numpy as np
from jax import lax
from jax.experimental import pallas as pl
from jax.experimental.pallas import tpu as pltpu

F32 = jnp.float32
BF16 = jnp.bfloat16
I32 = jnp.int32

SUBLANES = 8
LANES = 128
VMEM_LIMIT_BYTES = 56 * 1024 * 1024

D_MODEL = 4096
D_POOL = D_MODEL // 2
POOL_WINDOWS = (2, 4, 8, 16)
N_POOL_GROUPS = len(POOL_WINDOWS)
POOL_GROUP = D_POOL // N_POOL_GROUPS
POOL_HIST = max(POOL_WINDOWS) - 1
HEAD_DIM = 128
N_HEADS = (D_MODEL - D_POOL) // HEAD_DIM
N_KV_HEADS = 4
GROUP = N_HEADS // N_KV_HEADS
D_ATTN = N_HEADS * HEAD_DIM
D_KV = N_KV_HEADS * HEAD_DIM
IDX_HEADS = 8
IDX_DIM = 128
TOPK_MAX = 256
NUM_BUCKETS = 32
MAX_DISTANCE = 128
EPS = 1e-6
NEG = -1e30
LOG2E = math.log2(math.e)

C_Q = D_POOL
C_K = C_Q + D_ATTN
C_V = C_K + D_KV
C_QI = C_V + D_KV
C_KI = C_QI + IDX_HEADS * IDX_DIM
C_WI = C_KI + IDX_DIM
D_IN = C_WI + IDX_HEADS
D_IN_PAD = ((D_IN + 2 * LANES - 1) // (2 * LANES)) * (2 * LANES)
PROJ_TN = 10 * LANES
assert D_IN_PAD % PROJ_TN == 0

INT_MIN = -(2 ** 31)
KEY_NEG_INF = INT_MIN + 0x007FFFFF


def _compiler_params(semantics):
    return pltpu.CompilerParams(dimension_semantics=semantics, vmem_limit_bytes=VMEM_LIMIT_BYTES)


def _dot(a, b):
    return jnp.dot(a, b, preferred_element_type=F32)


def _dot_nt(a, b):
    return lax.dot_general(a, b, (((1,), (1,)), ((), ())), preferred_element_type=F32)


def _rms_scale(x):
    return lax.rsqrt(jnp.mean(x * x, axis=-1, keepdims=True) + EPS)


def _split_bf16(x):
    hi = x.astype(BF16)
    lo = (x - hi.astype(F32)).astype(BF16)
    return hi, lo


def _ffn_step(j, last, x_ref, g_ref, load_weights, o_ref, h_ref):
    @pl.when(j == 0)
    def _():
        x = x_ref[...]
        h_ref[...] = (x * _rms_scale(x) * g_ref[...]).astype(BF16)
        o_ref[...] = jnp.zeros_like(o_ref)

    wg, wu, wd = load_weights()
    h = h_ref[...]
    a = _dot(h, wg)
    b = _dot(h, wu)
    s = (a * jax.nn.sigmoid(a) * b).astype(BF16)
    o_ref[...] += _dot(s, wd)

    @pl.when(j == last)
    def _():
        o_ref[...] = x_ref[...] + 0.5 * o_ref[...]


def _ffn_kernel(x_ref, g_ref, wg_ref, wu_ref, wd_ref, o_ref, h_ref):
    _ffn_step(pl.program_id(1), pl.num_programs(1) - 1, x_ref, g_ref,
              lambda: (wg_ref[0], wu_ref[0], wd_ref[...]), o_ref, h_ref)


def _ffn_cast_kernel(x_ref, g_ref, wg_ref, wu_ref, wd_ref, o_ref, wg_o, wu_o, wd_o, h_ref):
    def load_weights():
        wg = wg_ref[...].astype(BF16)
        wu = wu_ref[...].astype(BF16)
        wd = wd_ref[...].astype(BF16)
        wg_o[0] = wg
        wu_o[0] = wu
        wd_o[...] = wd
        return wg, wu, wd

    _ffn_step(pl.program_id(0), pl.num_programs(0) - 1, x_ref, g_ref, load_weights, o_ref, h_ref)


def _ffn_cast(x, g, wg, wu, wd, tf):
    m, d = x.shape
    f = wg.shape[1]
    w_in = lambda: pl.BlockSpec((d, tf), lambda j: (0, j))
    w_tile = lambda: pl.BlockSpec((1, d, tf), lambda j: (j, 0, 0))
    w_out = lambda: pl.BlockSpec((tf, d), lambda j: (j, 0))
    return pl.pallas_call(
        _ffn_cast_kernel,
        out_shape=[jax.ShapeDtypeStruct((m, d), F32), jax.ShapeDtypeStruct((f // tf, d, tf), BF16),
                   jax.ShapeDtypeStruct((f // tf, d, tf), BF16), jax.ShapeDtypeStruct((f, d), BF16)],
        grid=(f // tf,),
        in_specs=[pl.BlockSpec((m, d), lambda j: (0, 0)), pl.BlockSpec((1, d), lambda j: (0, 0)),
                  w_in(), w_in(), w_out()],
        out_specs=[pl.BlockSpec((m, d), lambda j: (0, 0)), w_tile(), w_tile(), w_out()],
        scratch_shapes=[pltpu.VMEM((m, d), BF16)],
        compiler_params=_compiler_params(("arbitrary",)),
        name="swiglu_half_cast",
    )(x, g.reshape(1, d), wg, wu, wd)


def _ffn(x, g, wg, wu, wd, tm):
    m, d = x.shape
    nf, _, tf = wg.shape
    return pl.pallas_call(
        _ffn_kernel,
        out_shape=jax.ShapeDtypeStruct((m, d), F32),
        grid=(m // tm, nf),
        in_specs=[
            pl.BlockSpec((tm, d), lambda i, j: (i, 0), pipeline_mode=pl.Buffered(1)),
            pl.BlockSpec((1, d), lambda i, j: (0, 0)),
            pl.BlockSpec((1, d, tf), lambda i, j: (j, 0, 0)),
            pl.BlockSpec((1, d, tf), lambda i, j: (j, 0, 0)),
            pl.BlockSpec((tf, d), lambda i, j: (j, 0)),
        ],
        out_specs=pl.BlockSpec((tm, d), lambda i, j: (i, 0)),
        scratch_shapes=[pltpu.VMEM((tm, d), BF16)],
        compiler_params=_compiler_params(("parallel", "arbitrary")),
        name="swiglu_half",
    )(x, g.reshape(1, d), wg, wu, wd)


def _proj_kernel(x_ref, g_ref, w_ref, gain_ref, o_ref, h_ref):
    j = pl.program_id(1)

    @pl.when(j == 0)
    def _():
        x = x_ref[...]
        h_ref[...] = (x * _rms_scale(x) * g_ref[...]).astype(BF16)

    u = _dot(h_ref[...], w_ref[...])
    gain = gain_ref[...]

    def is_normed(col):
        return C_Q <= col < C_V or C_KI <= col < C_WI

    for jt in range(D_IN_PAD // PROJ_TN):
        @pl.when(j == jt)
        def _(jt=jt):
            c = 0
            while c < PROJ_TN:
                if is_normed(jt * PROJ_TN + c):
                    part = u[:, c:c + LANES]
                    o_ref[:, c:c + LANES] = part * _rms_scale(part) * gain[:, c:c + LANES]
                    c += LANES
                else:
                    end = c
                    while end < PROJ_TN and not is_normed(jt * PROJ_TN + end):
                        end += LANES
                    o_ref[:, c:end] = u[:, c:end]
                    c = end


def _proj(x, g_mix, w_in, gain, tm):
    m, d = x.shape
    n = w_in.shape[1]
    return pl.pallas_call(
        _proj_kernel,
        out_shape=jax.ShapeDtypeStruct((m, n), F32),
        grid=(m // tm, n // PROJ_TN),
        in_specs=[
            pl.BlockSpec((tm, d), lambda i, j: (i, 0), pipeline_mode=pl.Buffered(1)),
            pl.BlockSpec((1, d), lambda i, j: (0, 0)),
            pl.BlockSpec((d, PROJ_TN), lambda i, j: (0, j)),
            pl.BlockSpec((1, PROJ_TN), lambda i, j: (0, j)),
        ],
        out_specs=pl.BlockSpec((tm, PROJ_TN), lambda i, j: (i, j)),
        scratch_shapes=[pltpu.VMEM((tm, d), BF16)],
        compiler_params=_compiler_params(("parallel", "arbitrary")),
        name="in_proj",
    )(x, g_mix.reshape(1, d), w_in, gain)


HALO = 2 * SUBLANES


def _pool_groups(ext_rows, cur, count_of, wp_ref, scale_ref, store):
    for g, w in enumerate(POOL_WINDOWS):
        lo, hi = g * POOL_GROUP, (g + 1) * POOL_GROUP
        acc = ext_rows(0, lo, hi)
        for d in range(1, w):
            acc = acc + ext_rows(d, lo, hi)
        diff = acc / count_of(w) - cur(lo, hi)
        out = _dot(diff.astype(BF16), wp_ref[g]) * scale_ref[:, lo:hi]
        store(lo, hi, out)


def _pool_prompt_kernel(u_ref, halo_ref, wp_ref, scale_ref, o_ref, ext_ref, *, tp):
    i = pl.program_id(1)
    halo = halo_ref[...]
    ext_ref[:HALO, :] = jnp.where(i == 0, jnp.zeros_like(halo), halo)
    ext_ref[HALO:, :] = u_ref[...]
    pos = i * tp + lax.broadcasted_iota(I32, (tp, POOL_GROUP), 0)

    def store(lo, hi, out):
        o_ref[:, lo:hi] = out.astype(o_ref.dtype)

    _pool_groups(
        lambda d, lo, hi: ext_ref[HALO - d:HALO - d + tp, lo:hi],
        lambda lo, hi: u_ref[:, lo:hi],
        lambda w: jnp.minimum(pos + 1, w).astype(F32),
        wp_ref, scale_ref, store)


def _pool_prompt(u_all, wp, scale, batch, seq, tp):
    nt = seq // tp
    kern = functools.partial(_pool_prompt_kernel, tp=tp)
    return pl.pallas_call(
        kern,
        out_shape=jax.ShapeDtypeStruct((batch * seq, D_POOL), BF16),
        grid=(batch, nt),
        in_specs=[
            pl.BlockSpec((tp, D_POOL), lambda b, i: (b * nt + i, 0)),
            pl.BlockSpec((HALO, D_POOL),
                         lambda b, i: (jnp.maximum((b * seq + i * tp) // HALO - 1, 0), 0)),
            pl.BlockSpec((N_POOL_GROUPS, POOL_GROUP, POOL_GROUP), lambda b, i: (0, 0, 0)),
            pl.BlockSpec((1, D_POOL), lambda b, i: (0, 0)),
        ],
        out_specs=pl.BlockSpec((tp, D_POOL), lambda b, i: (b * nt + i, 0)),
        scratch_shapes=[pltpu.VMEM((HALO + tp, D_POOL), F32)],
        compiler_params=_compiler_params(("parallel", "arbitrary")),
        name="pool_prompt",
    )(u_all, u_all, wp, scale)


def _pool_sample_kernel(ext_ref, wp_ref, scale_ref, o_ref, *, t_new):
    for t in range(t_new):
        def store(lo, hi, out, t=t):
            o_ref[t, :, lo:hi] = out.astype(o_ref.dtype)

        _pool_groups(
            lambda d, lo, hi, t=t: ext_ref[POOL_HIST + t - d, :, lo:hi],
            lambda lo, hi, t=t: ext_ref[POOL_HIST + t, :, lo:hi],
            lambda w: float(w),
            wp_ref, scale_ref, store)


def _pool_sample(ext_t, wp, scale, t_new):
    _, bd, _ = ext_t.shape
    kern = functools.partial(_pool_sample_kernel, t_new=t_new)
    return pl.pallas_call(
        kern,
        out_shape=jax.ShapeDtypeStruct((t_new, bd, D_POOL), BF16),
        compiler_params=pltpu.CompilerParams(vmem_limit_bytes=VMEM_LIMIT_BYTES),
        name="pool_sample",
    )(ext_t, wp, scale)


def _bucket_thresholds():
    n = np.arange(0, 4 * MAX_DISTANCE, dtype=np.int64)
    max_exact = NUM_BUCKETS // 2
    nf = np.maximum(n, 1).astype(np.float32)
    large = max_exact + (np.log(nf / np.float32(max_exact)) / np.float32(math.log(MAX_DISTANCE / max_exact))
                         * np.float32(NUM_BUCKETS - max_exact)).astype(np.int32)
    large = np.minimum(large, NUM_BUCKETS - 1)
    bucket = np.where(n < max_exact, n, large)
    assert np.all(np.diff(bucket) >= 0) and bucket[-1] == NUM_BUCKETS - 1
    return [int(np.argmax(bucket >= b)) for b in range(1, NUM_BUCKETS)]


BUCKET_THRESHOLDS = _bucket_thresholds()
FAR_DISTANCE = BUCKET_THRESHOLDS[-1]
BIAS_W = 2 * LANES
assert FAR_DISTANCE <= LANES


def _bias_kernel(rb_ref, o_ref):
    r = lax.broadcasted_iota(I32, (LANES, BIAS_W), 0)
    c = lax.broadcasted_iota(I32, (LANES, BIAS_W), 1)
    d = LANES + r - c
    bucket = jnp.zeros((LANES, BIAS_W), I32)
    for thr in BUCKET_THRESHOLDS:
        bucket = bucket + (d >= thr).astype(I32)
    for h in range(N_HEADS):
        far = rb_ref[NUM_BUCKETS - 1, h]
        val = jnp.zeros((LANES, BIAS_W), F32)
        for b in range(NUM_BUCKETS - 1):
            val = jnp.where(bucket == b, rb_ref[b, h] - far, val)
        o_ref[h] = jnp.where(d >= 0, val * LOG2E, 0.0)


def _bias_tiles(rel_bias):
    return pl.pallas_call(
        _bias_kernel,
        out_shape=jax.ShapeDtypeStruct((N_HEADS, LANES, BIAS_W), F32),
        in_specs=[pl.BlockSpec(memory_space=pltpu.SMEM)],
        name="rel_bias_tiles",
    )(rel_bias)


RADIX4_MAX_ELEMENTS = 192 * SUBLANES * LANES


def _decode_key(key):
    bits = jnp.where(key < 0, key ^ 0x7FFFFFFF, key)
    return lax.bitcast_convert_type(bits, F32)


def _count(mask):
    return jnp.sum(jnp.where(mask, 1.0, 0.0), axis=1, keepdims=True)


def _topk_mask(score, admissible, kpos, need):
    rows, n = score.shape
    score = jnp.where(admissible, score, -jnp.inf)
    need_f = float(need)

    nonneg = _count(score >= 0.0) >= need_f
    lo0 = jnp.where(nonneg, 0, INT_MIN).astype(I32)

    def kept(cand):
        return _count(score >= _decode_key(cand)) >= need_f

    def bit_step(it, lo):
        cand = lo | (jnp.int32(1) << (30 - it))
        return jnp.where(kept(cand), cand, lo)

    def pair_step(it, lo):
        shift = 28 - 2 * it
        c1, c2, c3 = (lo | (jnp.int32(m) << shift) for m in (1, 2, 3))
        return jnp.where(kept(c3), c3, jnp.where(kept(c2), c2, jnp.where(kept(c1), c1, lo)))

    if rows * n <= RADIX4_MAX_ELEMENTS:
        lo = lax.fori_loop(0, 15, pair_step, bit_step(0, lo0))
    else:
        lo = lax.fori_loop(0, 31, bit_step, lo0)
    thr = jnp.where(lo <= KEY_NEG_INF, -jnp.inf, _decode_key(lo))

    gt = score > thr
    eq = score == thr
    n_gt = _count(gt)
    spare = need_f - n_gt
    nbits = int(n).bit_length()

    def trim_ties():
        def idx_step(it, lim):
            cand = lim + (jnp.int32(1) << (nbits - 1 - it))
            ok = _count(eq & (kpos < cand)) <= spare
            return jnp.where(ok, cand, lim)
        return lax.fori_loop(0, nbits, idx_step, jnp.zeros((rows, 1), I32))

    overshoot = jnp.max(n_gt + _count(eq)) > need_f
    lim = lax.cond(overshoot, trim_ties, lambda: jnp.full((rows, 1), 2 ** nbits, I32))
    return (gt | (eq & (kpos < lim))) & admissible


def _split3_keys(x):
    hi, lo = _split_bf16(x)
    return jnp.concatenate([hi, lo, hi], axis=1)


def _split3_queries(x):
    hi, lo = _split_bf16(x)
    return jnp.concatenate([hi, hi, lo], axis=1)


def _attn_prompt_body(i, ext, first, q_ref, qi_ref, wi_ref, bias_ref, o_ref, k_bf, v_bf, ki3, s_ref, tq, topk):
    def stack_heads(ref, h0, nh, width):
        return jnp.concatenate([ref[:, (h0 + h) * width:(h0 + h + 1) * width] for h in range(nh)], axis=0)

    wi = wi_ref[...] * (IDX_DIM ** -0.5 * IDX_HEADS ** -0.5)
    s_idx = _dot_nt(_split3_queries(stack_heads(qi_ref, 0, IDX_HEADS, IDX_DIM)), ki3[...])
    score = jnp.zeros((tq, ext), F32)
    for h in range(IDX_HEADS):
        score = score + jnp.maximum(s_idx[h * tq:(h + 1) * tq, :], 0.0) * wi[:, h:h + 1]

    kpos = lax.broadcasted_iota(I32, (tq, ext), 1)
    qpos = i * tq + lax.broadcasted_iota(I32, (tq, ext), 0)
    drop = jnp.where(_topk_mask(score, kpos <= qpos, kpos, topk), 0.0, NEG)

    for n in range(N_KV_HEADS):
        kn = k_bf[:, n * HEAD_DIM:(n + 1) * HEAD_DIM]
        vn = v_bf[:, n * HEAD_DIM:(n + 1) * HEAD_DIM]
        s_n = s_ref.at[n % 2]
        qn = (stack_heads(q_ref, n * GROUP, GROUP, HEAD_DIM) * (HEAD_DIM ** -0.5 * LOG2E)).astype(BF16)
        s_n[...] = _dot_nt(qn, kn)
        for g in range(GROUP):
            h = n * GROUP + g
            rows = slice(g * tq, (g + 1) * tq)

            def add_near_bias(h=h, rows=rows):
                ws = pl.multiple_of((i - 1) * tq, LANES)
                s_n[rows, pl.ds(ws, BIAS_W)] += bias_ref[h]

            if first:
                @pl.when(i == 0)
                def _(h=h, rows=rows):
                    s_n[rows, :tq] += bias_ref[h, :, LANES:]

                pl.when(i > 0)(add_near_bias)
            else:
                add_near_bias()

        s = (s_n[...].reshape(GROUP, tq, ext) + drop[None]).reshape(GROUP * tq, ext)
        p = jnp.exp2(s - jnp.max(s, axis=1, keepdims=True))
        l = jnp.sum(p, axis=1, keepdims=True)
        o = _dot(p.astype(BF16), vn) / l
        for g in range(GROUP):
            h = n * GROUP + g
            o_ref[:, h * HEAD_DIM:(h + 1) * HEAD_DIM] = o[g * tq:(g + 1) * tq, :].astype(o_ref.dtype)


def _attn_prompt_kernel(q_ref, k_ref, v_ref, qi_ref, ki_ref, wi_ref, bias_ref, o_ref,
                        k_bf, v_bf, ki3, s_ref, *, tq, topk, ext, tile0):
    j = pl.program_id(1)

    @pl.when(j == 0)
    def _():
        k_bf[...] = k_ref[:ext, :].astype(BF16)
        v_bf[...] = v_ref[:ext, :].astype(BF16)
        ki3[...] = _split3_keys(ki_ref[:ext, :])

    _attn_prompt_body(tile0 + j, ext, tile0 == 0, q_ref, qi_ref, wi_ref, bias_ref, o_ref, k_bf, v_bf, ki3,
                      s_ref, tq, topk)


def _attn_prompt_span(u_all, bias, batch, seq, tile0, ntile, tq, topk):
    nq = seq // tq
    ext = (tile0 + ntile) * tq
    kern = functools.partial(_attn_prompt_kernel, tq=tq, topk=topk, ext=ext, tile0=tile0)
    row = lambda b, j: b * nq + tile0 + j
    return pl.pallas_call(
        kern,
        out_shape=jax.ShapeDtypeStruct((batch * ntile * tq, D_ATTN), BF16),
        grid=(batch, ntile),
        in_specs=[
            pl.BlockSpec((tq, D_ATTN), lambda b, j: (row(b, j), C_Q // D_ATTN)),
            pl.BlockSpec((seq, D_KV), lambda b, j: (b, C_K // D_KV), pipeline_mode=pl.Buffered(1)),
            pl.BlockSpec((seq, D_KV), lambda b, j: (b, C_V // D_KV), pipeline_mode=pl.Buffered(1)),
            pl.BlockSpec((tq, IDX_HEADS * IDX_DIM), lambda b, j: (row(b, j), C_QI // (IDX_HEADS * IDX_DIM))),
            pl.BlockSpec((seq, IDX_DIM), lambda b, j: (b, C_KI // IDX_DIM), pipeline_mode=pl.Buffered(1)),
            pl.BlockSpec((tq, LANES), lambda b, j: (row(b, j), C_WI // LANES)),
            pl.BlockSpec((N_HEADS, LANES, BIAS_W), lambda b, j: (0, 0, 0)),
        ],
        out_specs=pl.BlockSpec((tq, D_ATTN), lambda b, j: (b * ntile + j, 0)),
        scratch_shapes=[
            pltpu.VMEM((ext, D_KV), BF16),
            pltpu.VMEM((ext, D_KV), BF16),
            pltpu.VMEM((ext, 3 * IDX_DIM), BF16),
            pltpu.VMEM((2, GROUP * tq, ext), F32),
        ],
        compiler_params=_compiler_params(("parallel", "arbitrary")),
        name=f"attn_prompt_{ext}",
    )(u_all, u_all, u_all, u_all, u_all, u_all, bias)


ATTN_SPANS = 8


def _attn_prompt(u_all, bias, batch, seq):
    tq = LANES
    nq = seq // tq
    topk = min(TOPK_MAX, seq // 4)
    ntile = nq // ATTN_SPANS
    assert ntile * ATTN_SPANS == nq and ntile >= 2
    spans = [_attn_prompt_span(u_all, bias, batch, seq, e * ntile, ntile, tq, topk).reshape(batch, ntile * tq, D_ATTN)
             for e in range(ATTN_SPANS)]
    return jnp.concatenate(spans, axis=1).reshape(batch * seq, D_ATTN)


def _page_copies(pt_ref, n_pages, rows, srcs_dsts_sems):
    def copy(src, dst, sem, p, pg):
        return pltpu.make_async_copy(src.at[pg], dst.at[pl.ds(pl.multiple_of(p * rows, rows), rows)], sem)

    def start(b, p0):
        def body(p, carry):
            pg = pt_ref[b, p0 + p]
            for src, dst, sem in srcs_dsts_sems:
                copy(src, dst, sem, p, pg).start()
            return carry
        lax.fori_loop(0, n_pages, body, 0)

    def wait():
        def body(p, carry):
            for src, dst, sem in srcs_dsts_sems:
                copy(src, dst, sem, p, 0).wait()
            return carry
        lax.fori_loop(0, n_pages, body, 0)

    return start, wait


def _prefetched(step, n_steps, start_into, wait_for):
    @pl.when(step == 0)
    def _():
        start_into(0, step)

    nxt = step + 1
    for slot in range(2):
        @pl.when((nxt < n_steps) & ((nxt & 1) == slot))
        def _(slot=slot):
            start_into(slot, nxt)

    for slot in range(2):
        @pl.when((step & 1) == slot)
        def _(slot=slot):
            wait_for(slot)


def _sample_scores_kernel(pt_ref, cache_ref, qi_ref, w_ref, kinew_ref, o_ref, ki_buf, sem,
                          *, n_pages, page, past, chunk):
    b = pl.program_id(0)
    total = ki_buf.shape[1]

    def copies(slot):
        return _page_copies(pt_ref, n_pages, page, [(cache_ref, ki_buf.at[slot], sem.at[slot])])

    _prefetched(b, pl.num_programs(0), lambda slot, step: copies(slot)[0](step, 0), lambda slot: copies(slot)[1]())
    ki_all = ki_buf.at[b & 1]
    ki_all[past:past + SUBLANES, :] = kinew_ref[0]
    ki_all[past + SUBLANES:, :] = jnp.zeros((total - past - SUBLANES, IDX_DIM), F32)

    q3 = _split3_queries(qi_ref[0])
    w = w_ref[0] * (IDX_DIM ** -0.5 * IDX_HEADS ** -0.5)
    rows = qi_ref.shape[1]
    for c0 in range(0, total, chunk):
        s = _dot_nt(q3, _split3_keys(ki_all[c0:c0 + chunk, :]))
        s = jnp.maximum(s, 0.0) * w
        o_ref[0, :, c0:c0 + chunk] = jnp.sum(s.reshape(rows // IDX_HEADS, IDX_HEADS, chunk), axis=1)


def _sample_scores(page_table, cache_idx, qi_rep, w_col, ki_new, past):
    bd, n_pages = page_table.shape
    page = cache_idx.shape[1]
    total = past + LANES
    rows = qi_rep.shape[1]
    nrow = rows // IDX_HEADS
    n_chunks = 5
    chunk = total // n_chunks
    assert chunk * n_chunks == total and chunk % LANES == 0
    kern = functools.partial(_sample_scores_kernel, n_pages=n_pages, page=page, past=past, chunk=chunk)
    return pl.pallas_call(
        kern,
        out_shape=jax.ShapeDtypeStruct((bd, nrow, total), F32),
        grid_spec=pltpu.PrefetchScalarGridSpec(
            num_scalar_prefetch=1,
            grid=(bd,),
            in_specs=[
                pl.BlockSpec(memory_space=pl.ANY),
                pl.BlockSpec((1, rows, IDX_DIM), lambda b, pt: (b, 0, 0)),
                pl.BlockSpec((1, rows, 1), lambda b, pt: (b, 0, 0)),
                pl.BlockSpec((1, SUBLANES, IDX_DIM), lambda b, pt: (b, 0, 0)),
            ],
            out_specs=pl.BlockSpec((1, nrow, total), lambda b, pt: (b, 0, 0)),
            scratch_shapes=[
                pltpu.VMEM((2, total, IDX_DIM), F32),
                pltpu.SemaphoreType.DMA((2,)),
            ],
        ),
        compiler_params=_compiler_params(("arbitrary",)),
        name="sample_scores",
    )(page_table, cache_idx, qi_rep, w_col, ki_new)


def _sample_topk_kernel(sc_ref, o_ref, *, past, topk, rows_per_batch, group_rows):
    rows, total = sc_ref.shape
    kpos = lax.broadcasted_iota(I32, (rows, total), 1)
    row = lax.broadcasted_iota(I32, (rows, total), 0)
    qpos = past + ((row & (rows_per_batch - 1)) >> (group_rows.bit_length() - 1))
    keep = _topk_mask(sc_ref[...], kpos <= qpos, kpos, topk)
    o_ref[...] = jnp.where(keep, 1.0, 0.0)


def _sample_topk(scores, past, topk, rows_per_batch):
    n, total = scores.shape
    tr = LANES
    assert n % tr == 0 and tr % rows_per_batch == 0 and rows_per_batch & (rows_per_batch - 1) == 0
    kern = functools.partial(_sample_topk_kernel, past=past, topk=topk, rows_per_batch=rows_per_batch,
                             group_rows=GROUP)
    return pl.pallas_call(
        kern,
        out_shape=jax.ShapeDtypeStruct((n, total), F32),
        grid=(n // tr,),
        in_specs=[pl.BlockSpec((tr, total), lambda i: (i, 0))],
        out_specs=pl.BlockSpec((tr, total), lambda i: (i, 0)),
        compiler_params=_compiler_params(("parallel",)),
        name="sample_topk",
    )(scores)


def _sample_attend_kernel(pt_ref, ck_ref, cv_ref, q_ref, knew_ref, vnew_ref, keep_ref, tail_ref,
                          o_ref, k_buf, v_buf, m_ref, l_ref, acc_ref, sem, *, layer, half_pages, page):
    s = pl.program_id(0)
    b = s >> 1
    half_keys = half_pages * page
    new0, new1 = half_keys * N_KV_HEADS, (half_keys + SUBLANES) * N_KV_HEADS

    def copies(slot):
        return _page_copies(pt_ref, half_pages, page * N_KV_HEADS,
                            [(ck_ref.at[layer], k_buf.at[slot], sem.at[0, slot]),
                             (cv_ref.at[layer], v_buf.at[slot], sem.at[1, slot])])

    _prefetched(s, pl.num_programs(0),
                lambda slot, step: copies(slot)[0](step >> 1, (step & 1) * half_pages),
                lambda slot: copies(slot)[1]())

    def partial(half, n):
        nkeys = half_keys + (LANES if half == 1 else 0)
        kn = k_buf[half, pl.ds(n, nkeys, stride=N_KV_HEADS), :].astype(BF16)
        vn = v_buf[half, pl.ds(n, nkeys, stride=N_KV_HEADS), :].astype(BF16)
        sc = _dot_nt((q_ref[0, n] * (HEAD_DIM ** -0.5 * LOG2E)).astype(BF16), kn)
        if half == 1:
            near = nkeys - BIAS_W
            sc = jnp.concatenate([sc[:, :near], sc[:, near:] + tail_ref[n]], axis=1)
        keep = keep_ref[0, :, half * half_keys:half * half_keys + nkeys] > 0.5
        sc = jnp.where(keep, sc, NEG)
        m = jnp.max(sc, axis=1, keepdims=True)
        p = jnp.exp2(sc - m)
        return m, jnp.sum(p, axis=1, keepdims=True), _dot(p.astype(BF16), vn)

    @pl.when((s & 1) == 0)
    def _():
        for n in range(N_KV_HEADS):
            m, l, acc = partial(0, n)
            m_ref[n] = m
            l_ref[n] = l
            acc_ref[n] = acc

    @pl.when((s & 1) == 1)
    def _():
        pad = jnp.zeros(((half_keys + LANES) * N_KV_HEADS - new1, HEAD_DIM), F32)
        k_buf[1, new0:new1, :] = knew_ref[0]
        v_buf[1, new0:new1, :] = vnew_ref[0]
        k_buf[1, new1:, :] = pad
        v_buf[1, new1:, :] = pad
        for n in range(N_KV_HEADS):
            m1, l1, acc1 = partial(1, n)
            m0 = m_ref[n]
            m = jnp.maximum(m0, m1)
            a0 = jnp.exp2(m0 - m)
            a1 = jnp.exp2(m1 - m)
            o = (acc_ref[n] * a0 + acc1 * a1) / (l_ref[n] * a0 + l1 * a1)
            o_ref[0, n] = o.astype(o_ref.dtype)


def _sample_attend(page_table, cache_k, cache_v, layer, page, q_rows, k_new, v_new, keep, tail, past):
    bd, n_pages = page_table.shape
    total = past + LANES
    nrow = q_rows.shape[2]
    half_pages = n_pages // 2
    assert half_pages * 2 == n_pages and BIAS_W <= LANES + half_pages * page
    buf_rows = (half_pages * page + LANES) * N_KV_HEADS
    kern = functools.partial(_sample_attend_kernel, layer=layer, half_pages=half_pages, page=page)
    per_batch = lambda rank: (lambda s, pt: (s // 2,) + (0,) * (rank - 1))
    return pl.pallas_call(
        kern,
        out_shape=jax.ShapeDtypeStruct((bd, N_KV_HEADS, nrow, HEAD_DIM), BF16),
        grid_spec=pltpu.PrefetchScalarGridSpec(
            num_scalar_prefetch=1,
            grid=(2 * bd,),
            in_specs=[
                pl.BlockSpec(memory_space=pl.ANY),
                pl.BlockSpec(memory_space=pl.ANY),
                pl.BlockSpec((1, N_KV_HEADS, nrow, HEAD_DIM), per_batch(4)),
                pl.BlockSpec((1, SUBLANES * N_KV_HEADS, HEAD_DIM), per_batch(3)),
                pl.BlockSpec((1, SUBLANES * N_KV_HEADS, HEAD_DIM), per_batch(3)),
                pl.BlockSpec((1, nrow, total), per_batch(3)),
                pl.BlockSpec((N_KV_HEADS, nrow, BIAS_W), lambda s, pt: (0, 0, 0)),
            ],
            out_specs=pl.BlockSpec((1, N_KV_HEADS, nrow, HEAD_DIM), per_batch(4)),
            scratch_shapes=[
                pltpu.VMEM((2, buf_rows, HEAD_DIM), F32),
                pltpu.VMEM((2, buf_rows, HEAD_DIM), F32),
                pltpu.VMEM((N_KV_HEADS, nrow, 1), F32),
                pltpu.VMEM((N_KV_HEADS, nrow, 1), F32),
                pltpu.VMEM((N_KV_HEADS, nrow, HEAD_DIM), F32),
                pltpu.SemaphoreType.DMA((2, 2)),
            ],
        ),
        compiler_params=_compiler_params(("arbitrary",)),
        name="sample_attend",
    )(page_table, cache_k, cache_v, q_rows, k_new, v_new, keep, tail)


def _out_proj_kernel(x_ref, pool_ref, attn_ref, wp_ref, wa_ref, o_ref):
    o_ref[...] = x_ref[...] + (_dot(pool_ref[...], wp_ref[...]) + _dot(attn_ref[...], wa_ref[...]))


def _out_proj(x, pool, attn, w_out, tm, tn):
    m, d = x.shape
    return pl.pallas_call(
        _out_proj_kernel,
        out_shape=jax.ShapeDtypeStruct((m, d), F32),
        grid=(m // tm, d // tn),
        in_specs=[
            pl.BlockSpec((tm, tn), lambda i, j: (i, j)),
            pl.BlockSpec((tm, D_POOL), lambda i, j: (i, 0)),
            pl.BlockSpec((tm, D_ATTN), lambda i, j: (i, 0)),
            pl.BlockSpec((D_POOL, tn), lambda i, j: (0, j)),
            pl.BlockSpec((D_ATTN, tn), lambda i, j: (D_POOL // D_ATTN, j)),
        ],
        out_specs=pl.BlockSpec((tm, tn), lambda i, j: (i, j)),
        compiler_params=_compiler_params(("parallel", "arbitrary")),
        name="out_proj",
    )(x, pool, attn, w_out, w_out)


def _proj_gain(g_q, g_k, g_idx_k):
    return jnp.concatenate([
        jnp.ones((D_POOL,), F32), jnp.tile(g_q, N_HEADS), jnp.tile(g_k, N_KV_HEADS),
        jnp.ones((C_KI - C_V,), F32), g_idx_k, jnp.ones((D_IN_PAD - C_WI,), F32)]).reshape(1, D_IN_PAD)


def _pad_rows(a, rows):
    return jnp.pad(a, ((0, 0), (0, rows - a.shape[1]), (0, 0)))


def kernel(x_prompt, x_sample, cache_k, cache_v, cache_idx_k, state_pool, page_table, g_ffn1, w_gate1, w_up1,
           w_down1, g_mix, w_in, g_q, g_k, g_idx_k, w_pool, pool_scale, w_out, rel_bias, g_ffn2, w_gate2,
           w_up2, w_down2):
    batch, seq, d = x_prompt.shape
    bd, t_new, _ = x_sample.shape
    depth = g_ffn1.shape[0]
    assert depth == 1 and d == D_MODEL
    past = page_table.shape[1] * cache_k.shape[2]
    mp, ms = batch * seq, bd * t_new
    tm_p, tm_s = 512, ms
    tf = 256

    xp = x_prompt.reshape(mp, d)
    xs = x_sample.reshape(ms, d)
    l = 0

    xs, wg1, wu1, wd1 = _ffn_cast(xs, g_ffn1[l], w_gate1[l], w_up1[l], w_down1[l], tf)
    xp = _ffn(xp, g_ffn1[l], wg1, wu1, wd1, tm_p)

    w_in_p = jnp.pad(w_in[l], ((0, 0), (0, D_IN_PAD - D_IN))).astype(BF16)
    gain = _proj_gain(g_q[l], g_k[l], g_idx_k[l])
    up = _proj(xp, g_mix[l], w_in_p, gain, tm_p)
    us = _proj(xs, g_mix[l], w_in_p, gain, tm_s)

    wp = w_pool[l].astype(BF16)
    scale = pool_scale[l].reshape(1, D_POOL)
    bias = _bias_tiles(rel_bias)

    pool_p = _pool_prompt(up, wp, scale, batch, seq, 256)
    attn_p = _attn_prompt(up, bias, batch, seq)

    hist = state_pool[l]
    u_pool_s = us[:, :D_POOL].reshape(bd, t_new, D_POOL)
    ext = jnp.concatenate([hist, u_pool_s], axis=1)
    pool_s = _pool_sample(ext.transpose(1, 0, 2), wp, scale, t_new)
    pool_s = pool_s.transpose(1, 0, 2).reshape(ms, D_POOL)

    us3 = us.reshape(bd, t_new, D_IN_PAD)
    rows_i = t_new * GROUP * IDX_HEADS
    qi_s = us3[:, :, C_QI:C_KI].reshape(bd, t_new, 1, IDX_HEADS, IDX_DIM)
    qi_rep = jnp.broadcast_to(qi_s, (bd, t_new, GROUP, IDX_HEADS, IDX_DIM)).reshape(bd, rows_i, IDX_DIM)
    wi_s = us3[:, :, C_WI:C_WI + IDX_HEADS].reshape(bd, t_new, 1, IDX_HEADS)
    w_col = jnp.broadcast_to(wi_s, (bd, t_new, GROUP, IDX_HEADS)).reshape(bd, rows_i, 1)
    ki_new = _pad_rows(us3[:, :, C_KI:C_WI], SUBLANES)
    k_new = _pad_rows(us3[:, :, C_K:C_V], SUBLANES)
    v_new = _pad_rows(us3[:, :, C_V:C_QI], SUBLANES)
    n_phys, page = cache_k.shape[1], cache_k.shape[2]
    topk_s = min(TOPK_MAX, (past + t_new) // 4)
    nrow = t_new * GROUP
    scores_s = _sample_scores(page_table, cache_idx_k[l], qi_rep, w_col, ki_new, past)
    keep_s = _sample_topk(scores_s.reshape(bd * nrow, past + LANES), past, topk_s, nrow)
    keep_s = keep_s.reshape(bd, nrow, past + LANES)

    q_rows = us3[:, :, C_Q:C_K].reshape(bd, t_new, N_KV_HEADS, GROUP, HEAD_DIM)
    q_rows = q_rows.transpose(0, 2, 1, 3, 4).reshape(bd, N_KV_HEADS, nrow, HEAD_DIM)
    tail = bias[:, :t_new, :].reshape(N_KV_HEADS, GROUP, t_new, BIAS_W)
    tail = tail.transpose(0, 2, 1, 3).reshape(N_KV_HEADS, nrow, BIAS_W)
    rows_shape = (depth, n_phys, page * N_KV_HEADS, HEAD_DIM)
    new_shape = (bd, SUBLANES * N_KV_HEADS, HEAD_DIM)
    attn_s = _sample_attend(page_table, cache_k.reshape(rows_shape), cache_v.reshape(rows_shape), l, page, q_rows,
                            k_new.reshape(new_shape), v_new.reshape(new_shape), keep_s, tail, past)
    attn_s = attn_s.reshape(bd, N_KV_HEADS, t_new, GROUP, HEAD_DIM).transpose(0, 2, 1, 3, 4).reshape(ms, D_ATTN)

    w_out_bf = w_out[l].astype(BF16)
    xp = _out_proj(xp, pool_p, attn_p, w_out_bf, tm_p, 1024)
    xs = _out_proj(xs, pool_s, attn_s, w_out_bf, tm_s, 1024)

    xs, wg2, wu2, wd2 = _ffn_cast(xs, g_ffn2[l], w_gate2[l], w_up2[l], w_down2[l], tf)
    xp = _ffn(xp, g_ffn2[l], wg2, wu2, wd2, tm_p)

    up4 = up.reshape(batch, seq, D_IN_PAD)
    return (
        xp.reshape(batch, seq, d),
        xs.reshape(bd, t_new, d),
        up4[:, :, C_K:C_V].reshape(1, batch, seq, N_KV_HEADS, HEAD_DIM),
        up4[:, :, C_V:C_QI].reshape(1, batch, seq, N_KV_HEADS, HEAD_DIM),
        up4[:, :, C_KI:C_WI].reshape(1, batch, seq, IDX_DIM),
        up4[:, seq - POOL_HIST:, :D_POOL].reshape(1, batch, POOL_HIST, D_POOL),
        us3[:, :, C_K:C_V].reshape(1, bd, t_new, N_KV_HEADS, HEAD_DIM),
        us3[:, :, C_V:C_QI].reshape(1, bd, t_new, N_KV_HEADS, HEAD_DIM),
        us3[:, :, C_KI:C_WI].reshape(1, bd, t_new, IDX_DIM),
        ext[:, t_new:].reshape(1, bd, POOL_HIST, D_POOL),
    )
```

```python
import functools
import math

import jax
import jax.numpy as jnp
import numpy as np
from jax import lax
from jax.experimental import pallas as pl
from jax.experimental.pallas import tpu as pltpu

F32 = jnp.float32
BF16 = jnp.bfloat16
I32 = jnp.int32

SUBLANES = 8
LANES = 128
VMEM_LIMIT_BYTES = 56 * 1024 * 1024

D_MODEL = 4096
D_POOL = D_MODEL // 2
POOL_WINDOWS = (2, 4, 8, 16)
N_POOL_GROUPS = len(POOL_WINDOWS)
POOL_GROUP = D_POOL // N_POOL_GROUPS
POOL_HIST = max(POOL_WINDOWS) - 1
HEAD_DIM = 128
N_HEADS = (D_MODEL - D_POOL) // HEAD_DIM
N_KV_HEADS = 4
GROUP = N_HEADS // N_KV_HEADS
D_ATTN = N_HEADS * HEAD_DIM
D_KV = N_KV_HEADS * HEAD_DIM
IDX_HEADS = 8
IDX_DIM = 128
TOPK_MAX = 256
NUM_BUCKETS = 32
MAX_DISTANCE = 128
EPS = 1e-6
NEG = -1e30
LOG2E = math.log2(math.e)

C_Q = D_POOL
C_K = C_Q + D_ATTN
C_V = C_K + D_KV
C_QI = C_V + D_KV
C_KI = C_QI + IDX_HEADS * IDX_DIM
C_WI = C_KI + IDX_DIM
D_IN = C_WI + IDX_HEADS
D_IN_PAD = ((D_IN + 2 * LANES - 1) // (2 * LANES)) * (2 * LANES)
PROJ_TN = 10 * LANES
assert D_IN_PAD % PROJ_TN == 0

INT_MIN = -(2 ** 31)
KEY_NEG_INF = INT_MIN + 0x007FFFFF


def _compiler_params(semantics):
    return pltpu.CompilerParams(dimension_semantics=semantics, vmem_limit_bytes=VMEM_LIMIT_BYTES)


def _dot(a, b):
    return jnp.dot(a, b, preferred_element_type=F32)


def _dot_nt(a, b):
    return lax.dot_general(a, b, (((1,), (1,)), ((), ())), preferred_element_type=F32)


def _rms_scale(x):
    return lax.rsqrt(jnp.mean(x * x, axis=-1, keepdims=True) + EPS)


def _split_bf16(x):
    hi = x.astype(BF16)
    lo = (x - hi.astype(F32)).astype(BF16)
    return hi, lo


def _ffn_step(j, last, x_ref, g_ref, load_weights, o_ref, h_ref):
    @pl.when(j == 0)
    def _():
        x = x_ref[...]
        h_ref[...] = (x * _rms_scale(x) * g_ref[...]).astype(BF16)
        o_ref[...] = jnp.zeros_like(o_ref)

    wg, wu, wd = load_weights()
    h = h_ref[...]
    a = _dot(h, wg)
    b = _dot(h, wu)
    s = (a * jax.nn.sigmoid(a) * b).astype(BF16)
    o_ref[...] += _dot(s, wd)

    @pl.when(j == last)
    def _():
        o_ref[...] = x_ref[...] + 0.5 * o_ref[...]


def _ffn_kernel(x_ref, g_ref, wg_ref, wu_ref, wd_ref, o_ref, h_ref):
    _ffn_step(pl.program_id(1), pl.num_programs(1) - 1, x_ref, g_ref,
              lambda: (wg_ref[0], wu_ref[0], wd_ref[...]), o_ref, h_ref)


def _ffn_cast_kernel(x_ref, g_ref, wg_ref, wu_ref, wd_ref, o_ref, wg_o, wu_o, wd_o, h_ref):
    def load_weights():
        wg = wg_ref[...].astype(BF16)
        wu = wu_ref[...].astype(BF16)
        wd = wd_ref[...].astype(BF16)
        wg_o[0] = wg
        wu_o[0] = wu
        wd_o[...] = wd
        return wg, wu, wd

    _ffn_step(pl.program_id(0), pl.num_programs(0) - 1, x_ref, g_ref, load_weights, o_ref, h_ref)


def _ffn_cast(x, g, wg, wu, wd, tf):
    m, d = x.shape
    f = wg.shape[1]
    w_in = lambda: pl.BlockSpec((d, tf), lambda j: (0, j))
    w_tile = lambda: pl.BlockSpec((1, d, tf), lambda j: (j, 0, 0))
    w_out = lambda: pl.BlockSpec((tf, d), lambda j: (j, 0))
    return pl.pallas_call(
        _ffn_cast_kernel,
        out_shape=[jax.ShapeDtypeStruct((m, d), F32), jax.ShapeDtypeStruct((f // tf, d, tf), BF16),
                   jax.ShapeDtypeStruct((f // tf, d, tf), BF16), jax.ShapeDtypeStruct((f, d), BF16)],
        grid=(f // tf,),
        in_specs=[pl.BlockSpec((m, d), lambda j: (0, 0)), pl.BlockSpec((1, d), lambda j: (0, 0)),
                  w_in(), w_in(), w_out()],
        out_specs=[pl.BlockSpec((m, d), lambda j: (0, 0)), w_tile(), w_tile(), w_out()],
        scratch_shapes=[pltpu.VMEM((m, d), BF16)],
        compiler_params=_compiler_params(("arbitrary",)),
        name="swiglu_half_cast",
    )(x, g.reshape(1, d), wg, wu, wd)


def _ffn(x, g, wg, wu, wd, tm):
    m, d = x.shape
    nf, _, tf = wg.shape
    return pl.pallas_call(
        _ffn_kernel,
        out_shape=jax.ShapeDtypeStruct((m, d), F32),
        grid=(m // tm, nf),
        in_specs=[
            pl.BlockSpec((tm, d), lambda i, j: (i, 0), pipeline_mode=pl.Buffered(1)),
            pl.BlockSpec((1, d), lambda i, j: (0, 0)),
            pl.BlockSpec((1, d, tf), lambda i, j: (j, 0, 0)),
            pl.BlockSpec((1, d, tf), lambda i, j: (j, 0, 0)),
            pl.BlockSpec((tf, d), lambda i, j: (j, 0)),
        ],
        out_specs=pl.BlockSpec((tm, d), lambda i, j: (i, 0)),
        scratch_shapes=[pltpu.VMEM((tm, d), BF16)],
        compiler_params=_compiler_params(("parallel", "arbitrary")),
        name="swiglu_half",
    )(x, g.reshape(1, d), wg, wu, wd)


def _proj_kernel(x_ref, g_ref, w_ref, gain_ref, o_ref, h_ref):
    j = pl.program_id(1)

    @pl.when(j == 0)
    def _():
        x = x_ref[...]
        h_ref[...] = (x * _rms_scale(x) * g_ref[...]).astype(BF16)

    u = _dot(h_ref[...], w_ref[...])
    gain = gain_ref[...]

    def is_normed(col):
        return C_Q <= col < C_V or C_KI <= col < C_WI

    for jt in range(D_IN_PAD // PROJ_TN):
        @pl.when(j == jt)
        def _(jt=jt):
            c = 0
            while c < PROJ_TN:
                if is_normed(jt * PROJ_TN + c):
                    part = u[:, c:c + LANES]
                    o_ref[:, c:c + LANES] = part * _rms_scale(part) * gain[:, c:c + LANES]
                    c += LANES
                else:
                    end = c
                    while end < PROJ_TN and not is_normed(jt * PROJ_TN + end):
                        end += LANES
                    o_ref[:, c:end] = u[:, c:end]
                    c = end


def _proj(x, g_mix, w_in, gain, tm):
    m, d = x.shape
    n = w_in.shape[1]
    return pl.pallas_call(
        _proj_kernel,
        out_shape=jax.ShapeDtypeStruct((m, n), F32),
        grid=(m // tm, n // PROJ_TN),
        in_specs=[
            pl.BlockSpec((tm, d), lambda i, j: (i, 0), pipeline_mode=pl.Buffered(1)),
            pl.BlockSpec((1, d), lambda i, j: (0, 0)),
            pl.BlockSpec((d, PROJ_TN), lambda i, j: (0, j)),
            pl.BlockSpec((1, PROJ_TN), lambda i, j: (0, j)),
        ],
        out_specs=pl.BlockSpec((tm, PROJ_TN), lambda i, j: (i, j)),
        scratch_shapes=[pltpu.VMEM((tm, d), BF16)],
        compiler_params=_compiler_params(("parallel", "arbitrary")),
        name="in_proj",
    )(x, g_mix.reshape(1, d), w_in, gain)


HALO = 2 * SUBLANES


def _pool_groups(ext_rows, cur, count_of, wp_ref, scale_ref, store):
    for g, w in enumerate(POOL_WINDOWS):
        lo, hi = g * POOL_GROUP, (g + 1) * POOL_GROUP
        acc = ext_rows(0, lo, hi)
        for d in range(1, w):
            acc = acc + ext_rows(d, lo, hi)
        diff = acc / count_of(w) - cur(lo, hi)
        out = _dot(diff.astype(BF16), wp_ref[g]) * scale_ref[:, lo:hi]
        store(lo, hi, out)


def _pool_prompt_kernel(u_ref, halo_ref, wp_ref, scale_ref, o_ref, ext_ref, *, tp):
    i = pl.program_id(1)
    halo = halo_ref[...]
    ext_ref[:HALO, :] = jnp.where(i == 0, jnp.zeros_like(halo), halo)
    ext_ref[HALO:, :] = u_ref[...]
    pos = i * tp + lax.broadcasted_iota(I32, (tp, POOL_GROUP), 0)

    def store(lo, hi, out):
        o_ref[:, lo:hi] = out.astype(o_ref.dtype)

    _pool_groups(
        lambda d, lo, hi: ext_ref[HALO - d:HALO - d + tp, lo:hi],
        lambda lo, hi: u_ref[:, lo:hi],
        lambda w: jnp.minimum(pos + 1, w).astype(F32),
        wp_ref, scale_ref, store)


def _pool_prompt(u_all, wp, scale, batch, seq, tp):
    nt = seq // tp
    kern = functools.partial(_pool_prompt_kernel, tp=tp)
    return pl.pallas_call(
        kern,
        out_shape=jax.ShapeDtypeStruct((batch * seq, D_POOL), BF16),
        grid=(batch, nt),
        in_specs=[
            pl.BlockSpec((tp, D_POOL), lambda b, i: (b * nt + i, 0)),
            pl.BlockSpec((HALO, D_POOL),
                         lambda b, i: (jnp.maximum((b * seq + i * tp) // HALO - 1, 0), 0)),
            pl.BlockSpec((N_POOL_GROUPS, POOL_GROUP, POOL_GROUP), lambda b, i: (0, 0, 0)),
            pl.BlockSpec((1, D_POOL), lambda b, i: (0, 0)),
        ],
        out_specs=pl.BlockSpec((tp, D_POOL), lambda b, i: (b * nt + i, 0)),
        scratch_shapes=[pltpu.VMEM((HALO + tp, D_POOL), F32)],
        compiler_params=_compiler_params(("parallel", "arbitrary")),
        name="pool_prompt",
    )(u_all, u_all, wp, scale)


def _pool_sample_kernel(ext_ref, wp_ref, scale_ref, o_ref, *, t_new):
    for t in range(t_new):
        def store(lo, hi, out, t=t):
            o_ref[t, :, lo:hi] = out.astype(o_ref.dtype)

        _pool_groups(
            lambda d, lo, hi, t=t: ext_ref[POOL_HIST + t - d, :, lo:hi],
            lambda lo, hi, t=t: ext_ref[POOL_HIST + t, :, lo:hi],
            lambda w: float(w),
            wp_ref, scale_ref, store)


def _pool_sample(ext_t, wp, scale, t_new):
    _, bd, _ = ext_t.shape
    kern = functools.partial(_pool_sample_kernel, t_new=t_new)
    return pl.pallas_call(
        kern,
        out_shape=jax.ShapeDtypeStruct((t_new, bd, D_POOL), BF16),
        compiler_params=pltpu.CompilerParams(vmem_limit_bytes=VMEM_LIMIT_BYTES),
        name="pool_sample",
    )(ext_t, wp, scale)


def _bucket_thresholds():
    n = np.arange(0, 4 * MAX_DISTANCE, dtype=np.int64)
    max_exact = NUM_BUCKETS // 2
    nf = np.maximum(n, 1).astype(np.float32)
    large = max_exact + (np.log(nf / np.float32(max_exact)) / np.float32(math.log(MAX_DISTANCE / max_exact))
                         * np.float32(NUM_BUCKETS - max_exact)).astype(np.int32)
    large = np.minimum(large, NUM_BUCKETS - 1)
    bucket = np.where(n < max_exact, n, large)
    assert np.all(np.diff(bucket) >= 0) and bucket[-1] == NUM_BUCKETS - 1
    return [int(np.argmax(bucket >= b)) for b in range(1, NUM_BUCKETS)]


BUCKET_THRESHOLDS = _bucket_thresholds()
FAR_DISTANCE = BUCKET_THRESHOLDS[-1]
BIAS_W = 2 * LANES
assert FAR_DISTANCE <= LANES


def _bias_kernel(rb_ref, o_ref):
    r = lax.broadcasted_iota(I32, (LANES, BIAS_W), 0)
    c = lax.broadcasted_iota(I32, (LANES, BIAS_W), 1)
    d = LANES + r - c
    bucket = jnp.zeros((LANES, BIAS_W), I32)
    for thr in BUCKET_THRESHOLDS:
        bucket = bucket + (d >= thr).astype(I32)
    for h in range(N_HEADS):
        far = rb_ref[NUM_BUCKETS - 1, h]
        val = jnp.zeros((LANES, BIAS_W), F32)
        for b in range(NUM_BUCKETS - 1):
            val = jnp.where(bucket == b, rb_ref[b, h] - far, val)
        o_ref[h] = jnp.where(d >= 0, val * LOG2E, 0.0)


def _bias_tiles(rel_bias):
    return pl.pallas_call(
        _bias_kernel,
        out_shape=jax.ShapeDtypeStruct((N_HEADS, LANES, BIAS_W), F32),
        in_specs=[pl.BlockSpec(memory_space=pltpu.SMEM)],
        name="rel_bias_tiles",
    )(rel_bias)


RADIX4_MAX_ELEMENTS = 192 * SUBLANES * LANES


def _decode_key(key):
    bits = jnp.where(key < 0, key ^ 0x7FFFFFFF, key)
    return lax.bitcast_convert_type(bits, F32)


def _count(mask):
    return jnp.sum(jnp.where(mask, 1.0, 0.0), axis=1, keepdims=True)


def _topk_mask(score, admissible, kpos, need):
    rows, n = score.shape
    score = jnp.where(admissible, score, -jnp.inf)
    need_f = float(need)

    nonneg = _count(score >= 0.0) >= need_f
    lo0 = jnp.where(nonneg, 0, INT_MIN).astype(I32)

    def kept(cand):
        return _count(score >= _decode_key(cand)) >= need_f

    def bit_step(it, lo):
        cand = lo | (jnp.int32(1) << (30 - it))
        return jnp.where(kept(cand), cand, lo)

    def pair_step(it, lo):
        shift = 28 - 2 * it
        c1, c2, c3 = (lo | (jnp.int32(m) << shift) for m in (1, 2, 3))
        return jnp.where(kept(c3), c3, jnp.where(kept(c2), c2, jnp.where(kept(c1), c1, lo)))

    if rows * n <= RADIX4_MAX_ELEMENTS:
        lo = lax.fori_loop(0, 15, pair_step, bit_step(0, lo0))
    else:
        lo = lax.fori_loop(0, 31, bit_step, lo0)
    thr = jnp.where(lo <= KEY_NEG_INF, -jnp.inf, _decode_key(lo))

    gt = score > thr
    eq = score == thr
    n_gt = _count(gt)
    spare = need_f - n_gt
    nbits = int(n).bit_length()

    def trim_ties():
        def idx_step(it, lim):
            cand = lim + (jnp.int32(1) << (nbits - 1 - it))
            ok = _count(eq & (kpos < cand)) <= spare
            return jnp.where(ok, cand, lim)
        return lax.fori_loop(0, nbits, idx_step, jnp.zeros((rows, 1), I32))

    overshoot = jnp.max(n_gt + _count(eq)) > need_f
    lim = lax.cond(overshoot, trim_ties, lambda: jnp.full((rows, 1), 2 ** nbits, I32))
    return (gt | (eq & (kpos < lim))) & admissible


def _split3_keys(x):
    hi, lo = _split_bf16(x)
    return jnp.concatenate([hi, lo, hi], axis=1)


def _split3_queries(x):
    hi, lo = _split_bf16(x)
    return jnp.concatenate([hi, hi, lo], axis=1)


def _attn_prompt_body(i, ext, first, q_ref, qi_ref, wi_ref, bias_ref, o_ref, k_bf, v_bf, ki3, s_ref, tq, topk):
    def stack_heads(ref, h0, nh, width):
        return jnp.concatenate([ref[:, (h0 + h) * width:(h0 + h + 1) * width] for h in range(nh)], axis=0)

    wi = wi_ref[...] * (IDX_DIM ** -0.5 * IDX_HEADS ** -0.5)
    s_idx = _dot_nt(_split3_queries(stack_heads(qi_ref, 0, IDX_HEADS, IDX_DIM)), ki3[:ext, :])
    score = jnp.zeros((tq, ext), F32)
    for h in range(IDX_HEADS):
        score = score + jnp.maximum(s_idx[h * tq:(h + 1) * tq, :], 0.0) * wi[:, h:h + 1]

    kpos = lax.broadcasted_iota(I32, (tq, ext), 1)
    qpos = i * tq + lax.broadcasted_iota(I32, (tq, ext), 0)
    drop = jnp.where(_topk_mask(score, kpos <= qpos, kpos, topk), 0.0, NEG)

    for n in range(N_KV_HEADS):
        kn = k_bf[:ext, n * HEAD_DIM:(n + 1) * HEAD_DIM]
        vn = v_bf[:ext, n * HEAD_DIM:(n + 1) * HEAD_DIM]
        s_n = s_ref.at[n % 2]
        qn = (stack_heads(q_ref, n * GROUP, GROUP, HEAD_DIM) * (HEAD_DIM ** -0.5 * LOG2E)).astype(BF16)
        s_n[...] = _dot_nt(qn, kn)
        for g in range(GROUP):
            h = n * GROUP + g
            rows = slice(g * tq, (g + 1) * tq)

            def add_near_bias(h=h, rows=rows):
                ws = pl.multiple_of((i - 1) * tq, LANES)
                s_n[rows, pl.ds(ws, BIAS_W)] += bias_ref[h]

            if first:
                @pl.when(i == 0)
                def _(h=h, rows=rows):
                    s_n[rows, :tq] += bias_ref[h, :, LANES:]

                pl.when(i > 0)(add_near_bias)
            else:
                add_near_bias()

        s = (s_n[...].reshape(GROUP, tq, ext) + drop[None]).reshape(GROUP * tq, ext)
        p = jnp.exp2(s - jnp.max(s, axis=1, keepdims=True))
        l = jnp.sum(p, axis=1, keepdims=True)
        o = _dot(p.astype(BF16), vn) / l
        for g in range(GROUP):
            h = n * GROUP + g
            o_ref[:, h * HEAD_DIM:(h + 1) * HEAD_DIM] = o[g * tq:(g + 1) * tq, :].astype(o_ref.dtype)


def _key_prep_kernel(k_ref, v_ref, ki_ref, k_o, v_o, ki3_o):
    k_o[...] = k_ref[...].astype(BF16)
    v_o[...] = v_ref[...].astype(BF16)
    ki3_o[...] = _split3_keys(ki_ref[...])


def _key_prep(u_all, rows, tr):
    return pl.pallas_call(
        _key_prep_kernel,
        out_shape=[jax.ShapeDtypeStruct((rows, D_KV), BF16), jax.ShapeDtypeStruct((rows, D_KV), BF16),
                   jax.ShapeDtypeStruct((rows, 3 * IDX_DIM), BF16)],
        grid=(rows // tr,),
        in_specs=[
            pl.BlockSpec((tr, D_KV), lambda i: (i, C_K // D_KV)),
            pl.BlockSpec((tr, D_KV), lambda i: (i, C_V // D_KV)),
            pl.BlockSpec((tr, IDX_DIM), lambda i: (i, C_KI // IDX_DIM)),
        ],
        out_specs=[pl.BlockSpec((tr, D_KV), lambda i: (i, 0)), pl.BlockSpec((tr, D_KV), lambda i: (i, 0)),
                   pl.BlockSpec((tr, 3 * IDX_DIM), lambda i: (i, 0))],
        compiler_params=_compiler_params(("parallel",)),
        name="key_prep",
    )(u_all, u_all, u_all)


def _attn_prompt_kernel(q_ref, k_ref, v_ref, qi_ref, ki3_ref, wi_ref, bias_ref, o_ref, s_ref,
                        *, tq, topk, ext, tile0):
    _attn_prompt_body(tile0 + pl.program_id(1), ext, tile0 == 0, q_ref, qi_ref, wi_ref, bias_ref, o_ref,
                      k_ref, v_ref, ki3_ref, s_ref, tq, topk)


def _attn_prompt_span(u_all, keys, bias, batch, seq, tile0, ntile, tq, topk):
    nq = seq // tq
    ext = (tile0 + ntile) * tq
    kern = functools.partial(_attn_prompt_kernel, tq=tq, topk=topk, ext=ext, tile0=tile0)
    row = lambda b, j: b * nq + tile0 + j
    k_bf, v_bf, ki3 = keys
    return pl.pallas_call(
        kern,
        out_shape=jax.ShapeDtypeStruct((batch * ntile * tq, D_ATTN), BF16),
        grid=(batch, ntile),
        in_specs=[
            pl.BlockSpec((tq, D_ATTN), lambda b, j: (row(b, j), C_Q // D_ATTN)),
            pl.BlockSpec((seq, D_KV), lambda b, j: (b, 0)),
            pl.BlockSpec((seq, D_KV), lambda b, j: (b, 0)),
            pl.BlockSpec((tq, IDX_HEADS * IDX_DIM), lambda b, j: (row(b, j), C_QI // (IDX_HEADS * IDX_DIM))),
            pl.BlockSpec((seq, 3 * IDX_DIM), lambda b, j: (b, 0)),
            pl.BlockSpec((tq, LANES), lambda b, j: (row(b, j), C_WI // LANES)),
            pl.BlockSpec((N_HEADS, LANES, BIAS_W), lambda b, j: (0, 0, 0)),
        ],
        out_specs=pl.BlockSpec((tq, D_ATTN), lambda b, j: (b * ntile + j, 0)),
        scratch_shapes=[
            pltpu.VMEM((2, GROUP * tq, ext), F32),
        ],
        compiler_params=_compiler_params(("parallel", "parallel")),
        name=f"attn_prompt_{ext}",
    )(u_all, k_bf, v_bf, u_all, ki3, u_all, bias)


ATTN_SPANS = 8


def _attn_prompt(u_all, bias, batch, seq):
    tq = LANES
    nq = seq // tq
    topk = min(TOPK_MAX, seq // 4)
    ntile = nq // ATTN_SPANS
    assert ntile * ATTN_SPANS == nq and ntile >= 2
    keys = _key_prep(u_all, batch * seq, seq)
    spans = [_attn_prompt_span(u_all, keys, bias, batch, seq, e * ntile, ntile, tq, topk)
             .reshape(batch, ntile * tq, D_ATTN) for e in range(ATTN_SPANS)]
    return jnp.concatenate(spans, axis=1).reshape(batch * seq, D_ATTN)


def _page_copies(pt_ref, n_pages, rows, srcs_dsts_sems):
    def copy(src, dst, sem, p, pg):
        return pltpu.make_async_copy(src.at[pg], dst.at[pl.ds(pl.multiple_of(p * rows, rows), rows)], sem)

    def start(b, p0):
        def body(p, carry):
            pg = pt_ref[b, p0 + p]
            for src, dst, sem in srcs_dsts_sems:
                copy(src, dst, sem, p, pg).start()
            return carry
        lax.fori_loop(0, n_pages, body, 0)

    def wait():
        def body(p, carry):
            for src, dst, sem in srcs_dsts_sems:
                copy(src, dst, sem, p, 0).wait()
            return carry
        lax.fori_loop(0, n_pages, body, 0)

    return start, wait


def _prefetched(step, n_steps, start_into, wait_for):
    @pl.when(step == 0)
    def _():
        start_into(0, step)

    nxt = step + 1
    for slot in range(2):
        @pl.when((nxt < n_steps) & ((nxt & 1) == slot))
        def _(slot=slot):
            start_into(slot, nxt)

    for slot in range(2):
        @pl.when((step & 1) == slot)
        def _(slot=slot):
            wait_for(slot)


def _sample_scores_kernel(pt_ref, cache_ref, qi_ref, w_ref, kinew_ref, o_ref, ki_buf, sem,
                          *, n_pages, page, past, chunk):
    b = pl.program_id(0)
    total = ki_buf.shape[1]

    def copies(slot):
        return _page_copies(pt_ref, n_pages, page, [(cache_ref, ki_buf.at[slot], sem.at[slot])])

    _prefetched(b, pl.num_programs(0), lambda slot, step: copies(slot)[0](step, 0), lambda slot: copies(slot)[1]())
    ki_all = ki_buf.at[b & 1]
    ki_all[past:past + SUBLANES, :] = kinew_ref[0]
    ki_all[past + SUBLANES:, :] = jnp.zeros((total - past - SUBLANES, IDX_DIM), F32)

    q3 = _split3_queries(qi_ref[0])
    w = w_ref[0] * (IDX_DIM ** -0.5 * IDX_HEADS ** -0.5)
    rows = qi_ref.shape[1]
    for c0 in range(0, total, chunk):
        s = _dot_nt(q3, _split3_keys(ki_all[c0:c0 + chunk, :]))
        s = jnp.maximum(s, 0.0) * w
        o_ref[0, :, c0:c0 + chunk] = jnp.sum(s.reshape(rows // IDX_HEADS, IDX_HEADS, chunk), axis=1)


def _sample_scores(page_table, cache_idx, qi_rep, w_col, ki_new, past):
    bd, n_pages = page_table.shape
    page = cache_idx.shape[1]
    total = past + LANES
    rows = qi_rep.shape[1]
    nrow = rows // IDX_HEADS
    n_chunks = 5
    chunk = total // n_chunks
    assert chunk * n_chunks == total and chunk % LANES == 0
    kern = functools.partial(_sample_scores_kernel, n_pages=n_pages, page=page, past=past, chunk=chunk)
    return pl.pallas_call(
        kern,
        out_shape=jax.ShapeDtypeStruct((bd, nrow, total), F32),
        grid_spec=pltpu.PrefetchScalarGridSpec(
            num_scalar_prefetch=1,
            grid=(bd,),
            in_specs=[
                pl.BlockSpec(memory_space=pl.ANY),
                pl.BlockSpec((1, rows, IDX_DIM), lambda b, pt: (b, 0, 0)),
                pl.BlockSpec((1, rows, 1), lambda b, pt: (b, 0, 0)),
                pl.BlockSpec((1, SUBLANES, IDX_DIM), lambda b, pt: (b, 0, 0)),
            ],
            out_specs=pl.BlockSpec((1, nrow, total), lambda b, pt: (b, 0, 0)),
            scratch_shapes=[
                pltpu.VMEM((2, total, IDX_DIM), F32),
                pltpu.SemaphoreType.DMA((2,)),
            ],
        ),
        compiler_params=_compiler_params(("arbitrary",)),
        name="sample_scores",
    )(page_table, cache_idx, qi_rep, w_col, ki_new)


def _sample_topk_kernel(sc_ref, o_ref, *, past, topk, rows_per_batch, group_rows):
    rows, total = sc_ref.shape
    kpos = lax.broadcasted_iota(I32, (rows, total), 1)
    row = lax.broadcasted_iota(I32, (rows, total), 0)
    qpos = past + ((row & (rows_per_batch - 1)) >> (group_rows.bit_length() - 1))
    keep = _topk_mask(sc_ref[...], kpos <= qpos, kpos, topk)
    o_ref[...] = jnp.where(keep, 1.0, 0.0)


def _sample_topk(scores, past, topk, rows_per_batch):
    n, total = scores.shape
    tr = LANES
    assert n % tr == 0 and tr % rows_per_batch == 0 and rows_per_batch & (rows_per_batch - 1) == 0
    kern = functools.partial(_sample_topk_kernel, past=past, topk=topk, rows_per_batch=rows_per_batch,
                             group_rows=GROUP)
    return pl.pallas_call(
        kern,
        out_shape=jax.ShapeDtypeStruct((n, total), F32),
        grid=(n // tr,),
        in_specs=[pl.BlockSpec((tr, total), lambda i: (i, 0))],
        out_specs=pl.BlockSpec((tr, total), lambda i: (i, 0)),
        compiler_params=_compiler_params(("parallel",)),
        name="sample_topk",
    )(scores)


def _sample_attend_kernel(pt_ref, ck_ref, cv_ref, q_ref, knew_ref, vnew_ref, keep_ref, tail_ref,
                          o_ref, k_buf, v_buf, m_ref, l_ref, acc_ref, sem, *, layer, half_pages, page):
    s = pl.program_id(0)
    b = s >> 1
    half_keys = half_pages * page
    new0, new1 = half_keys * N_KV_HEADS, (half_keys + SUBLANES) * N_KV_HEADS

    def copies(slot):
        return _page_copies(pt_ref, half_pages, page * N_KV_HEADS,
                            [(ck_ref.at[layer], k_buf.at[slot], sem.at[0, slot]),
                             (cv_ref.at[layer], v_buf.at[slot], sem.at[1, slot])])

    _prefetched(s, pl.num_programs(0),
                lambda slot, step: copies(slot)[0](step >> 1, (step & 1) * half_pages),
                lambda slot: copies(slot)[1]())

    def partial(half, n):
        nkeys = half_keys + (LANES if half == 1 else 0)
        kn = k_buf[half, pl.ds(n, nkeys, stride=N_KV_HEADS), :].astype(BF16)
        vn = v_buf[half, pl.ds(n, nkeys, stride=N_KV_HEADS), :].astype(BF16)
        sc = _dot_nt((q_ref[0, n] * (HEAD_DIM ** -0.5 * LOG2E)).astype(BF16), kn)
        if half == 1:
            near = nkeys - BIAS_W
            sc = jnp.concatenate([sc[:, :near], sc[:, near:] + tail_ref[n]], axis=1)
        keep = keep_ref[0, :, half * half_keys:half * half_keys + nkeys] > 0.5
        sc = jnp.where(keep, sc, NEG)
        m = jnp.max(sc, axis=1, keepdims=True)
        p = jnp.exp2(sc - m)
        return m, jnp.sum(p, axis=1, keepdims=True), _dot(p.astype(BF16), vn)

    @pl.when((s & 1) == 0)
    def _():
        for n in range(N_KV_HEADS):
            m, l, acc = partial(0, n)
            m_ref[n] = m
            l_ref[n] = l
            acc_ref[n] = acc

    @pl.when((s & 1) == 1)
    def _():
        pad = jnp.zeros(((half_keys + LANES) * N_KV_HEADS - new1, HEAD_DIM), F32)
        k_buf[1, new0:new1, :] = knew_ref[0]
        v_buf[1, new0:new1, :] = vnew_ref[0]
        k_buf[1, new1:, :] = pad
        v_buf[1, new1:, :] = pad
        for n in range(N_KV_HEADS):
            m1, l1, acc1 = partial(1, n)
            m0 = m_ref[n]
            m = jnp.maximum(m0, m1)
            a0 = jnp.exp2(m0 - m)
            a1 = jnp.exp2(m1 - m)
            o = (acc_ref[n] * a0 + acc1 * a1) / (l_ref[n] * a0 + l1 * a1)
            o_ref[0, n] = o.astype(o_ref.dtype)


def _sample_attend(page_table, cache_k, cache_v, layer, page, q_rows, k_new, v_new, keep, tail, past):
    bd, n_pages = page_table.shape
    total = past + LANES
    nrow = q_rows.shape[2]
    half_pages = n_pages // 2
    assert half_pages * 2 == n_pages and BIAS_W <= LANES + half_pages * page
    buf_rows = (half_pages * page + LANES) * N_KV_HEADS
    kern = functools.partial(_sample_attend_kernel, layer=layer, half_pages=half_pages, page=page)
    per_batch = lambda rank: (lambda s, pt: (s // 2,) + (0,) * (rank - 1))
    return pl.pallas_call(
        kern,
        out_shape=jax.ShapeDtypeStruct((bd, N_KV_HEADS, nrow, HEAD_DIM), BF16),
        grid_spec=pltpu.PrefetchScalarGridSpec(
            num_scalar_prefetch=1,
            grid=(2 * bd,),
            in_specs=[
                pl.BlockSpec(memory_space=pl.ANY),
                pl.BlockSpec(memory_space=pl.ANY),
                pl.BlockSpec((1, N_KV_HEADS, nrow, HEAD_DIM), per_batch(4)),
                pl.BlockSpec((1, SUBLANES * N_KV_HEADS, HEAD_DIM), per_batch(3)),
                pl.BlockSpec((1, SUBLANES * N_KV_HEADS, HEAD_DIM), per_batch(3)),
                pl.BlockSpec((1, nrow, total), per_batch(3)),
                pl.BlockSpec((N_KV_HEADS, nrow, BIAS_W), lambda s, pt: (0, 0, 0)),
            ],
            out_specs=pl.BlockSpec((1, N_KV_HEADS, nrow, HEAD_DIM), per_batch(4)),
            scratch_shapes=[
                pltpu.VMEM((2, buf_rows, HEAD_DIM), F32),
                pltpu.VMEM((2, buf_rows, HEAD_DIM), F32),
                pltpu.VMEM((N_KV_HEADS, nrow, 1), F32),
                pltpu.VMEM((N_KV_HEADS, nrow, 1), F32),
                pltpu.VMEM((N_KV_HEADS, nrow, HEAD_DIM), F32),
                pltpu.SemaphoreType.DMA((2, 2)),
            ],
        ),
        compiler_params=_compiler_params(("arbitrary",)),
        name="sample_attend",
    )(page_table, cache_k, cache_v, q_rows, k_new, v_new, keep, tail)


def _out_proj_kernel(x_ref, pool_ref, attn_ref, wp_ref, wa_ref, o_ref):
    o_ref[...] = x_ref[...] + (_dot(pool_ref[...], wp_ref[...]) + _dot(attn_ref[...], wa_ref[...]))


def _out_proj(x, pool, attn, w_out, tm, tn):
    m, d = x.shape
    return pl.pallas_call(
        _out_proj_kernel,
        out_shape=jax.ShapeDtypeStruct((m, d), F32),
        grid=(m // tm, d // tn),
        in_specs=[
            pl.BlockSpec((tm, tn), lambda i, j: (i, j)),
            pl.BlockSpec((tm, D_POOL), lambda i, j: (i, 0)),
            pl.BlockSpec((tm, D_ATTN), lambda i, j: (i, 0)),
            pl.BlockSpec((D_POOL, tn), lambda i, j: (0, j)),
            pl.BlockSpec((D_ATTN, tn), lambda i, j: (D_POOL // D_ATTN, j)),
        ],
        out_specs=pl.BlockSpec((tm, tn), lambda i, j: (i, j)),
        compiler_params=_compiler_params(("parallel", "arbitrary")),
        name="out_proj",
    )(x, pool, attn, w_out, w_out)


def _proj_gain(g_q, g_k, g_idx_k):
    return jnp.concatenate([
        jnp.ones((D_POOL,), F32), jnp.tile(g_q, N_HEADS), jnp.tile(g_k, N_KV_HEADS),
        jnp.ones((C_KI - C_V,), F32), g_idx_k, jnp.ones((D_IN_PAD - C_WI,), F32)]).reshape(1, D_IN_PAD)


def _pad_rows(a, rows):
    return jnp.pad(a, ((0, 0), (0, rows - a.shape[1]), (0, 0)))


def kernel(x_prompt, x_sample, cache_k, cache_v, cache_idx_k, state_pool, page_table, g_ffn1, w_gate1, w_up1,
           w_down1, g_mix, w_in, g_q, g_k, g_idx_k, w_pool, pool_scale, w_out, rel_bias, g_ffn2, w_gate2,
           w_up2, w_down2):
    batch, seq, d = x_prompt.shape
    bd, t_new, _ = x_sample.shape
    depth = g_ffn1.shape[0]
    assert depth == 1 and d == D_MODEL
    past = page_table.shape[1] * cache_k.shape[2]
    mp, ms = batch * seq, bd * t_new
    tm_p, tm_s = 512, ms
    tf = 256

    xp = x_prompt.reshape(mp, d)
    xs = x_sample.reshape(ms, d)
    l = 0

    xs, wg1, wu1, wd1 = _ffn_cast(xs, g_ffn1[l], w_gate1[l], w_up1[l], w_down1[l], tf)
    xp = _ffn(xp, g_ffn1[l], wg1, wu1, wd1, tm_p)

    w_in_p = jnp.pad(w_in[l], ((0, 0), (0, D_IN_PAD - D_IN))).astype(BF16)
    gain = _proj_gain(g_q[l], g_k[l], g_idx_k[l])
    up = _proj(xp, g_mix[l], w_in_p, gain, tm_p)
    us = _proj(xs, g_mix[l], w_in_p, gain, tm_s)

    wp = w_pool[l].astype(BF16)
    scale = pool_scale[l].reshape(1, D_POOL)
    bias = _bias_tiles(rel_bias)

    pool_p = _pool_prompt(up, wp, scale, batch, seq, 256)
    attn_p = _attn_prompt(up, bias, batch, seq)

    hist = state_pool[l]
    u_pool_s = us[:, :D_POOL].reshape(bd, t_new, D_POOL)
    ext = jnp.concatenate([hist, u_pool_s], axis=1)
    pool_s = _pool_sample(ext.transpose(1, 0, 2), wp, scale, t_new)
    pool_s = pool_s.transpose(1, 0, 2).reshape(ms, D_POOL)

    us3 = us.reshape(bd, t_new, D_IN_PAD)
    rows_i = t_new * GROUP * IDX_HEADS
    qi_s = us3[:, :, C_QI:C_KI].reshape(bd, t_new, 1, IDX_HEADS, IDX_DIM)
    qi_rep = jnp.broadcast_to(qi_s, (bd, t_new, GROUP, IDX_HEADS, IDX_DIM)).reshape(bd, rows_i, IDX_DIM)
    wi_s = us3[:, :, C_WI:C_WI + IDX_HEADS].reshape(bd, t_new, 1, IDX_HEADS)
    w_col = jnp.broadcast_to(wi_s, (bd, t_new, GROUP, IDX_HEADS)).reshape(bd, rows_i, 1)
    ki_new = _pad_rows(us3[:, :, C_KI:C_WI], SUBLANES)
    k_new = _pad_rows(us3[:, :, C_K:C_V], SUBLANES)
    v_new = _pad_rows(us3[:, :, C_V:C_QI], SUBLANES)
    n_phys, page = cache_k.shape[1], cache_k.shape[2]
    topk_s = min(TOPK_MAX, (past + t_new) // 4)
    nrow = t_new * GROUP
    scores_s = _sample_scores(page_table, cache_idx_k[l], qi_rep, w_col, ki_new, past)
    keep_s = _sample_topk(scores_s.reshape(bd * nrow, past + LANES), past, topk_s, nrow)
    keep_s = keep_s.reshape(bd, nrow, past + LANES)

    q_rows = us3[:, :, C_Q:C_K].reshape(bd, t_new, N_KV_HEADS, GROUP, HEAD_DIM)
    q_rows = q_rows.transpose(0, 2, 1, 3, 4).reshape(bd, N_KV_HEADS, nrow, HEAD_DIM)
    tail = bias[:, :t_new, :].reshape(N_KV_HEADS, GROUP, t_new, BIAS_W)
    tail = tail.transpose(0, 2, 1, 3).reshape(N_KV_HEADS, nrow, BIAS_W)
    rows_shape = (depth, n_phys, page * N_KV_HEADS, HEAD_DIM)
    new_shape = (bd, SUBLANES * N_KV_HEADS, HEAD_DIM)
    attn_s = _sample_attend(page_table, cache_k.reshape(rows_shape), cache_v.reshape(rows_shape), l, page, q_rows,
                            k_new.reshape(new_shape), v_new.reshape(new_shape), keep_s, tail, past)
    attn_s = attn_s.reshape(bd, N_KV_HEADS, t_new, GROUP, HEAD_DIM).transpose(0, 2, 1, 3, 4).reshape(ms, D_ATTN)

    w_out_bf = w_out[l].astype(BF16)
    xp = _out_proj(xp, pool_p, attn_p, w_out_bf, tm_p, 1024)
    xs = _out_proj(xs, pool_s, attn_s, w_out_bf, tm_s, 1024)

    xs, wg2, wu2, wd2 = _ffn_cast(xs, g_ffn2[l], w_gate2[l], w_up2[l], w_down2[l], tf)
    xp = _ffn(xp, g_ffn2[l], wg2, wu2, wd2, tm_p)

    up4 = up.reshape(batch, seq, D_IN_PAD)
    return (
        xp.reshape(batch, seq, d),
        xs.reshape(bd, t_new, d),
        up4[:, :, C_K:C_V].reshape(1, batch, seq, N_KV_HEADS, HEAD_DIM),
        up4[:, :, C_V:C_QI].reshape(1, batch, seq, N_KV_HEADS, HEAD_DIM),
        up4[:, :, C_KI:C_WI].reshape(1, batch, seq, IDX_DIM),
        up4[:, seq - POOL_HIST:, :D_POOL].reshape(1, batch, POOL_HIST, D_POOL),
        us3[:, :, C_K:C_V].reshape(1, bd, t_new, N_KV_HEADS, HEAD_DIM),
        us3[:, :, C_V:C_QI].reshape(1, bd, t_new, N_KV_HEADS, HEAD_DIM),
        us3[:, :, C_KI:C_WI].reshape(1, bd, t_new, IDX_DIM),
        ext[:, t_new:].reshape(1, bd, POOL_HIST, D_POOL),
    )
```

```python
import functools
import math

import jax
import jax.numpy as jnp
import numpy as np
from jax import lax
from jax.experimental import pallas as pl
from jax.experimental.pallas import tpu as pltpu

F32 = jnp.float32
BF16 = jnp.bfloat16
I32 = jnp.int32

SUBLANES = 8
LANES = 128
VMEM_LIMIT_BYTES = 56 * 1024 * 1024

D_MODEL = 4096
D_POOL = D_MODEL // 2
POOL_WINDOWS = (2, 4, 8, 16)
N_POOL_GROUPS = len(POOL_WINDOWS)
POOL_GROUP = D_POOL // N_POOL_GROUPS
POOL_HIST = max(POOL_WINDOWS) - 1
HEAD_DIM = 128
N_HEADS = (D_MODEL - D_POOL) // HEAD_DIM
N_KV_HEADS = 4
GROUP = N_HEADS // N_KV_HEADS
D_ATTN = N_HEADS * HEAD_DIM
D_KV = N_KV_HEADS * HEAD_DIM
IDX_HEADS = 8
IDX_DIM = 128
TOPK_MAX = 256
NUM_BUCKETS = 32
MAX_DISTANCE = 128
EPS = 1e-6
NEG = -1e30
LOG2E = math.log2(math.e)

C_Q = D_POOL
C_K = C_Q + D_ATTN
C_V = C_K + D_KV
C_QI = C_V + D_KV
C_KI = C_QI + IDX_HEADS * IDX_DIM
C_WI = C_KI + IDX_DIM
D_IN = C_WI + IDX_HEADS
D_IN_PAD = ((D_IN + 2 * LANES - 1) // (2 * LANES)) * (2 * LANES)
PROJ_TN = 10 * LANES
assert D_IN_PAD % PROJ_TN == 0

INT_MIN = -(2 ** 31)
KEY_NEG_INF = INT_MIN + 0x007FFFFF


def _compiler_params(semantics):
    return pltpu.CompilerParams(dimension_semantics=semantics, vmem_limit_bytes=VMEM_LIMIT_BYTES)


def _dot(a, b):
    return jnp.dot(a, b, preferred_element_type=F32)


def _dot_nt(a, b):
    return lax.dot_general(a, b, (((1,), (1,)), ((), ())), preferred_element_type=F32)


def _rms_scale(x):
    return lax.rsqrt(jnp.mean(x * x, axis=-1, keepdims=True) + EPS)


def _split_bf16(x):
    hi = x.astype(BF16)
    lo = (x - hi.astype(F32)).astype(BF16)
    return hi, lo


def _ffn_step(j, last, x_ref, g_ref, load_weights, o_ref, h_ref):
    @pl.when(j == 0)
    def _():
        x = x_ref[...]
        h_ref[...] = (x * _rms_scale(x) * g_ref[...]).astype(BF16)
        o_ref[...] = jnp.zeros_like(o_ref)

    wg, wu, wd = load_weights()
    h = h_ref[...]
    a = _dot(h, wg)
    b = _dot(h, wu)
    s = (a * jax.nn.sigmoid(a) * b).astype(BF16)
    o_ref[...] += _dot(s, wd)

    @pl.when(j == last)
    def _():
        o_ref[...] = x_ref[...] + 0.5 * o_ref[...]


def _ffn_kernel(x_ref, g_ref, wg_ref, wu_ref, wd_ref, o_ref, h_ref):
    _ffn_step(pl.program_id(1), pl.num_programs(1) - 1, x_ref, g_ref,
              lambda: (wg_ref[0], wu_ref[0], wd_ref[...]), o_ref, h_ref)


def _ffn_cast_kernel(x_ref, g_ref, wg_ref, wu_ref, wd_ref, o_ref, wg_o, wu_o, wd_o, h_ref):
    def load_weights():
        wg = wg_ref[...].astype(BF16)
        wu = wu_ref[...].astype(BF16)
        wd = wd_ref[...].astype(BF16)
        wg_o[0] = wg
        wu_o[0] = wu
        wd_o[...] = wd
        return wg, wu, wd

    _ffn_step(pl.program_id(0), pl.num_programs(0) - 1, x_ref, g_ref, load_weights, o_ref, h_ref)


def _ffn_cast(x, g, wg, wu, wd, tf):
    m, d = x.shape
    f = wg.shape[1]
    w_in = lambda: pl.BlockSpec((d, tf), lambda j: (0, j))
    w_tile = lambda: pl.BlockSpec((1, d, tf), lambda j: (j, 0, 0))
    w_out = lambda: pl.BlockSpec((tf, d), lambda j: (j, 0))
    return pl.pallas_call(
        _ffn_cast_kernel,
        out_shape=[jax.ShapeDtypeStruct((m, d), F32), jax.ShapeDtypeStruct((f // tf, d, tf), BF16),
                   jax.ShapeDtypeStruct((f // tf, d, tf), BF16), jax.ShapeDtypeStruct((f, d), BF16)],
        grid=(f // tf,),
        in_specs=[pl.BlockSpec((m, d), lambda j: (0, 0)), pl.BlockSpec((1, d), lambda j: (0, 0)),
                  w_in(), w_in(), w_out()],
        out_specs=[pl.BlockSpec((m, d), lambda j: (0, 0)), w_tile(), w_tile(), w_out()],
        scratch_shapes=[pltpu.VMEM((m, d), BF16)],
        compiler_params=_compiler_params(("arbitrary",)),
        name="swiglu_half_cast",
    )(x, g.reshape(1, d), wg, wu, wd)


def _ffn(x, g, wg, wu, wd, tm):
    m, d = x.shape
    nf, _, tf = wg.shape
    return pl.pallas_call(
        _ffn_kernel,
        out_shape=jax.ShapeDtypeStruct((m, d), F32),
        grid=(m // tm, nf),
        in_specs=[
            pl.BlockSpec((tm, d), lambda i, j: (i, 0), pipeline_mode=pl.Buffered(1)),
            pl.BlockSpec((1, d), lambda i, j: (0, 0)),
            pl.BlockSpec((1, d, tf), lambda i, j: (j, 0, 0)),
            pl.BlockSpec((1, d, tf), lambda i, j: (j, 0, 0)),
            pl.BlockSpec((tf, d), lambda i, j: (j, 0)),
        ],
        out_specs=pl.BlockSpec((tm, d), lambda i, j: (i, 0)),
        scratch_shapes=[pltpu.VMEM((tm, d), BF16)],
        compiler_params=_compiler_params(("parallel", "arbitrary")),
        name="swiglu_half",
    )(x, g.reshape(1, d), wg, wu, wd)


def _proj_kernel(x_ref, g_ref, w_ref, gain_ref, o_ref, h_ref):
    j = pl.program_id(1)

    @pl.when(j == 0)
    def _():
        x = x_ref[...]
        h_ref[...] = (x * _rms_scale(x) * g_ref[...]).astype(BF16)

    u = _dot(h_ref[...], w_ref[...])
    gain = gain_ref[...]

    def is_normed(col):
        return C_Q <= col < C_V or C_KI <= col < C_WI

    for jt in range(D_IN_PAD // PROJ_TN):
        @pl.when(j == jt)
        def _(jt=jt):
            c = 0
            while c < PROJ_TN:
                if is_normed(jt * PROJ_TN + c):
                    part = u[:, c:c + LANES]
                    o_ref[:, c:c + LANES] = part * _rms_scale(part) * gain[:, c:c + LANES]
                    c += LANES
                else:
                    end = c
                    while end < PROJ_TN and not is_normed(jt * PROJ_TN + end):
                        end += LANES
                    o_ref[:, c:end] = u[:, c:end]
                    c = end


def _proj(x, g_mix, w_in, gain, tm):
    m, d = x.shape
    n = w_in.shape[1]
    return pl.pallas_call(
        _proj_kernel,
        out_shape=jax.ShapeDtypeStruct((m, n), F32),
        grid=(m // tm, n // PROJ_TN),
        in_specs=[
            pl.BlockSpec((tm, d), lambda i, j: (i, 0), pipeline_mode=pl.Buffered(1)),
            pl.BlockSpec((1, d), lambda i, j: (0, 0)),
            pl.BlockSpec((d, PROJ_TN), lambda i, j: (0, j)),
            pl.BlockSpec((1, PROJ_TN), lambda i, j: (0, j)),
        ],
        out_specs=pl.BlockSpec((tm, PROJ_TN), lambda i, j: (i, j)),
        scratch_shapes=[pltpu.VMEM((tm, d), BF16)],
        compiler_params=_compiler_params(("parallel", "arbitrary")),
        name="in_proj",
    )(x, g_mix.reshape(1, d), w_in, gain)


HALO = 2 * SUBLANES


def _pool_groups(ext_rows, cur, count_of, wp_ref, scale_ref, store):
    for g, w in enumerate(POOL_WINDOWS):
        lo, hi = g * POOL_GROUP, (g + 1) * POOL_GROUP
        acc = ext_rows(0, lo, hi)
        for d in range(1, w):
            acc = acc + ext_rows(d, lo, hi)
        diff = acc / count_of(w) - cur(lo, hi)
        out = _dot(diff.astype(BF16), wp_ref[g]) * scale_ref[:, lo:hi]
        store(lo, hi, out)


def _pool_prompt_kernel(u_ref, halo_ref, wp_ref, scale_ref, o_ref, ext_ref, *, tp):
    i = pl.program_id(1)
    halo = halo_ref[...]
    ext_ref[:HALO, :] = jnp.where(i == 0, jnp.zeros_like(halo), halo)
    ext_ref[HALO:, :] = u_ref[...]
    pos = i * tp + lax.broadcasted_iota(I32, (tp, POOL_GROUP), 0)

    def store(lo, hi, out):
        o_ref[:, lo:hi] = out.astype(o_ref.dtype)

    _pool_groups(
        lambda d, lo, hi: ext_ref[HALO - d:HALO - d + tp, lo:hi],
        lambda lo, hi: u_ref[:, lo:hi],
        lambda w: jnp.minimum(pos + 1, w).astype(F32),
        wp_ref, scale_ref, store)


def _pool_prompt(u_all, wp, scale, batch, seq, tp):
    nt = seq // tp
    kern = functools.partial(_pool_prompt_kernel, tp=tp)
    return pl.pallas_call(
        kern,
        out_shape=jax.ShapeDtypeStruct((batch * seq, D_POOL), BF16),
        grid=(batch, nt),
        in_specs=[
            pl.BlockSpec((tp, D_POOL), lambda b, i: (b * nt + i, 0)),
            pl.BlockSpec((HALO, D_POOL),
                         lambda b, i: (jnp.maximum((b * seq + i * tp) // HALO - 1, 0), 0)),
            pl.BlockSpec((N_POOL_GROUPS, POOL_GROUP, POOL_GROUP), lambda b, i: (0, 0, 0)),
            pl.BlockSpec((1, D_POOL), lambda b, i: (0, 0)),
        ],
        out_specs=pl.BlockSpec((tp, D_POOL), lambda b, i: (b * nt + i, 0)),
        scratch_shapes=[pltpu.VMEM((HALO + tp, D_POOL), F32)],
        compiler_params=_compiler_params(("parallel", "arbitrary")),
        name="pool_prompt",
    )(u_all, u_all, wp, scale)


def _pool_sample_kernel(ext_ref, wp_ref, scale_ref, o_ref, *, t_new):
    for t in range(t_new):
        def store(lo, hi, out, t=t):
            o_ref[t, :, lo:hi] = out.astype(o_ref.dtype)

        _pool_groups(
            lambda d, lo, hi, t=t: ext_ref[POOL_HIST + t - d, :, lo:hi],
            lambda lo, hi, t=t: ext_ref[POOL_HIST + t, :, lo:hi],
            lambda w: float(w),
            wp_ref, scale_ref, store)


def _pool_sample(ext_t, wp, scale, t_new):
    _, bd, _ = ext_t.shape
    kern = functools.partial(_pool_sample_kernel, t_new=t_new)
    return pl.pallas_call(
        kern,
        out_shape=jax.ShapeDtypeStruct((t_new, bd, D_POOL), BF16),
        compiler_params=pltpu.CompilerParams(vmem_limit_bytes=VMEM_LIMIT_BYTES),
        name="pool_sample",
    )(ext_t, wp, scale)


def _bucket_thresholds():
    n = np.arange(0, 4 * MAX_DISTANCE, dtype=np.int64)
    max_exact = NUM_BUCKETS // 2
    nf = np.maximum(n, 1).astype(np.float32)
    large = max_exact + (np.log(nf / np.float32(max_exact)) / np.float32(math.log(MAX_DISTANCE / max_exact))
                         * np.float32(NUM_BUCKETS - max_exact)).astype(np.int32)
    large = np.minimum(large, NUM_BUCKETS - 1)
    bucket = np.where(n < max_exact, n, large)
    assert np.all(np.diff(bucket) >= 0) and bucket[-1] == NUM_BUCKETS - 1
    return [int(np.argmax(bucket >= b)) for b in range(1, NUM_BUCKETS)]


BUCKET_THRESHOLDS = _bucket_thresholds()
FAR_DISTANCE = BUCKET_THRESHOLDS[-1]
BIAS_W = 2 * LANES
assert FAR_DISTANCE <= LANES


def _bias_kernel(rb_ref, o_ref):
    r = lax.broadcasted_iota(I32, (LANES, BIAS_W), 0)
    c = lax.broadcasted_iota(I32, (LANES, BIAS_W), 1)
    d = LANES + r - c
    bucket = jnp.zeros((LANES, BIAS_W), I32)
    for thr in BUCKET_THRESHOLDS:
        bucket = bucket + (d >= thr).astype(I32)
    for h in range(N_HEADS):
        far = rb_ref[NUM_BUCKETS - 1, h]
        val = jnp.zeros((LANES, BIAS_W), F32)
        for b in range(NUM_BUCKETS - 1):
            val = jnp.where(bucket == b, rb_ref[b, h] - far, val)
        o_ref[h] = jnp.where(d >= 0, val * LOG2E, 0.0)


def _bias_tiles(rel_bias):
    return pl.pallas_call(
        _bias_kernel,
        out_shape=jax.ShapeDtypeStruct((N_HEADS, LANES, BIAS_W), F32),
        in_specs=[pl.BlockSpec(memory_space=pltpu.SMEM)],
        name="rel_bias_tiles",
    )(rel_bias)


RADIX4_MAX_ELEMENTS = 192 * SUBLANES * LANES


def _decode_key(key):
    bits = jnp.where(key < 0, key ^ 0x7FFFFFFF, key)
    return lax.bitcast_convert_type(bits, F32)


def _count(mask):
    return jnp.sum(jnp.where(mask, 1.0, 0.0), axis=1, keepdims=True)


def _topk_mask(score, admissible, kpos, need):
    rows, n = score.shape
    score = jnp.where(admissible, score, -jnp.inf)
    need_f = float(need)

    nonneg = _count(score >= 0.0) >= need_f
    lo0 = jnp.where(nonneg, 0, INT_MIN).astype(I32)

    def kept(cand):
        return _count(score >= _decode_key(cand)) >= need_f

    def bit_step(it, lo):
        cand = lo | (jnp.int32(1) << (30 - it))
        return jnp.where(kept(cand), cand, lo)

    def pair_step(it, lo):
        shift = 28 - 2 * it
        c1, c2, c3 = (lo | (jnp.int32(m) << shift) for m in (1, 2, 3))
        return jnp.where(kept(c3), c3, jnp.where(kept(c2), c2, jnp.where(kept(c1), c1, lo)))

    if rows * n <= RADIX4_MAX_ELEMENTS:
        lo = lax.fori_loop(0, 15, pair_step, bit_step(0, lo0))
    else:
        lo = lax.fori_loop(0, 31, bit_step, lo0)
    thr = jnp.where(lo <= KEY_NEG_INF, -jnp.inf, _decode_key(lo))

    gt = score > thr
    eq = score == thr
    n_gt = _count(gt)
    spare = need_f - n_gt
    nbits = int(n).bit_length()

    def trim_ties():
        def idx_step(it, lim):
            cand = lim + (jnp.int32(1) << (nbits - 1 - it))
            ok = _count(eq & (kpos < cand)) <= spare
            return jnp.where(ok, cand, lim)
        return lax.fori_loop(0, nbits, idx_step, jnp.zeros((rows, 1), I32))

    overshoot = jnp.max(n_gt + _count(eq)) > need_f
    lim = lax.cond(overshoot, trim_ties, lambda: jnp.full((rows, 1), 2 ** nbits, I32))
    return (gt | (eq & (kpos < lim))) & admissible


def _split3_keys(x):
    hi, lo = _split_bf16(x)
    return jnp.concatenate([hi, lo, hi], axis=1)


def _split3_queries(x):
    hi, lo = _split_bf16(x)
    return jnp.concatenate([hi, hi, lo], axis=1)


def _attn_prompt_body(i, ext, first, q_ref, qi_ref, wi_ref, bias_ref, o_ref, k_bf, v_bf, ki3, s_ref, tq, topk):
    def stack_heads(ref, h0, nh, width):
        return jnp.concatenate([ref[:, (h0 + h) * width:(h0 + h + 1) * width] for h in range(nh)], axis=0)

    wi = wi_ref[...] * (IDX_DIM ** -0.5 * IDX_HEADS ** -0.5)
    s_idx = _dot_nt(_split3_queries(stack_heads(qi_ref, 0, IDX_HEADS, IDX_DIM)), ki3[:ext, :])
    score = jnp.zeros((tq, ext), F32)
    for h in range(IDX_HEADS):
        score = score + jnp.maximum(s_idx[h * tq:(h + 1) * tq, :], 0.0) * wi[:, h:h + 1]

    kpos = lax.broadcasted_iota(I32, (tq, ext), 1)
    qpos = i * tq + lax.broadcasted_iota(I32, (tq, ext), 0)
    drop = jnp.where(_topk_mask(score, kpos <= qpos, kpos, topk), 0.0, NEG)

    for n in range(N_KV_HEADS):
        kn = k_bf[:ext, n * HEAD_DIM:(n + 1) * HEAD_DIM]
        vn = v_bf[:ext, n * HEAD_DIM:(n + 1) * HEAD_DIM]
        s_n = s_ref.at[n % 2]
        qn = (stack_heads(q_ref, n * GROUP, GROUP, HEAD_DIM) * (HEAD_DIM ** -0.5 * LOG2E)).astype(BF16)
        s_n[...] = _dot_nt(qn, kn)
        for g in range(GROUP):
            h = n * GROUP + g
            rows = slice(g * tq, (g + 1) * tq)

            def add_near_bias(h=h, rows=rows):
                ws = pl.multiple_of((i - 1) * tq, LANES)
                s_n[rows, pl.ds(ws, BIAS_W)] += bias_ref[h]

            if first:
                @pl.when(i == 0)
                def _(h=h, rows=rows):
                    s_n[rows, :tq] += bias_ref[h, :, LANES:]

                pl.when(i > 0)(add_near_bias)
            else:
                add_near_bias()

        s = (s_n[...].reshape(GROUP, tq, ext) + drop[None]).reshape(GROUP * tq, ext)
        p = jnp.exp2(s - jnp.max(s, axis=1, keepdims=True))
        l = jnp.sum(p, axis=1, keepdims=True)
        o = _dot(p.astype(BF16), vn) / l
        for g in range(GROUP):
            h = n * GROUP + g
            o_ref[:, h * HEAD_DIM:(h + 1) * HEAD_DIM] = o[g * tq:(g + 1) * tq, :].astype(o_ref.dtype)


def _key_prep_kernel(k_ref, v_ref, ki_ref, k_o, v_o, ki3_o):
    k_o[...] = k_ref[...].astype(BF16)
    v_o[...] = v_ref[...].astype(BF16)
    ki3_o[...] = _split3_keys(ki_ref[...])


def _key_prep(u_all, rows, tr):
    return pl.pallas_call(
        _key_prep_kernel,
        out_shape=[jax.ShapeDtypeStruct((rows, D_KV), BF16), jax.ShapeDtypeStruct((rows, D_KV), BF16),
                   jax.ShapeDtypeStruct((rows, 3 * IDX_DIM), BF16)],
        grid=(rows // tr,),
        in_specs=[
            pl.BlockSpec((tr, D_KV), lambda i: (i, C_K // D_KV)),
            pl.BlockSpec((tr, D_KV), lambda i: (i, C_V // D_KV)),
            pl.BlockSpec((tr, IDX_DIM), lambda i: (i, C_KI // IDX_DIM)),
        ],
        out_specs=[pl.BlockSpec((tr, D_KV), lambda i: (i, 0)), pl.BlockSpec((tr, D_KV), lambda i: (i, 0)),
                   pl.BlockSpec((tr, 3 * IDX_DIM), lambda i: (i, 0))],
        compiler_params=_compiler_params(("parallel",)),
        name="key_prep",
    )(u_all, u_all, u_all)


def _attn_prompt_kernel(q_ref, k_ref, v_ref, qi_ref, ki3_ref, wi_ref, bias_ref, o_ref, s_ref,
                        *, tq, topk, ext, tile0):
    _attn_prompt_body(tile0 + pl.program_id(1), ext, tile0 == 0, q_ref, qi_ref, wi_ref, bias_ref, o_ref,
                      k_ref, v_ref, ki3_ref, s_ref, tq, topk)


def _attn_prompt_span(u_all, keys, bias, batch, seq, tile0, ntile, tq, topk):
    nq = seq // tq
    ext = (tile0 + ntile) * tq
    kern = functools.partial(_attn_prompt_kernel, tq=tq, topk=topk, ext=ext, tile0=tile0)
    row = lambda b, j: b * nq + tile0 + j
    k_bf, v_bf, ki3 = keys
    return pl.pallas_call(
        kern,
        out_shape=jax.ShapeDtypeStruct((batch * ntile * tq, D_ATTN), BF16),
        grid=(batch, ntile),
        in_specs=[
            pl.BlockSpec((tq, D_ATTN), lambda b, j: (row(b, j), C_Q // D_ATTN)),
            pl.BlockSpec((seq, D_KV), lambda b, j: (b, 0)),
            pl.BlockSpec((seq, D_KV), lambda b, j: (b, 0)),
            pl.BlockSpec((tq, IDX_HEADS * IDX_DIM), lambda b, j: (row(b, j), C_QI // (IDX_HEADS * IDX_DIM))),
            pl.BlockSpec((seq, 3 * IDX_DIM), lambda b, j: (b, 0)),
            pl.BlockSpec((tq, LANES), lambda b, j: (row(b, j), C_WI // LANES)),
            pl.BlockSpec((N_HEADS, LANES, BIAS_W), lambda b, j: (0, 0, 0)),
        ],
        out_specs=pl.BlockSpec((tq, D_ATTN), lambda b, j: (b * ntile + j, 0)),
        scratch_shapes=[
            pltpu.VMEM((2, GROUP * tq, ext), F32),
        ],
        compiler_params=_compiler_params(("parallel", "parallel")),
        name=f"attn_prompt_{ext}",
    )(u_all, k_bf, v_bf, u_all, ki3, u_all, bias)


ATTN_SPANS = 8


def _attn_prompt(u_all, bias, batch, seq):
    tq = LANES
    nq = seq // tq
    topk = min(TOPK_MAX, seq // 4)
    ntile = nq // ATTN_SPANS
    assert ntile * ATTN_SPANS == nq and ntile >= 2
    keys = _key_prep(u_all, batch * seq, seq)
    spans = [_attn_prompt_span(u_all, keys, bias, batch, seq, e * ntile, ntile, tq, topk)
             .reshape(batch, ntile * tq, D_ATTN) for e in range(ATTN_SPANS)]
    return jnp.concatenate(spans, axis=1).reshape(batch * seq, D_ATTN)


def _page_copies(pt_ref, n_pages, rows, srcs_dsts_sems):
    def copy(src, dst, sem, p, pg):
        return pltpu.make_async_copy(src.at[pg], dst.at[pl.ds(pl.multiple_of(p * rows, rows), rows)], sem)

    def start(b, p0):
        def body(pair, carry):
            for priority in range(2):
                p = 2 * pair + priority
                pg = pt_ref[b, p0 + p]
                for src, dst, sem in srcs_dsts_sems:
                    copy(src, dst, sem, p, pg).start(priority=priority)
            return carry
        assert n_pages % 2 == 0
        lax.fori_loop(0, n_pages // 2, body, 0)

    def wait():
        def body(p, carry):
            for src, dst, sem in srcs_dsts_sems:
                copy(src, dst, sem, p, 0).wait()
            return carry
        lax.fori_loop(0, n_pages, body, 0)

    return start, wait


def _prefetched(step, n_steps, start_into, wait_for):
    @pl.when(step == 0)
    def _():
        start_into(0, step)

    nxt = step + 1
    for slot in range(2):
        @pl.when((nxt < n_steps) & ((nxt & 1) == slot))
        def _(slot=slot):
            start_into(slot, nxt)

    for slot in range(2):
        @pl.when((step & 1) == slot)
        def _(slot=slot):
            wait_for(slot)


def _sample_scores_kernel(pt_ref, cache_ref, qi_ref, w_ref, kinew_ref, o_ref, ki_buf, sem,
                          *, n_pages, page, past, chunk):
    b = pl.program_id(0)
    total = ki_buf.shape[1]

    def copies(slot):
        return _page_copies(pt_ref, n_pages, page, [(cache_ref, ki_buf.at[slot], sem.at[slot])])

    _prefetched(b, pl.num_programs(0), lambda slot, step: copies(slot)[0](step, 0), lambda slot: copies(slot)[1]())
    ki_all = ki_buf.at[b & 1]
    ki_all[past:past + SUBLANES, :] = kinew_ref[0]
    ki_all[past + SUBLANES:, :] = jnp.zeros((total - past - SUBLANES, IDX_DIM), F32)

    q3 = _split3_queries(qi_ref[0])
    w = w_ref[0] * (IDX_DIM ** -0.5 * IDX_HEADS ** -0.5)
    rows = qi_ref.shape[1]
    for c0 in range(0, total, chunk):
        s = _dot_nt(q3, _split3_keys(ki_all[c0:c0 + chunk, :]))
        s = jnp.maximum(s, 0.0) * w
        o_ref[0, :, c0:c0 + chunk] = jnp.sum(s.reshape(rows // IDX_HEADS, IDX_HEADS, chunk), axis=1)


def _sample_scores(page_table, cache_idx, qi_rep, w_col, ki_new, past):
    bd, n_pages = page_table.shape
    page = cache_idx.shape[1]
    total = past + LANES
    rows = qi_rep.shape[1]
    nrow = rows // IDX_HEADS
    n_chunks = 5
    chunk = total // n_chunks
    assert chunk * n_chunks == total and chunk % LANES == 0
    kern = functools.partial(_sample_scores_kernel, n_pages=n_pages, page=page, past=past, chunk=chunk)
    return pl.pallas_call(
        kern,
        out_shape=jax.ShapeDtypeStruct((bd, nrow, total), F32),
        grid_spec=pltpu.PrefetchScalarGridSpec(
            num_scalar_prefetch=1,
            grid=(bd,),
            in_specs=[
                pl.BlockSpec(memory_space=pl.ANY),
                pl.BlockSpec((1, rows, IDX_DIM), lambda b, pt: (b, 0, 0)),
                pl.BlockSpec((1, rows, 1), lambda b, pt: (b, 0, 0)),
                pl.BlockSpec((1, SUBLANES, IDX_DIM), lambda b, pt: (b, 0, 0)),
            ],
            out_specs=pl.BlockSpec((1, nrow, total), lambda b, pt: (b, 0, 0)),
            scratch_shapes=[
                pltpu.VMEM((2, total, IDX_DIM), F32),
                pltpu.SemaphoreType.DMA((2,)),
            ],
        ),
        compiler_params=_compiler_params(("arbitrary",)),
        name="sample_scores",
    )(page_table, cache_idx, qi_rep, w_col, ki_new)


def _sample_topk_kernel(sc_ref, o_ref, *, past, topk, rows_per_batch, group_rows):
    rows, total = sc_ref.shape
    kpos = lax.broadcasted_iota(I32, (rows, total), 1)
    row = lax.broadcasted_iota(I32, (rows, total), 0)
    qpos = past + ((row & (rows_per_batch - 1)) >> (group_rows.bit_length() - 1))
    keep = _topk_mask(sc_ref[...], kpos <= qpos, kpos, topk)
    o_ref[...] = jnp.where(keep, 1.0, 0.0)


def _sample_topk(scores, past, topk, rows_per_batch):
    n, total = scores.shape
    tr = LANES
    assert n % tr == 0 and tr % rows_per_batch == 0 and rows_per_batch & (rows_per_batch - 1) == 0
    kern = functools.partial(_sample_topk_kernel, past=past, topk=topk, rows_per_batch=rows_per_batch,
                             group_rows=GROUP)
    return pl.pallas_call(
        kern,
        out_shape=jax.ShapeDtypeStruct((n, total), F32),
        grid=(n // tr,),
        in_specs=[pl.BlockSpec((tr, total), lambda i: (i, 0))],
        out_specs=pl.BlockSpec((tr, total), lambda i: (i, 0)),
        compiler_params=_compiler_params(("parallel",)),
        name="sample_topk",
    )(scores)


def _sample_attend_kernel(pt_ref, ck_ref, cv_ref, q_ref, knew_ref, vnew_ref, keep_ref, tail_ref,
                          o_ref, k_buf, v_buf, m_ref, l_ref, acc_ref, sem, *, layer, half_pages, page):
    s = pl.program_id(0)
    b = s >> 1
    half_keys = half_pages * page
    new0, new1 = half_keys * N_KV_HEADS, (half_keys + SUBLANES) * N_KV_HEADS

    def copies(slot):
        return _page_copies(pt_ref, half_pages, page * N_KV_HEADS,
                            [(ck_ref.at[layer], k_buf.at[slot], sem.at[0, slot]),
                             (cv_ref.at[layer], v_buf.at[slot], sem.at[1, slot])])

    _prefetched(s, pl.num_programs(0),
                lambda slot, step: copies(slot)[0](step >> 1, (step & 1) * half_pages),
                lambda slot: copies(slot)[1]())

    def partial(half, n):
        nkeys = half_keys + (LANES if half == 1 else 0)
        kn = k_buf[half, pl.ds(n, nkeys, stride=N_KV_HEADS), :].astype(BF16)
        vn = v_buf[half, pl.ds(n, nkeys, stride=N_KV_HEADS), :].astype(BF16)
        sc = _dot_nt((q_ref[0, n] * (HEAD_DIM ** -0.5 * LOG2E)).astype(BF16), kn)
        if half == 1:
            near = nkeys - BIAS_W
            sc = jnp.concatenate([sc[:, :near], sc[:, near:] + tail_ref[n]], axis=1)
        keep = keep_ref[0, :, half * half_keys:half * half_keys + nkeys] > 0.5
        sc = jnp.where(keep, sc, NEG)
        m = jnp.max(sc, axis=1, keepdims=True)
        p = jnp.exp2(sc - m)
        return m, jnp.sum(p, axis=1, keepdims=True), _dot(p.astype(BF16), vn)

    @pl.when((s & 1) == 0)
    def _():
        for n in range(N_KV_HEADS):
            m, l, acc = partial(0, n)
            m_ref[n] = m
            l_ref[n] = l
            acc_ref[n] = acc

    @pl.when((s & 1) == 1)
    def _():
        pad = jnp.zeros(((half_keys + LANES) * N_KV_HEADS - new1, HEAD_DIM), F32)
        k_buf[1, new0:new1, :] = knew_ref[0]
        v_buf[1, new0:new1, :] = vnew_ref[0]
        k_buf[1, new1:, :] = pad
        v_buf[1, new1:, :] = pad
        for n in range(N_KV_HEADS):
            m1, l1, acc1 = partial(1, n)
            m0 = m_ref[n]
            m = jnp.maximum(m0, m1)
            a0 = jnp.exp2(m0 - m)
            a1 = jnp.exp2(m1 - m)
            o = (acc_ref[n] * a0 + acc1 * a1) / (l_ref[n] * a0 + l1 * a1)
            o_ref[0, n] = o.astype(o_ref.dtype)


def _sample_attend(page_table, cache_k, cache_v, layer, page, q_rows, k_new, v_new, keep, tail, past):
    bd, n_pages = page_table.shape
    total = past + LANES
    nrow = q_rows.shape[2]
    half_pages = n_pages // 2
    assert half_pages * 2 == n_pages and BIAS_W <= LANES + half_pages * page
    buf_rows = (half_pages * page + LANES) * N_KV_HEADS
    kern = functools.partial(_sample_attend_kernel, layer=layer, half_pages=half_pages, page=page)
    per_batch = lambda rank: (lambda s, pt: (s // 2,) + (0,) * (rank - 1))
    return pl.pallas_call(
        kern,
        out_shape=jax.ShapeDtypeStruct((bd, N_KV_HEADS, nrow, HEAD_DIM), BF16),
        grid_spec=pltpu.PrefetchScalarGridSpec(
            num_scalar_prefetch=1,
            grid=(2 * bd,),
            in_specs=[
                pl.BlockSpec(memory_space=pl.ANY),
                pl.BlockSpec(memory_space=pl.ANY),
                pl.BlockSpec((1, N_KV_HEADS, nrow, HEAD_DIM), per_batch(4)),
                pl.BlockSpec((1, SUBLANES * N_KV_HEADS, HEAD_DIM), per_batch(3)),
                pl.BlockSpec((1, SUBLANES * N_KV_HEADS, HEAD_DIM), per_batch(3)),
                pl.BlockSpec((1, nrow, total), per_batch(3)),
                pl.BlockSpec((N_KV_HEADS, nrow, BIAS_W), lambda s, pt: (0, 0, 0)),
            ],
            out_specs=pl.BlockSpec((1, N_KV_HEADS, nrow, HEAD_DIM), per_batch(4)),
            scratch_shapes=[
                pltpu.VMEM((2, buf_rows, HEAD_DIM), F32),
                pltpu.VMEM((2, buf_rows, HEAD_DIM), F32),
                pltpu.VMEM((N_KV_HEADS, nrow, 1), F32),
                pltpu.VMEM((N_KV_HEADS, nrow, 1), F32),
                pltpu.VMEM((N_KV_HEADS, nrow, HEAD_DIM), F32),
                pltpu.SemaphoreType.DMA((2, 2)),
            ],
        ),
        compiler_params=_compiler_params(("arbitrary",)),
        name="sample_attend",
    )(page_table, cache_k, cache_v, q_rows, k_new, v_new, keep, tail)


def _out_proj_kernel(x_ref, pool_ref, attn_ref, wp_ref, wa_ref, o_ref):
    o_ref[...] = x_ref[...] + (_dot(pool_ref[...], wp_ref[...]) + _dot(attn_ref[...], wa_ref[...]))


def _out_proj(x, pool, attn, w_out, tm, tn):
    m, d = x.shape
    return pl.pallas_call(
        _out_proj_kernel,
        out_shape=jax.ShapeDtypeStruct((m, d), F32),
        grid=(m // tm, d // tn),
        in_specs=[
            pl.BlockSpec((tm, tn), lambda i, j: (i, j)),
            pl.BlockSpec((tm, D_POOL), lambda i, j: (i, 0)),
            pl.BlockSpec((tm, D_ATTN), lambda i, j: (i, 0)),
            pl.BlockSpec((D_POOL, tn), lambda i, j: (0, j)),
            pl.BlockSpec((D_ATTN, tn), lambda i, j: (D_POOL // D_ATTN, j)),
        ],
        out_specs=pl.BlockSpec((tm, tn), lambda i, j: (i, j)),
        compiler_params=_compiler_params(("parallel", "arbitrary")),
        name="out_proj",
    )(x, pool, attn, w_out, w_out)


def _proj_gain(g_q, g_k, g_idx_k):
    return jnp.concatenate([
        jnp.ones((D_POOL,), F32), jnp.tile(g_q, N_HEADS), jnp.tile(g_k, N_KV_HEADS),
        jnp.ones((C_KI - C_V,), F32), g_idx_k, jnp.ones((D_IN_PAD - C_WI,), F32)]).reshape(1, D_IN_PAD)


def _pad_rows(a, rows):
    return jnp.pad(a, ((0, 0), (0, rows - a.shape[1]), (0, 0)))


def kernel(x_prompt, x_sample, cache_k, cache_v, cache_idx_k, state_pool, page_table, g_ffn1, w_gate1, w_up1,
           w_down1, g_mix, w_in, g_q, g_k, g_idx_k, w_pool, pool_scale, w_out, rel_bias, g_ffn2, w_gate2,
           w_up2, w_down2):
    batch, seq, d = x_prompt.shape
    bd, t_new, _ = x_sample.shape
    depth = g_ffn1.shape[0]
    assert depth == 1 and d == D_MODEL
    past = page_table.shape[1] * cache_k.shape[2]
    mp, ms = batch * seq, bd * t_new
    tm_p, tm_s = 512, ms
    tf = 256

    xp = x_prompt.reshape(mp, d)
    xs = x_sample.reshape(ms, d)
    l = 0

    xs, wg1, wu1, wd1 = _ffn_cast(xs, g_ffn1[l], w_gate1[l], w_up1[l], w_down1[l], tf)
    xp = _ffn(xp, g_ffn1[l], wg1, wu1, wd1, tm_p)

    w_in_p = jnp.pad(w_in[l], ((0, 0), (0, D_IN_PAD - D_IN))).astype(BF16)
    gain = _proj_gain(g_q[l], g_k[l], g_idx_k[l])
    up = _proj(xp, g_mix[l], w_in_p, gain, tm_p)
    us = _proj(xs, g_mix[l], w_in_p, gain, tm_s)

    wp = w_pool[l].astype(BF16)
    scale = pool_scale[l].reshape(1, D_POOL)
    bias = _bias_tiles(rel_bias)

    pool_p = _pool_prompt(up, wp, scale, batch, seq, 256)
    attn_p = _attn_prompt(up, bias, batch, seq)

    hist = state_pool[l]
    u_pool_s = us[:, :D_POOL].reshape(bd, t_new, D_POOL)
    ext = jnp.concatenate([hist, u_pool_s], axis=1)
    pool_s = _pool_sample(ext.transpose(1, 0, 2), wp, scale, t_new)
    pool_s = pool_s.transpose(1, 0, 2).reshape(ms, D_POOL)

    us3 = us.reshape(bd, t_new, D_IN_PAD)
    rows_i = t_new * GROUP * IDX_HEADS
    qi_s = us3[:, :, C_QI:C_KI].reshape(bd, t_new, 1, IDX_HEADS, IDX_DIM)
    qi_rep = jnp.broadcast_to(qi_s, (bd, t_new, GROUP, IDX_HEADS, IDX_DIM)).reshape(bd, rows_i, IDX_DIM)
    wi_s = us3[:, :, C_WI:C_WI + IDX_HEADS].reshape(bd, t_new, 1, IDX_HEADS)
    w_col = jnp.broadcast_to(wi_s, (bd, t_new, GROUP, IDX_HEADS)).reshape(bd, rows_i, 1)
    ki_new = _pad_rows(us3[:, :, C_KI:C_WI], SUBLANES)
    k_new = _pad_rows(us3[:, :, C_K:C_V], SUBLANES)
    v_new = _pad_rows(us3[:, :, C_V:C_QI], SUBLANES)
    n_phys, page = cache_k.shape[1], cache_k.shape[2]
    topk_s = min(TOPK_MAX, (past + t_new) // 4)
    nrow = t_new * GROUP
    scores_s = _sample_scores(page_table, cache_idx_k[l], qi_rep, w_col, ki_new, past)
    keep_s = _sample_topk(scores_s.reshape(bd * nrow, past + LANES), past, topk_s, nrow)
    keep_s = keep_s.reshape(bd, nrow, past + LANES)

    q_rows = us3[:, :, C_Q:C_K].reshape(bd, t_new, N_KV_HEADS, GROUP, HEAD_DIM)
    q_rows = q_rows.transpose(0, 2, 1, 3, 4).reshape(bd, N_KV_HEADS, nrow, HEAD_DIM)
    tail = bias[:, :t_new, :].reshape(N_KV_HEADS, GROUP, t_new, BIAS_W)
    tail = tail.transpose(0, 2, 1, 3).reshape(N_KV_HEADS, nrow, BIAS_W)
    rows_shape = (depth, n_phys, page * N_KV_HEADS, HEAD_DIM)
    new_shape = (bd, SUBLANES * N_KV_HEADS, HEAD_DIM)
    attn_s = _sample_attend(page_table, cache_k.reshape(rows_shape), cache_v.reshape(rows_shape), l, page, q_rows,
                            k_new.reshape(new_shape), v_new.reshape(new_shape), keep_s, tail, past)
    attn_s = attn_s.reshape(bd, N_KV_HEADS, t_new, GROUP, HEAD_DIM).transpose(0, 2, 1, 3, 4).reshape(ms, D_ATTN)

    w_out_bf = w_out[l].astype(BF16)
    xp = _out_proj(xp, pool_p, attn_p, w_out_bf, tm_p, 1024)
    xs = _out_proj(xs, pool_s, attn_s, w_out_bf, tm_s, 1024)

    xs, wg2, wu2, wd2 = _ffn_cast(xs, g_ffn2[l], w_gate2[l], w_up2[l], w_down2[l], tf)
    xp = _ffn(xp, g_ffn2[l], wg2, wu2, wd2, tm_p)

    up4 = up.reshape(batch, seq, D_IN_PAD)
    return (
        xp.reshape(batch, seq, d),
        xs.reshape(bd, t_new, d),
        up4[:, :, C_K:C_V].reshape(1, batch, seq, N_KV_HEADS, HEAD_DIM),
        up4[:, :, C_V:C_QI].reshape(1, batch, seq, N_KV_HEADS, HEAD_DIM),
        up4[:, :, C_KI:C_WI].reshape(1, batch, seq, IDX_DIM),
        up4[:, seq - POOL_HIST:, :D_POOL].reshape(1, batch, POOL_HIST, D_POOL),
        us3[:, :, C_K:C_V].reshape(1, bd, t_new, N_KV_HEADS, HEAD_DIM),
        us3[:, :, C_V:C_QI].reshape(1, bd, t_new, N_KV_HEADS, HEAD_DIM),
        us3[:, :, C_KI:C_WI].reshape(1, bd, t_new, IDX_DIM),
        ext[:, t_new:].reshape(1, bd, POOL_HIST, D_POOL),
    )
```

```python
import functools
import math

import jax
import jax.numpy as jnp
import numpy as np
from jax import lax
from jax.experimental import pallas as pl
from jax.experimental.pallas import tpu as pltpu

F32 = jnp.float32
BF16 = jnp.bfloat16
I32 = jnp.int32

SUBLANES = 8
LANES = 128
VMEM_LIMIT_BYTES = 56 * 1024 * 1024

D_MODEL = 4096
D_POOL = D_MODEL // 2
POOL_WINDOWS = (2, 4, 8, 16)
N_POOL_GROUPS = len(POOL_WINDOWS)
POOL_GROUP = D_POOL // N_POOL_GROUPS
POOL_HIST = max(POOL_WINDOWS) - 1
HEAD_DIM = 128
N_HEADS = (D_MODEL - D_POOL) // HEAD_DIM
N_KV_HEADS = 4
GROUP = N_HEADS // N_KV_HEADS
D_ATTN = N_HEADS * HEAD_DIM
D_KV = N_KV_HEADS * HEAD_DIM
IDX_HEADS = 8
IDX_DIM = 128
TOPK_MAX = 256
NUM_BUCKETS = 32
MAX_DISTANCE = 128
EPS = 1e-6
NEG = -1e30
LOG2E = math.log2(math.e)

C_Q = D_POOL
C_K = C_Q + D_ATTN
C_V = C_K + D_KV
C_QI = C_V + D_KV
C_KI = C_QI + IDX_HEADS * IDX_DIM
C_WI = C_KI + IDX_DIM
D_IN = C_WI + IDX_HEADS
D_IN_PAD = ((D_IN + 2 * LANES - 1) // (2 * LANES)) * (2 * LANES)
PROJ_TN = 10 * LANES
assert D_IN_PAD % PROJ_TN == 0

INT_MIN = -(2 ** 31)
KEY_NEG_INF = INT_MIN + 0x007FFFFF


def _compiler_params(semantics):
    return pltpu.CompilerParams(dimension_semantics=semantics, vmem_limit_bytes=VMEM_LIMIT_BYTES)


def _dot(a, b):
    return jnp.dot(a, b, preferred_element_type=F32)


def _dot_nt(a, b):
    return lax.dot_general(a, b, (((1,), (1,)), ((), ())), preferred_element_type=F32)


def _rms_scale(x):
    return lax.rsqrt(jnp.mean(x * x, axis=-1, keepdims=True) + EPS)


def _split_bf16(x):
    hi = x.astype(BF16)
    lo = (x - hi.astype(F32)).astype(BF16)
    return hi, lo


def _ffn_step(j, last, x_ref, g_ref, load_weights, o_ref, h_ref):
    @pl.when(j == 0)
    def _():
        x = x_ref[...]
        h_ref[...] = (x * _rms_scale(x) * g_ref[...]).astype(BF16)
        o_ref[...] = jnp.zeros_like(o_ref)

    wg, wu, wd = load_weights()
    h = h_ref[...]
    a = _dot(h, wg)
    b = _dot(h, wu)
    s = (a * jax.nn.sigmoid(a) * b).astype(BF16)
    o_ref[...] += _dot(s, wd)

    @pl.when(j == last)
    def _():
        o_ref[...] = x_ref[...] + 0.5 * o_ref[...]


def _ffn_kernel(x_ref, g_ref, wg_ref, wu_ref, wd_ref, o_ref, h_ref):
    _ffn_step(pl.program_id(1), pl.num_programs(1) - 1, x_ref, g_ref,
              lambda: (wg_ref[0], wu_ref[0], wd_ref[...]), o_ref, h_ref)


def _ffn_cast_kernel(x_ref, g_ref, wg_ref, wu_ref, wd_ref, o_ref, wg_o, wu_o, wd_o, h_ref):
    def load_weights():
        wg = wg_ref[...].astype(BF16)
        wu = wu_ref[...].astype(BF16)
        wd = wd_ref[...].astype(BF16)
        wg_o[0] = wg
        wu_o[0] = wu
        wd_o[...] = wd
        return wg, wu, wd

    _ffn_step(pl.program_id(0), pl.num_programs(0) - 1, x_ref, g_ref, load_weights, o_ref, h_ref)


def _ffn_cast(x, g, wg, wu, wd, tf):
    m, d = x.shape
    f = wg.shape[1]
    w_in = lambda: pl.BlockSpec((d, tf), lambda j: (0, j))
    w_tile = lambda: pl.BlockSpec((1, d, tf), lambda j: (j, 0, 0))
    w_out = lambda: pl.BlockSpec((tf, d), lambda j: (j, 0))
    return pl.pallas_call(
        _ffn_cast_kernel,
        out_shape=[jax.ShapeDtypeStruct((m, d), F32), jax.ShapeDtypeStruct((f // tf, d, tf), BF16),
                   jax.ShapeDtypeStruct((f // tf, d, tf), BF16), jax.ShapeDtypeStruct((f, d), BF16)],
        grid=(f // tf,),
        in_specs=[pl.BlockSpec((m, d), lambda j: (0, 0)), pl.BlockSpec((1, d), lambda j: (0, 0)),
                  w_in(), w_in(), w_out()],
        out_specs=[pl.BlockSpec((m, d), lambda j: (0, 0)), w_tile(), w_tile(), w_out()],
        scratch_shapes=[pltpu.VMEM((m, d), BF16)],
        compiler_params=_compiler_params(("arbitrary",)),
        name="swiglu_half_cast",
    )(x, g.reshape(1, d), wg, wu, wd)


def _ffn(x, g, wg, wu, wd, tm):
    m, d = x.shape
    nf, _, tf = wg.shape
    return pl.pallas_call(
        _ffn_kernel,
        out_shape=jax.ShapeDtypeStruct((m, d), F32),
        grid=(m // tm, nf),
        in_specs=[
            pl.BlockSpec((tm, d), lambda i, j: (i, 0), pipeline_mode=pl.Buffered(1)),
            pl.BlockSpec((1, d), lambda i, j: (0, 0)),
            pl.BlockSpec((1, d, tf), lambda i, j: (j, 0, 0)),
            pl.BlockSpec((1, d, tf), lambda i, j: (j, 0, 0)),
            pl.BlockSpec((tf, d), lambda i, j: (j, 0)),
        ],
        out_specs=pl.BlockSpec((tm, d), lambda i, j: (i, 0)),
        scratch_shapes=[pltpu.VMEM((tm, d), BF16)],
        compiler_params=_compiler_params(("parallel", "arbitrary")),
        name="swiglu_half",
    )(x, g.reshape(1, d), wg, wu, wd)


def _proj_kernel(x_ref, g_ref, w_ref, gain_ref, o_ref, h_ref):
    j = pl.program_id(1)

    @pl.when(j == 0)
    def _():
        x = x_ref[...]
        h_ref[...] = (x * _rms_scale(x) * g_ref[...]).astype(BF16)

    u = _dot(h_ref[...], w_ref[...])
    gain = gain_ref[...]

    def is_normed(col):
        return C_Q <= col < C_V or C_KI <= col < C_WI

    for jt in range(D_IN_PAD // PROJ_TN):
        @pl.when(j == jt)
        def _(jt=jt):
            c = 0
            while c < PROJ_TN:
                if is_normed(jt * PROJ_TN + c):
                    part = u[:, c:c + LANES]
                    o_ref[:, c:c + LANES] = part * _rms_scale(part) * gain[:, c:c + LANES]
                    c += LANES
                else:
                    end = c
                    while end < PROJ_TN and not is_normed(jt * PROJ_TN + end):
                        end += LANES
                    o_ref[:, c:end] = u[:, c:end]
                    c = end


def _proj(x, g_mix, w_in, gain, tm):
    m, d = x.shape
    n = w_in.shape[1]
    return pl.pallas_call(
        _proj_kernel,
        out_shape=jax.ShapeDtypeStruct((m, n), F32),
        grid=(m // tm, n // PROJ_TN),
        in_specs=[
            pl.BlockSpec((tm, d), lambda i, j: (i, 0), pipeline_mode=pl.Buffered(1)),
            pl.BlockSpec((1, d), lambda i, j: (0, 0)),
            pl.BlockSpec((d, PROJ_TN), lambda i, j: (0, j)),
            pl.BlockSpec((1, PROJ_TN), lambda i, j: (0, j)),
        ],
        out_specs=pl.BlockSpec((tm, PROJ_TN), lambda i, j: (i, j)),
        scratch_shapes=[pltpu.VMEM((tm, d), BF16)],
        compiler_params=_compiler_params(("parallel", "arbitrary")),
        name="in_proj",
    )(x, g_mix.reshape(1, d), w_in, gain)


HALO = 2 * SUBLANES


def _pool_groups(ext_rows, cur, count_of, wp_ref, scale_ref, store):
    for g, w in enumerate(POOL_WINDOWS):
        lo, hi = g * POOL_GROUP, (g + 1) * POOL_GROUP
        acc = ext_rows(0, lo, hi)
        for d in range(1, w):
            acc = acc + ext_rows(d, lo, hi)
        diff = acc / count_of(w) - cur(lo, hi)
        out = _dot(diff.astype(BF16), wp_ref[g]) * scale_ref[:, lo:hi]
        store(lo, hi, out)


def _pool_prompt_kernel(u_ref, halo_ref, wp_ref, scale_ref, o_ref, ext_ref, *, tp):
    i = pl.program_id(1)
    halo = halo_ref[...]
    ext_ref[:HALO, :] = jnp.where(i == 0, jnp.zeros_like(halo), halo)
    ext_ref[HALO:, :] = u_ref[...]
    pos = i * tp + lax.broadcasted_iota(I32, (tp, POOL_GROUP), 0)

    def store(lo, hi, out):
        o_ref[:, lo:hi] = out.astype(o_ref.dtype)

    _pool_groups(
        lambda d, lo, hi: ext_ref[HALO - d:HALO - d + tp, lo:hi],
        lambda lo, hi: u_ref[:, lo:hi],
        lambda w: jnp.minimum(pos + 1, w).astype(F32),
        wp_ref, scale_ref, store)


def _pool_prompt(u_all, wp, scale, batch, seq, tp):
    nt = seq // tp
    kern = functools.partial(_pool_prompt_kernel, tp=tp)
    return pl.pallas_call(
        kern,
        out_shape=jax.ShapeDtypeStruct((batch * seq, D_POOL), BF16),
        grid=(batch, nt),
        in_specs=[
            pl.BlockSpec((tp, D_POOL), lambda b, i: (b * nt + i, 0)),
            pl.BlockSpec((HALO, D_POOL),
                         lambda b, i: (jnp.maximum((b * seq + i * tp) // HALO - 1, 0), 0)),
            pl.BlockSpec((N_POOL_GROUPS, POOL_GROUP, POOL_GROUP), lambda b, i: (0, 0, 0)),
            pl.BlockSpec((1, D_POOL), lambda b, i: (0, 0)),
        ],
        out_specs=pl.BlockSpec((tp, D_POOL), lambda b, i: (b * nt + i, 0)),
        scratch_shapes=[pltpu.VMEM((HALO + tp, D_POOL), F32)],
        compiler_params=_compiler_params(("parallel", "arbitrary")),
        name="pool_prompt",
    )(u_all, u_all, wp, scale)


def _pool_sample_kernel(ext_ref, wp_ref, scale_ref, o_ref, *, t_new):
    for t in range(t_new):
        def store(lo, hi, out, t=t):
            o_ref[t, :, lo:hi] = out.astype(o_ref.dtype)

        _pool_groups(
            lambda d, lo, hi, t=t: ext_ref[POOL_HIST + t - d, :, lo:hi],
            lambda lo, hi, t=t: ext_ref[POOL_HIST + t, :, lo:hi],
            lambda w: float(w),
            wp_ref, scale_ref, store)


def _pool_sample(ext_t, wp, scale, t_new):
    _, bd, _ = ext_t.shape
    kern = functools.partial(_pool_sample_kernel, t_new=t_new)
    return pl.pallas_call(
        kern,
        out_shape=jax.ShapeDtypeStruct((t_new, bd, D_POOL), BF16),
        compiler_params=pltpu.CompilerParams(vmem_limit_bytes=VMEM_LIMIT_BYTES),
        name="pool_sample",
    )(ext_t, wp, scale)


def _bucket_thresholds():
    n = np.arange(0, 4 * MAX_DISTANCE, dtype=np.int64)
    max_exact = NUM_BUCKETS // 2
    nf = np.maximum(n, 1).astype(np.float32)
    large = max_exact + (np.log(nf / np.float32(max_exact)) / np.float32(math.log(MAX_DISTANCE / max_exact))
                         * np.float32(NUM_BUCKETS - max_exact)).astype(np.int32)
    large = np.minimum(large, NUM_BUCKETS - 1)
    bucket = np.where(n < max_exact, n, large)
    assert np.all(np.diff(bucket) >= 0) and bucket[-1] == NUM_BUCKETS - 1
    return [int(np.argmax(bucket >= b)) for b in range(1, NUM_BUCKETS)]


BUCKET_THRESHOLDS = _bucket_thresholds()
FAR_DISTANCE = BUCKET_THRESHOLDS[-1]
BIAS_W = 2 * LANES
assert FAR_DISTANCE <= LANES


def _bias_kernel(rb_ref, o_ref):
    r = lax.broadcasted_iota(I32, (LANES, BIAS_W), 0)
    c = lax.broadcasted_iota(I32, (LANES, BIAS_W), 1)
    d = LANES + r - c
    bucket = jnp.zeros((LANES, BIAS_W), I32)
    for thr in BUCKET_THRESHOLDS:
        bucket = bucket + (d >= thr).astype(I32)
    for h in range(N_HEADS):
        far = rb_ref[NUM_BUCKETS - 1, h]
        val = jnp.zeros((LANES, BIAS_W), F32)
        for b in range(NUM_BUCKETS - 1):
            val = jnp.where(bucket == b, rb_ref[b, h] - far, val)
        o_ref[h] = jnp.where(d >= 0, val * LOG2E, 0.0)


def _bias_tiles(rel_bias):
    return pl.pallas_call(
        _bias_kernel,
        out_shape=jax.ShapeDtypeStruct((N_HEADS, LANES, BIAS_W), F32),
        in_specs=[pl.BlockSpec(memory_space=pltpu.SMEM)],
        name="rel_bias_tiles",
    )(rel_bias)


RADIX4_MAX_ELEMENTS = 192 * SUBLANES * LANES


def _decode_key(key):
    bits = jnp.where(key < 0, key ^ 0x7FFFFFFF, key)
    return lax.bitcast_convert_type(bits, F32)


def _count(mask):
    return jnp.sum(jnp.where(mask, 1.0, 0.0), axis=1, keepdims=True)


def _topk_mask(score, admissible, kpos, need):
    rows, n = score.shape
    score = jnp.where(admissible, score, -jnp.inf)
    need_f = float(need)

    nonneg = _count(score >= 0.0) >= need_f
    lo0 = jnp.where(nonneg, 0, INT_MIN).astype(I32)

    def kept(cand):
        return _count(score >= _decode_key(cand)) >= need_f

    def bit_step(it, lo):
        cand = lo | (jnp.int32(1) << (30 - it))
        return jnp.where(kept(cand), cand, lo)

    def pair_step(it, lo):
        shift = 28 - 2 * it
        c1, c2, c3 = (lo | (jnp.int32(m) << shift) for m in (1, 2, 3))
        return jnp.where(kept(c3), c3, jnp.where(kept(c2), c2, jnp.where(kept(c1), c1, lo)))

    if rows * n <= RADIX4_MAX_ELEMENTS:
        lo = lax.fori_loop(0, 15, pair_step, bit_step(0, lo0))
    else:
        lo = lax.fori_loop(0, 31, bit_step, lo0)
    thr = jnp.where(lo <= KEY_NEG_INF, -jnp.inf, _decode_key(lo))

    gt = score > thr
    eq = score == thr
    n_gt = _count(gt)
    spare = need_f - n_gt
    nbits = int(n).bit_length()

    def trim_ties():
        def idx_step(it, lim):
            cand = lim + (jnp.int32(1) << (nbits - 1 - it))
            ok = _count(eq & (kpos < cand)) <= spare
            return jnp.where(ok, cand, lim)
        return lax.fori_loop(0, nbits, idx_step, jnp.zeros((rows, 1), I32))

    overshoot = jnp.max(n_gt + _count(eq)) > need_f
    lim = lax.cond(overshoot, trim_ties, lambda: jnp.full((rows, 1), 2 ** nbits, I32))
    return (gt | (eq & (kpos < lim))) & admissible


def _split3_keys(x):
    hi, lo = _split_bf16(x)
    return jnp.concatenate([hi, lo, hi], axis=1)


def _split3_queries(x):
    hi, lo = _split_bf16(x)
    return jnp.concatenate([hi, hi, lo], axis=1)


def _attn_prompt_body(i, ext, first, q_ref, qi_ref, wi_ref, bias_ref, o_ref, k_bf, v_bf, ki3, s_ref, tq, topk):
    def stack_heads(ref, h0, nh, width):
        return jnp.concatenate([ref[:, (h0 + h) * width:(h0 + h + 1) * width] for h in range(nh)], axis=0)

    wi = wi_ref[...] * (IDX_DIM ** -0.5 * IDX_HEADS ** -0.5)
    score = jnp.zeros((tq, ext), F32)
    for h0 in range(0, IDX_HEADS, IDX_HEADS_PER_PASS):
        s_idx = _dot_nt(_split3_queries(stack_heads(qi_ref, h0, IDX_HEADS_PER_PASS, IDX_DIM)), ki3[:ext, :])
        for g in range(IDX_HEADS_PER_PASS):
            score = score + jnp.maximum(s_idx[g * tq:(g + 1) * tq, :], 0.0) * wi[:, h0 + g:h0 + g + 1]

    kpos = lax.broadcasted_iota(I32, (tq, ext), 1)
    qpos = i * tq + lax.broadcasted_iota(I32, (tq, ext), 0)
    drop = jnp.where(_topk_mask(score, kpos <= qpos, kpos, topk), 0.0, NEG)

    for n in range(N_KV_HEADS):
        kn = k_bf[:ext, n * HEAD_DIM:(n + 1) * HEAD_DIM]
        vn = v_bf[:ext, n * HEAD_DIM:(n + 1) * HEAD_DIM]
        for part in range(GROUP // HEADS_PER_PASS):
            h0 = n * GROUP + part * HEADS_PER_PASS
            s_n = s_ref.at[(n * (GROUP // HEADS_PER_PASS) + part) % 2]
            qn = (stack_heads(q_ref, h0, HEADS_PER_PASS, HEAD_DIM) * (HEAD_DIM ** -0.5 * LOG2E)).astype(BF16)
            s_n[...] = _dot_nt(qn, kn)
            for g in range(HEADS_PER_PASS):
                h = h0 + g
                rows = slice(g * tq, (g + 1) * tq)

                def add_near_bias(h=h, rows=rows, s_n=s_n):
                    ws = pl.multiple_of((i - 1) * tq, LANES)
                    s_n[rows, pl.ds(ws, BIAS_W)] += bias_ref[h]

                if first:
                    @pl.when(i == 0)
                    def _(h=h, rows=rows, s_n=s_n):
                        s_n[rows, :tq] += bias_ref[h, :, LANES:]

                    pl.when(i > 0)(add_near_bias)
                else:
                    add_near_bias()

            s = (s_n[...].reshape(HEADS_PER_PASS, tq, ext) + drop[None]).reshape(HEADS_PER_PASS * tq, ext)
            p = jnp.exp2(s - jnp.max(s, axis=1, keepdims=True))
            l = jnp.sum(p, axis=1, keepdims=True)
            o = _dot(p.astype(BF16), vn) / l
            for g in range(HEADS_PER_PASS):
                h = h0 + g
                o_ref[:, h * HEAD_DIM:(h + 1) * HEAD_DIM] = o[g * tq:(g + 1) * tq, :].astype(o_ref.dtype)


def _key_prep_kernel(k_ref, v_ref, ki_ref, k_o, v_o, ki3_o):
    k_o[...] = k_ref[...].astype(BF16)
    v_o[...] = v_ref[...].astype(BF16)
    ki3_o[...] = _split3_keys(ki_ref[...])


def _key_prep(u_all, rows, tr):
    return pl.pallas_call(
        _key_prep_kernel,
        out_shape=[jax.ShapeDtypeStruct((rows, D_KV), BF16), jax.ShapeDtypeStruct((rows, D_KV), BF16),
                   jax.ShapeDtypeStruct((rows, 3 * IDX_DIM), BF16)],
        grid=(rows // tr,),
        in_specs=[
            pl.BlockSpec((tr, D_KV), lambda i: (i, C_K // D_KV)),
            pl.BlockSpec((tr, D_KV), lambda i: (i, C_V // D_KV)),
            pl.BlockSpec((tr, IDX_DIM), lambda i: (i, C_KI // IDX_DIM)),
        ],
        out_specs=[pl.BlockSpec((tr, D_KV), lambda i: (i, 0)), pl.BlockSpec((tr, D_KV), lambda i: (i, 0)),
                   pl.BlockSpec((tr, 3 * IDX_DIM), lambda i: (i, 0))],
        compiler_params=_compiler_params(("parallel",)),
        name="key_prep",
    )(u_all, u_all, u_all)


def _attn_prompt_kernel(q_ref, k_ref, v_ref, qi_ref, ki3_ref, wi_ref, bias_ref, o_ref, s_ref,
                        *, tq, topk, ext, tile0):
    _attn_prompt_body(tile0 + pl.program_id(1), ext, tile0 == 0, q_ref, qi_ref, wi_ref, bias_ref, o_ref,
                      k_ref, v_ref, ki3_ref, s_ref, tq, topk)


def _attn_prompt_span(u_all, keys, bias, batch, seq, tile0, ntile, tq, topk):
    nq = seq // tq
    ext = (tile0 + ntile) * tq
    kern = functools.partial(_attn_prompt_kernel, tq=tq, topk=topk, ext=ext, tile0=tile0)
    row = lambda b, j: b * nq + tile0 + j
    k_bf, v_bf, ki3 = keys
    return pl.pallas_call(
        kern,
        out_shape=jax.ShapeDtypeStruct((batch * ntile * tq, D_ATTN), BF16),
        grid=(batch, ntile),
        in_specs=[
            pl.BlockSpec((tq, D_ATTN), lambda b, j: (row(b, j), C_Q // D_ATTN)),
            pl.BlockSpec((seq, D_KV), lambda b, j: (b, 0)),
            pl.BlockSpec((seq, D_KV), lambda b, j: (b, 0)),
            pl.BlockSpec((tq, IDX_HEADS * IDX_DIM), lambda b, j: (row(b, j), C_QI // (IDX_HEADS * IDX_DIM))),
            pl.BlockSpec((seq, 3 * IDX_DIM), lambda b, j: (b, 0)),
            pl.BlockSpec((tq, LANES), lambda b, j: (row(b, j), C_WI // LANES)),
            pl.BlockSpec((N_HEADS, LANES, BIAS_W), lambda b, j: (0, 0, 0)),
        ],
        out_specs=pl.BlockSpec((tq, D_ATTN), lambda b, j: (b * ntile + j, 0)),
        scratch_shapes=[
            pltpu.VMEM((2, HEADS_PER_PASS * tq, ext), F32),
        ],
        compiler_params=_compiler_params(("parallel", "parallel")),
        name=f"attn_prompt_{ext}",
    )(u_all, k_bf, v_bf, u_all, ki3, u_all, bias)


ATTN_SPANS = 8
HEADS_PER_PASS = 4
assert GROUP % HEADS_PER_PASS == 0
IDX_HEADS_PER_PASS = 4
assert IDX_HEADS % IDX_HEADS_PER_PASS == 0


def _attn_prompt(u_all, bias, batch, seq):
    tq = LANES
    nq = seq // tq
    topk = min(TOPK_MAX, seq // 4)
    ntile = nq // ATTN_SPANS
    assert ntile * ATTN_SPANS == nq and ntile >= 2
    keys = _key_prep(u_all, batch * seq, seq)
    spans = [_attn_prompt_span(u_all, keys, bias, batch, seq, e * ntile, ntile, tq, topk)
             .reshape(batch, ntile * tq, D_ATTN) for e in range(ATTN_SPANS)]
    return jnp.concatenate(spans, axis=1).reshape(batch * seq, D_ATTN)


def _page_copies(pt_ref, n_pages, rows, srcs_dsts_sems):
    def copy(src, dst, sem, p, pg):
        return pltpu.make_async_copy(src.at[pg], dst.at[pl.ds(pl.multiple_of(p * rows, rows), rows)], sem)

    def start(b, p0):
        def body(p, carry):
            pg = pt_ref[b, p0 + p]
            for src, dst, sem in srcs_dsts_sems:
                copy(src, dst, sem, p, pg).start()
            return carry
        lax.fori_loop(0, n_pages, body, 0)

    def wait():
        def body(p, carry):
            for src, dst, sem in srcs_dsts_sems:
                copy(src, dst, sem, p, 0).wait()
            return carry
        lax.fori_loop(0, n_pages, body, 0)

    return start, wait


def _prefetched(step, n_steps, start_into, wait_for):
    @pl.when(step == 0)
    def _():
        start_into(0, step)

    nxt = step + 1
    for slot in range(2):
        @pl.when((nxt < n_steps) & ((nxt & 1) == slot))
        def _(slot=slot):
            start_into(slot, nxt)

    for slot in range(2):
        @pl.when((step & 1) == slot)
        def _(slot=slot):
            wait_for(slot)


def _sample_scores_kernel(pt_ref, cache_ref, qi_ref, w_ref, kinew_ref, o_ref, ki_buf, sem,
                          *, n_pages, page, past, chunk):
    b = pl.program_id(0)
    total = ki_buf.shape[1]

    def copies(slot):
        return _page_copies(pt_ref, n_pages, page, [(cache_ref, ki_buf.at[slot], sem.at[slot])])

    _prefetched(b, pl.num_programs(0), lambda slot, step: copies(slot)[0](step, 0), lambda slot: copies(slot)[1]())
    ki_all = ki_buf.at[b & 1]
    ki_all[past:past + SUBLANES, :] = kinew_ref[0]
    ki_all[past + SUBLANES:, :] = jnp.zeros((total - past - SUBLANES, IDX_DIM), F32)

    q3 = _split3_queries(qi_ref[0])
    w = w_ref[0] * (IDX_DIM ** -0.5 * IDX_HEADS ** -0.5)
    rows = qi_ref.shape[1]
    for c0 in range(0, total, chunk):
        s = _dot_nt(q3, _split3_keys(ki_all[c0:c0 + chunk, :]))
        s = jnp.maximum(s, 0.0) * w
        o_ref[0, :, c0:c0 + chunk] = jnp.sum(s.reshape(rows // IDX_HEADS, IDX_HEADS, chunk), axis=1)


def _sample_scores(page_table, cache_idx, qi_rep, w_col, ki_new, past):
    bd, n_pages = page_table.shape
    page = cache_idx.shape[1]
    total = past + LANES
    rows = qi_rep.shape[1]
    nrow = rows // IDX_HEADS
    n_chunks = 5
    chunk = total // n_chunks
    assert chunk * n_chunks == total and chunk % LANES == 0
    kern = functools.partial(_sample_scores_kernel, n_pages=n_pages, page=page, past=past, chunk=chunk)
    return pl.pallas_call(
        kern,
        out_shape=jax.ShapeDtypeStruct((bd, nrow, total), F32),
        grid_spec=pltpu.PrefetchScalarGridSpec(
            num_scalar_prefetch=1,
            grid=(bd,),
            in_specs=[
                pl.BlockSpec(memory_space=pl.ANY),
                pl.BlockSpec((1, rows, IDX_DIM), lambda b, pt: (b, 0, 0)),
                pl.BlockSpec((1, rows, 1), lambda b, pt: (b, 0, 0)),
                pl.BlockSpec((1, SUBLANES, IDX_DIM), lambda b, pt: (b, 0, 0)),
            ],
            out_specs=pl.BlockSpec((1, nrow, total), lambda b, pt: (b, 0, 0)),
            scratch_shapes=[
                pltpu.VMEM((2, total, IDX_DIM), F32),
                pltpu.SemaphoreType.DMA((2,)),
            ],
        ),
        compiler_params=_compiler_params(("arbitrary",)),
        name="sample_scores",
    )(page_table, cache_idx, qi_rep, w_col, ki_new)


def _sample_topk_kernel(sc_ref, o_ref, *, past, topk, rows_per_batch, group_rows):
    rows, total = sc_ref.shape
    kpos = lax.broadcasted_iota(I32, (rows, total), 1)
    row = lax.broadcasted_iota(I32, (rows, total), 0)
    qpos = past + ((row & (rows_per_batch - 1)) >> (group_rows.bit_length() - 1))
    keep = _topk_mask(sc_ref[...], kpos <= qpos, kpos, topk)
    o_ref[...] = jnp.where(keep, 1.0, 0.0)


def _sample_topk(scores, past, topk, rows_per_batch):
    n, total = scores.shape
    tr = LANES
    assert n % tr == 0 and tr % rows_per_batch == 0 and rows_per_batch & (rows_per_batch - 1) == 0
    kern = functools.partial(_sample_topk_kernel, past=past, topk=topk, rows_per_batch=rows_per_batch,
                             group_rows=GROUP)
    return pl.pallas_call(
        kern,
        out_shape=jax.ShapeDtypeStruct((n, total), F32),
        grid=(n // tr,),
        in_specs=[pl.BlockSpec((tr, total), lambda i: (i, 0))],
        out_specs=pl.BlockSpec((tr, total), lambda i: (i, 0)),
        compiler_params=_compiler_params(("parallel",)),
        name="sample_topk",
    )(scores)


def _sample_attend_kernel(pt_ref, ck_ref, cv_ref, q_ref, knew_ref, vnew_ref, keep_ref, tail_ref,
                          o_ref, k_buf, v_buf, m_ref, l_ref, acc_ref, sem, *, layer, half_pages, page):
    s = pl.program_id(0)
    b = s >> 1
    half_keys = half_pages * page
    new0, new1 = half_keys * N_KV_HEADS, (half_keys + SUBLANES) * N_KV_HEADS

    def copies(slot):
        return _page_copies(pt_ref, half_pages, page * N_KV_HEADS,
                            [(ck_ref.at[layer], k_buf.at[slot], sem.at[0, slot]),
                             (cv_ref.at[layer], v_buf.at[slot], sem.at[1, slot])])

    _prefetched(s, pl.num_programs(0),
                lambda slot, step: copies(slot)[0](step >> 1, (step & 1) * half_pages),
                lambda slot: copies(slot)[1]())

    def partial(half, n):
        nkeys = half_keys + (LANES if half == 1 else 0)
        kn = k_buf[half, pl.ds(n, nkeys, stride=N_KV_HEADS), :].astype(BF16)
        vn = v_buf[half, pl.ds(n, nkeys, stride=N_KV_HEADS), :].astype(BF16)
        sc = _dot_nt((q_ref[0, n] * (HEAD_DIM ** -0.5 * LOG2E)).astype(BF16), kn)
        if half == 1:
            near = nkeys - BIAS_W
            sc = jnp.concatenate([sc[:, :near], sc[:, near:] + tail_ref[n]], axis=1)
        keep = keep_ref[0, :, half * half_keys:half * half_keys + nkeys] > 0.5
        sc = jnp.where(keep, sc, NEG)
        m = jnp.max(sc, axis=1, keepdims=True)
        p = jnp.exp2(sc - m)
        return m, jnp.sum(p, axis=1, keepdims=True), _dot(p.astype(BF16), vn)

    @pl.when((s & 1) == 0)
    def _():
        for n in range(N_KV_HEADS):
            m, l, acc = partial(0, n)
            m_ref[n] = m
            l_ref[n] = l
            acc_ref[n] = acc

    @pl.when((s & 1) == 1)
    def _():
        pad = jnp.zeros(((half_keys + LANES) * N_KV_HEADS - new1, HEAD_DIM), F32)
        k_buf[1, new0:new1, :] = knew_ref[0]
        v_buf[1, new0:new1, :] = vnew_ref[0]
        k_buf[1, new1:, :] = pad
        v_buf[1, new1:, :] = pad
        for n in range(N_KV_HEADS):
            m1, l1, acc1 = partial(1, n)
            m0 = m_ref[n]
            m = jnp.maximum(m0, m1)
            a0 = jnp.exp2(m0 - m)
            a1 = jnp.exp2(m1 - m)
            o = (acc_ref[n] * a0 + acc1 * a1) / (l_ref[n] * a0 + l1 * a1)
            o_ref[0, n] = o.astype(o_ref.dtype)


def _sample_attend(page_table, cache_k, cache_v, layer, page, q_rows, k_new, v_new, keep, tail, past):
    bd, n_pages = page_table.shape
    total = past + LANES
    nrow = q_rows.shape[2]
    half_pages = n_pages // 2
    assert half_pages * 2 == n_pages and BIAS_W <= LANES + half_pages * page
    buf_rows = (half_pages * page + LANES) * N_KV_HEADS
    kern = functools.partial(_sample_attend_kernel, layer=layer, half_pages=half_pages, page=page)
    per_batch = lambda rank: (lambda s, pt: (s // 2,) + (0,) * (rank - 1))
    return pl.pallas_call(
        kern,
        out_shape=jax.ShapeDtypeStruct((bd, N_KV_HEADS, nrow, HEAD_DIM), BF16),
        grid_spec=pltpu.PrefetchScalarGridSpec(
            num_scalar_prefetch=1,
            grid=(2 * bd,),
            in_specs=[
                pl.BlockSpec(memory_space=pl.ANY),
                pl.BlockSpec(memory_space=pl.ANY),
                pl.BlockSpec((1, N_KV_HEADS, nrow, HEAD_DIM), per_batch(4)),
                pl.BlockSpec((1, SUBLANES * N_KV_HEADS, HEAD_DIM), per_batch(3)),
                pl.BlockSpec((1, SUBLANES * N_KV_HEADS, HEAD_DIM), per_batch(3)),
                pl.BlockSpec((1, nrow, total), per_batch(3)),
                pl.BlockSpec((N_KV_HEADS, nrow, BIAS_W), lambda s, pt: (0, 0, 0)),
            ],
            out_specs=pl.BlockSpec((1, N_KV_HEADS, nrow, HEAD_DIM), per_batch(4)),
            scratch_shapes=[
                pltpu.VMEM((2, buf_rows, HEAD_DIM), F32),
                pltpu.VMEM((2, buf_rows, HEAD_DIM), F32),
                pltpu.VMEM((N_KV_HEADS, nrow, 1), F32),
                pltpu.VMEM((N_KV_HEADS, nrow, 1), F32),
                pltpu.VMEM((N_KV_HEADS, nrow, HEAD_DIM), F32),
                pltpu.SemaphoreType.DMA((2, 2)),
            ],
        ),
        compiler_params=_compiler_params(("arbitrary",)),
        name="sample_attend",
    )(page_table, cache_k, cache_v, q_rows, k_new, v_new, keep, tail)


def _out_proj_kernel(x_ref, pool_ref, attn_ref, wp_ref, wa_ref, o_ref):
    o_ref[...] = x_ref[...] + (_dot(pool_ref[...], wp_ref[...]) + _dot(attn_ref[...], wa_ref[...]))


def _out_proj(x, pool, attn, w_out, tm, tn):
    m, d = x.shape
    return pl.pallas_call(
        _out_proj_kernel,
        out_shape=jax.ShapeDtypeStruct((m, d), F32),
        grid=(m // tm, d // tn),
        in_specs=[
            pl.BlockSpec((tm, tn), lambda i, j: (i, j)),
            pl.BlockSpec((tm, D_POOL), lambda i, j: (i, 0)),
            pl.BlockSpec((tm, D_ATTN), lambda i, j: (i, 0)),
            pl.BlockSpec((D_POOL, tn), lambda i, j: (0, j)),
            pl.BlockSpec((D_ATTN, tn), lambda i, j: (D_POOL // D_ATTN, j)),
        ],
        out_specs=pl.BlockSpec((tm, tn), lambda i, j: (i, j)),
        compiler_params=_compiler_params(("parallel", "arbitrary")),
        name="out_proj",
    )(x, pool, attn, w_out, w_out)


def _proj_gain(g_q, g_k, g_idx_k):
    return jnp.concatenate([
        jnp.ones((D_POOL,), F32), jnp.tile(g_q, N_HEADS), jnp.tile(g_k, N_KV_HEADS),
        jnp.ones((C_KI - C_V,), F32), g_idx_k, jnp.ones((D_IN_PAD - C_WI,), F32)]).reshape(1, D_IN_PAD)


def _pad_rows(a, rows):
    return jnp.pad(a, ((0, 0), (0, rows - a.shape[1]), (0, 0)))


def kernel(x_prompt, x_sample, cache_k, cache_v, cache_idx_k, state_pool, page_table, g_ffn1, w_gate1, w_up1,
           w_down1, g_mix, w_in, g_q, g_k, g_idx_k, w_pool, pool_scale, w_out, rel_bias, g_ffn2, w_gate2,
           w_up2, w_down2):
    batch, seq, d = x_prompt.shape
    bd, t_new, _ = x_sample.shape
    depth = g_ffn1.shape[0]
    assert depth == 1 and d == D_MODEL
    past = page_table.shape[1] * cache_k.shape[2]
    mp, ms = batch * seq, bd * t_new
    tm_p, tm_s = 512, ms
    tf = 256

    xp = x_prompt.reshape(mp, d)
    xs = x_sample.reshape(ms, d)
    l = 0

    xs, wg1, wu1, wd1 = _ffn_cast(xs, g_ffn1[l], w_gate1[l], w_up1[l], w_down1[l], tf)
    xp = _ffn(xp, g_ffn1[l], wg1, wu1, wd1, tm_p)

    w_in_p = jnp.pad(w_in[l], ((0, 0), (0, D_IN_PAD - D_IN))).astype(BF16)
    gain = _proj_gain(g_q[l], g_k[l], g_idx_k[l])
    up = _proj(xp, g_mix[l], w_in_p, gain, tm_p)
    us = _proj(xs, g_mix[l], w_in_p, gain, tm_s)

    wp = w_pool[l].astype(BF16)
    scale = pool_scale[l].reshape(1, D_POOL)
    bias = _bias_tiles(rel_bias)

    pool_p = _pool_prompt(up, wp, scale, batch, seq, 256)
    attn_p = _attn_prompt(up, bias, batch, seq)

    hist = state_pool[l]
    u_pool_s = us[:, :D_POOL].reshape(bd, t_new, D_POOL)
    ext = jnp.concatenate([hist, u_pool_s], axis=1)
    pool_s = _pool_sample(ext.transpose(1, 0, 2), wp, scale, t_new)
    pool_s = pool_s.transpose(1, 0, 2).reshape(ms, D_POOL)

    us3 = us.reshape(bd, t_new, D_IN_PAD)
    rows_i = t_new * GROUP * IDX_HEADS
    qi_s = us3[:, :, C_QI:C_KI].reshape(bd, t_new, 1, IDX_HEADS, IDX_DIM)
    qi_rep = jnp.broadcast_to(qi_s, (bd, t_new, GROUP, IDX_HEADS, IDX_DIM)).reshape(bd, rows_i, IDX_DIM)
    wi_s = us3[:, :, C_WI:C_WI + IDX_HEADS].reshape(bd, t_new, 1, IDX_HEADS)
    w_col = jnp.broadcast_to(wi_s, (bd, t_new, GROUP, IDX_HEADS)).reshape(bd, rows_i, 1)
    ki_new = _pad_rows(us3[:, :, C_KI:C_WI], SUBLANES)
    k_new = _pad_rows(us3[:, :, C_K:C_V], SUBLANES)
    v_new = _pad_rows(us3[:, :, C_V:C_QI], SUBLANES)
    n_phys, page = cache_k.shape[1], cache_k.shape[2]
    topk_s = min(TOPK_MAX, (past + t_new) // 4)
    nrow = t_new * GROUP
    scores_s = _sample_scores(page_table, cache_idx_k[l], qi_rep, w_col, ki_new, past)
    keep_s = _sample_topk(scores_s.reshape(bd * nrow, past + LANES), past, topk_s, nrow)
    keep_s = keep_s.reshape(bd, nrow, past + LANES)

    q_rows = us3[:, :, C_Q:C_K].reshape(bd, t_new, N_KV_HEADS, GROUP, HEAD_DIM)
    q_rows = q_rows.transpose(0, 2, 1, 3, 4).reshape(bd, N_KV_HEADS, nrow, HEAD_DIM)
    tail = bias[:, :t_new, :].reshape(N_KV_HEADS, GROUP, t_new, BIAS_W)
    tail = tail.transpose(0, 2, 1, 3).reshape(N_KV_HEADS, nrow, BIAS_W)
    rows_shape = (depth, n_phys, page * N_KV_HEADS, HEAD_DIM)
    new_shape = (bd, SUBLANES * N_KV_HEADS, HEAD_DIM)
    attn_s = _sample_attend(page_table, cache_k.reshape(rows_shape), cache_v.reshape(rows_shape), l, page, q_rows,
                            k_new.reshape(new_shape), v_new.reshape(new_shape), keep_s, tail, past)
    attn_s = attn_s.reshape(bd, N_KV_HEADS, t_new, GROUP, HEAD_DIM).transpose(0, 2, 1, 3, 4).reshape(ms, D_ATTN)

    w_out_bf = w_out[l].astype(BF16)
    xp = _out_proj(xp, pool_p, attn_p, w_out_bf, tm_p, 1024)
    xs = _out_proj(xs, pool_s, attn_s, w_out_bf, tm_s, 1024)

    xs, wg2, wu2, wd2 = _ffn_cast(xs, g_ffn2[l], w_gate2[l], w_up2[l], w_down2[l], tf)
    xp = _ffn(xp, g_ffn2[l], wg2, wu2, wd2, tm_p)

    up4 = up.reshape(batch, seq, D_IN_PAD)
    return (
        xp.reshape(batch, seq, d),
        xs.reshape(bd, t_new, d),
        up4[:, :, C_K:C_V].reshape(1, batch, seq, N_KV_HEADS, HEAD_DIM),
        up4[:, :, C_V:C_QI].reshape(1, batch, seq, N_KV_HEADS, HEAD_DIM),
        up4[:, :, C_KI:C_WI].reshape(1, batch, seq, IDX_DIM),
        up4[:, seq - POOL_HIST:, :D_POOL].reshape(1, batch, POOL_HIST, D_POOL),
        us3[:, :, C_K:C_V].reshape(1, bd, t_new, N_KV_HEADS, HEAD_DIM),
        us3[:, :, C_V:C_QI].reshape(1, bd, t_new, N_KV_HEADS, HEAD_DIM),
        us3[:, :, C_KI:C_WI].reshape(1, bd, t_new, IDX_DIM),
        ext[:, t_new:].reshape(1, bd, POOL_HIST, D_POOL),
    )
```
